```python
import jax, jax.numpy as jnp
from jax import lax
import numpy as np

D_MODEL = 1024
BATCH = 2
SEQ = 8192
DEPTH = 2

ROPE_THETA = 10000.0
Q_BLOCK = 128
NORM_EPS = 1e-6
SUBLN_EPS = 1e-5

DA_HEAD_DIM = 64
DA_V_DIM = 2 * DA_HEAD_DIM
DA_WIDTH = D_MODEL // 2
DA_HEADS = DA_WIDTH // DA_V_DIM

MLA_V_DIM = 128
MLA_WIDTH = D_MODEL - DA_WIDTH
MLA_HEADS = MLA_WIDTH // MLA_V_DIM
MLA_Q_RANK = D_MODEL // 4
MLA_KV_RANK = D_MODEL // 8
MLA_NOPE_DIM = 64
MLA_ROPE_DIM = 32
MLA_QK_DIM = MLA_NOPE_DIM + MLA_ROPE_DIM

MIX_WIDTH = DA_WIDTH + MLA_WIDTH

IN_SIZES = (
    DA_HEADS * 2 * DA_HEAD_DIM,
    DA_HEADS * 2 * DA_HEAD_DIM,
    DA_HEADS * DA_V_DIM,
    MLA_Q_RANK,
    MLA_KV_RANK,
    MLA_ROPE_DIM,
)
IN_COLS = sum(IN_SIZES)
IN_SPLITS = [int(v) for v in np.cumsum(IN_SIZES)[:-1]]

FF_DENSE = 2816
N_EXPERTS = 8
TOP_K = 2
FF_EXPERT = (7 * D_MODEL) // 2
N_DENSE = (DEPTH + 1) // 2
N_MOE = DEPTH // 2

kernel_name = "hybrid_diffattn_mla_moe_encoder"


def rmsnorm(x, g, eps=NORM_EPS):
    xf = x.astype(jnp.float32)
    y = xf * lax.rsqrt(jnp.mean(xf * xf, axis=-1, keepdims=True) + eps)
    return (y * g.astype(jnp.float32)).astype(x.dtype)


def rope_tables(positions, dim):
    inv = 1.0 / (ROPE_THETA ** (jnp.arange(0, dim, 2, dtype=jnp.float32) / dim))
    ang = positions.astype(jnp.float32)[..., None] * inv
    return jnp.cos(ang), jnp.sin(ang)


def apply_rope(x, cos, sin):
    extra = x.ndim - 3
    shp = cos.shape[:2] + (1,) * extra + cos.shape[-1:]
    c, s = cos.reshape(shp), sin.reshape(shp)
    xf = x.astype(jnp.float32)
    half = x.shape[-1] // 2
    x1, x2 = xf[..., :half], xf[..., half:]
    return jnp.concatenate([x1 * c - x2 * s, x1 * s + x2 * c], axis=-1).astype(x.dtype)


def map_query_blocks(fn, q):
    b, s = q.shape[:2]
    nb = s // Q_BLOCK
    qb = jnp.moveaxis(q.reshape((b, nb, Q_BLOCK) + q.shape[2:]), 1, 0)
    out = lax.map(fn, qb)
    out = jnp.moveaxis(out, 0, 1)
    return out.reshape((b, s) + out.shape[3:])


def diff_attention(q, k, v, lam):
    scale = DA_HEAD_DIM ** -0.5
    kf = k.astype(jnp.float32)
    vf = v.astype(jnp.float32)

    def block(qb):
        s = jnp.einsum('bqhcd,bkhcd->bhcqk', qb.astype(jnp.float32), kf) * scale
        p = jax.nn.softmax(s, axis=-1)
        w = p[:, :, 0] - lam * p[:, :, 1]
        return jnp.einsum('bhqk,bkhd->bqhd', w, vf)

    return map_query_blocks(block, q)


def softmax_attention(q, k, v, scale):
    kf = k.astype(jnp.float32)
    vf = v.astype(jnp.float32)

    def block(qb):
        s = jnp.einsum('bqhd,bkhd->bhqk', qb.astype(jnp.float32), kf) * scale
        p = jax.nn.softmax(s, axis=-1)
        return jnp.einsum('bhqk,bkhd->bqhd', p, vf)

    return map_query_blocks(block, q)


def swiglu(h, w_gate, w_up, w_down):
    return (jax.nn.silu(h @ w_gate) * (h @ w_up)) @ w_down


def moe_swiglu(h, router, w_gate, w_up, w_down):
    b, s, d = h.shape
    t = h.reshape(b * s, d)
    logits = (t @ router).astype(jnp.float32)
    top_v, top_i = lax.top_k(logits, TOP_K)
    gates = jax.nn.softmax(top_v, axis=-1)
    combine = jnp.sum(jax.nn.one_hot(top_i, N_EXPERTS, dtype=jnp.float32)
                      * gates[..., None], axis=1)
    y = jnp.zeros((b * s, d), jnp.float32)
    for e in range(N_EXPERTS):
        ye = swiglu(t, w_gate[e], w_up[e], w_down[e]).astype(jnp.float32)
        y = y + combine[:, e:e + 1] * ye
    return y.reshape(b, s, d).astype(h.dtype)


def setup_inputs(seed: int = 0) -> dict:
    key = jax.random.key(seed)
    ks = jax.random.split(key, 24)
    f32 = jnp.float32

    def nrm(k, shape, fan_in):
        return jax.random.normal(k, shape, f32) * (fan_in ** -0.5)

    def gain(k, shape):
        return 1.0 + 0.01 * jax.random.normal(k, shape, f32)

    return {
        "x": jax.random.normal(ks[0], (BATCH, SEQ, D_MODEL), f32),
        "positions": jnp.tile(jnp.arange(SEQ, dtype=jnp.int32)[None, :], (BATCH, 1)),
        "w_in": nrm(ks[1], (DEPTH, D_MODEL, IN_COLS), D_MODEL),
        "w_out": nrm(ks[2], (DEPTH, MIX_WIDTH, D_MODEL), MIX_WIDTH),
        "norm_mix": gain(ks[3], (DEPTH, D_MODEL)),
        "norm_ffn": gain(ks[4], (DEPTH, D_MODEL)),
        "da_lambda_q1": 0.1 * jax.random.normal(ks[5], (DEPTH, DA_HEAD_DIM), f32),
        "da_lambda_k1": 0.1 * jax.random.normal(ks[6], (DEPTH, DA_HEAD_DIM), f32),
        "da_lambda_q2": 0.1 * jax.random.normal(ks[7], (DEPTH, DA_HEAD_DIM), f32),
        "da_lambda_k2": 0.1 * jax.random.normal(ks[8], (DEPTH, DA_HEAD_DIM), f32),
        "da_subln": gain(ks[9], (DEPTH, DA_V_DIM)),
        "mla_q_norm": gain(ks[10], (DEPTH, MLA_Q_RANK)),
        "mla_w_uq": nrm(ks[11], (DEPTH, MLA_Q_RANK, MLA_HEADS * MLA_QK_DIM), MLA_Q_RANK),
        "mla_kv_norm": gain(ks[12], (DEPTH, MLA_KV_RANK)),
        "mla_w_ukv": nrm(ks[13], (DEPTH, MLA_KV_RANK, MLA_HEADS * (MLA_NOPE_DIM + MLA_V_DIM)), MLA_KV_RANK),
        "ffn_w_gate": nrm(ks[14], (N_DENSE, D_MODEL, FF_DENSE), D_MODEL),
        "ffn_w_up": nrm(ks[15], (N_DENSE, D_MODEL, FF_DENSE), D_MODEL),
        "ffn_w_down": nrm(ks[16], (N_DENSE, FF_DENSE, D_MODEL), FF_DENSE),
        "moe_router": nrm(ks[17], (N_MOE, D_MODEL, N_EXPERTS), D_MODEL),
        "moe_w_gate": nrm(ks[18], (N_MOE, N_EXPERTS, D_MODEL, FF_EXPERT), D_MODEL),
        "moe_w_up": nrm(ks[19], (N_MOE, N_EXPERTS, D_MODEL, FF_EXPERT), D_MODEL),
        "moe_w_down": nrm(ks[20], (N_MOE, N_EXPERTS, FF_EXPERT, D_MODEL), FF_EXPERT),
        "norm_final": gain(ks[21], (D_MODEL,)),
    }


def reference(x, positions, w_in, w_out, norm_mix, norm_ffn,
              da_lambda_q1, da_lambda_k1, da_lambda_q2, da_lambda_k2, da_subln,
              mla_q_norm, mla_w_uq, mla_kv_norm, mla_w_ukv,
              ffn_w_gate, ffn_w_up, ffn_w_down,
              moe_router, moe_w_gate, moe_w_up, moe_w_down, norm_final):
    b, s, _ = x.shape
    cos_da, sin_da = rope_tables(positions, DA_HEAD_DIM)
    cos_ml, sin_ml = rope_tables(positions, MLA_ROPE_DIM)

    for l in range(DEPTH):
        h = rmsnorm(x, norm_mix[l])
        proj = h @ w_in[l]
        da_q, da_k, da_v, c_q, c_kv, k_rope = jnp.split(proj, IN_SPLITS, axis=-1)

        da_q = apply_rope(da_q.reshape(b, s, DA_HEADS, 2, DA_HEAD_DIM), cos_da, sin_da)
        da_k = apply_rope(da_k.reshape(b, s, DA_HEADS, 2, DA_HEAD_DIM), cos_da, sin_da)
        da_v = da_v.reshape(b, s, DA_HEADS, DA_V_DIM)
        lambda_init = 0.8 - 0.6 * float(np.exp(-0.3 * l))
        lam = (jnp.exp(jnp.sum(da_lambda_q1[l].astype(jnp.float32) * da_lambda_k1[l].astype(jnp.float32)))
               - jnp.exp(jnp.sum(da_lambda_q2[l].astype(jnp.float32) * da_lambda_k2[l].astype(jnp.float32)))
               + lambda_init)
        o_da = diff_attention(da_q, da_k, da_v, lam)
        o_da = rmsnorm(o_da, da_subln[l], SUBLN_EPS) * (1.0 - lambda_init)
        o_da = o_da.reshape(b, s, DA_WIDTH).astype(x.dtype)

        q = (rmsnorm(c_q, mla_q_norm[l]) @ mla_w_uq[l]).reshape(b, s, MLA_HEADS, MLA_QK_DIM)
        q = jnp.concatenate([q[..., :MLA_NOPE_DIM],
                             apply_rope(q[..., MLA_NOPE_DIM:], cos_ml, sin_ml)], axis=-1)
        kv = (rmsnorm(c_kv, mla_kv_norm[l]) @ mla_w_ukv[l]).reshape(
            b, s, MLA_HEADS, MLA_NOPE_DIM + MLA_V_DIM)
        k_nope, v_ml = kv[..., :MLA_NOPE_DIM], kv[..., MLA_NOPE_DIM:]
        k_pe = apply_rope(k_rope[:, :, None, :], cos_ml, sin_ml)
        k = jnp.concatenate([k_nope, jnp.broadcast_to(k_pe, (b, s, MLA_HEADS, MLA_ROPE_DIM))], axis=-1)
        o_ml = softmax_attention(q, k, v_ml, MLA_QK_DIM ** -0.5)
        o_ml = o_ml.reshape(b, s, MLA_WIDTH).astype(x.dtype)

        x = x + jnp.concatenate([o_da, o_ml], axis=-1) @ w_out[l]

        h = rmsnorm(x, norm_ffn[l])
        if l % 2 == 0:
            i = l // 2
            x = x + swiglu(h, ffn_w_gate[i], ffn_w_up[i], ffn_w_down[i])
        else:
            i = l // 2
            x = x + moe_swiglu(h, moe_router[i], moe_w_gate[i], moe_w_up[i], moe_w_down[i])

    return rmsnorm(x, norm_final)
```

```python
import functools
import math

import numpy as np
import jax
import jax.numpy as jnp
from jax import lax
from jax.experimental import pallas as pl
from jax.experimental.pallas import tpu as pltpu

D_MODEL = 1024
ROPE_THETA = 10000.0
NORM_EPS = 1e-6
SUBLN_EPS = 1e-5
DA_HEAD_DIM = 64
DA_V_DIM = 128
DA_HEADS = 4
DA_WIDTH = 512
MLA_HEADS = 4
MLA_V_DIM = 128
MLA_WIDTH = 512
MLA_Q_RANK = 256
MLA_KV_RANK = 128
MLA_NOPE_DIM = 64
MLA_ROPE_DIM = 32
MLA_QK_DIM = 96
N_EXPERTS = 8
TOP_K = 2
LOG2E = 1.4426950408889634

LANES = 128
VMEM_LIMIT = 56 * 1024 * 1024

TM_PREP = 512
TQ = 256
TKC = TM_PREP
TM_FFN = 512
TM_EXP = 512
TM_ROW = 256
TM_ROUTE = 512

BF16 = jnp.bfloat16
F32 = jnp.float32


def _cparams(sem):
    return pltpu.CompilerParams(dimension_semantics=sem, vmem_limit_bytes=VMEM_LIMIT)


def _rms(x, g, eps):
    return x * lax.rsqrt(jnp.mean(x * x, axis=-1, keepdims=True) + eps) * g


def _rope_table_kernel(pos_ref, inv_da_ref, sgn_da_ref, inv_ml_ref, sgn_ml_ref,
                       cos_da_ref, sin_da_ref, cos_ml_ref, sin_ml_ref):
    pos = pos_ref[...]
    ang_da = pos * inv_da_ref[...]
    cos_da_ref[...] = jnp.cos(ang_da)
    sin_da_ref[...] = jnp.sin(ang_da) * sgn_da_ref[...]
    ang_ml = pos * inv_ml_ref[...]
    cos_ml_ref[...] = jnp.cos(ang_ml)
    sin_ml_ref[...] = jnp.sin(ang_ml) * sgn_ml_ref[...]


def _rope_tables(positions):
    t = positions.size
    tm = min(TM_PREP, t)
    pos = positions.reshape(t, 1).astype(F32)
    lane = np.arange(LANES)
    inv_da = 1.0 / (ROPE_THETA ** (jnp.arange(0, DA_HEAD_DIM, 2, dtype=F32) / DA_HEAD_DIM))
    inv_ml = 1.0 / (ROPE_THETA ** (jnp.arange(0, MLA_ROPE_DIM, 2, dtype=F32) / MLA_ROPE_DIM))
    half_da = DA_HEAD_DIM // 2
    half_ml = MLA_ROPE_DIM // 2
    inv_da_l = inv_da[lane % half_da][None, :]
    sgn_da = jnp.asarray(np.where(lane % DA_HEAD_DIM < half_da, -1.0, 1.0), F32)[None, :]
    in_rope = (lane >= MLA_NOPE_DIM) & (lane < MLA_QK_DIM)
    inv_ml_l = jnp.where(in_rope, inv_ml[(lane - MLA_NOPE_DIM) % half_ml], 0.0)[None, :]
    sgn_ml = jnp.asarray(
        np.where(in_rope, np.where(lane < MLA_NOPE_DIM + half_ml, -1.0, 1.0), 0.0), F32)[None, :]
    row = pl.BlockSpec((tm, LANES), lambda i: (i, 0))
    const = pl.BlockSpec((1, LANES), lambda i: (0, 0))
    out = jax.ShapeDtypeStruct((t, LANES), F32)
    return pl.pallas_call(
        _rope_table_kernel,
        grid=(t // tm,),
        in_specs=[pl.BlockSpec((tm, 1), lambda i: (i, 0)), const, const, const, const],
        out_specs=[row, row, row, row],
        out_shape=[out, out, out, out],
        compiler_params=_cparams(("parallel",)),
        name="rope_tables",
    )(pos, inv_da_l, sgn_da, inv_ml_l, sgn_ml)


def _rot_da(blk, lane):
    fwd = pltpu.roll(blk, LANES - DA_HEAD_DIM // 2, 1)
    bwd = pltpu.roll(blk, DA_HEAD_DIM // 2, 1)
    return jnp.where(lane % DA_HEAD_DIM < DA_HEAD_DIM // 2, fwd, bwd)


def _rot_ml(blk, lane):
    fwd = pltpu.roll(blk, LANES - MLA_ROPE_DIM // 2, 1)
    bwd = pltpu.roll(blk, MLA_ROPE_DIM // 2, 1)
    return jnp.where(lane < MLA_NOPE_DIM + MLA_ROPE_DIM // 2, fwd, bwd)


def _prep_kernel(x_ref, g_ref, wqk_ref, wv_ref, wc_ref, gq_ref, wuq_ref, gkv_ref, wuk_ref, wuv_ref,
                 cda_ref, sda_ref, cml_ref, sml_ref,
                 daq_ref, dak_ref, davt_ref, mlq_ref, mlk_ref, mlvt_ref):
    tm = x_ref.shape[0]
    lane = lax.broadcasted_iota(jnp.int32, (tm, LANES), 1)
    hb = _rms(x_ref[...], g_ref[...], NORM_EPS).astype(BF16)

    qk = jnp.dot(hb, wqk_ref[...], preferred_element_type=F32)
    cda, sda = cda_ref[...], sda_ref[...]
    q_scale = DA_HEAD_DIM ** -0.5 * LOG2E
    for j in range(2 * DA_HEADS):
        blk = qk[:, j * LANES:(j + 1) * LANES]
        r = blk * cda + _rot_da(blk, lane) * sda
        if j < DA_HEADS:
            daq_ref[:, j * LANES:(j + 1) * LANES] = (r * q_scale).astype(BF16)
        else:
            jj = j - DA_HEADS
            dak_ref[:, jj * LANES:(jj + 1) * LANES] = r.astype(BF16)
    v = jnp.dot(hb, wv_ref[...], preferred_element_type=F32)
    davt_ref[...] = v.T.astype(BF16)

    c = jnp.dot(hb, wc_ref[...], preferred_element_type=F32)
    cml, sml = cml_ref[...], sml_ref[...]
    cq = _rms(c[:, :MLA_Q_RANK], gq_ref[...], NORM_EPS).astype(BF16)
    qm = jnp.dot(cq, wuq_ref[...], preferred_element_type=F32)
    ml_scale = MLA_QK_DIM ** -0.5 * LOG2E
    for j in range(MLA_HEADS):
        blk = qm[:, j * LANES:(j + 1) * LANES]
        r = blk * cml + _rot_ml(blk, lane) * sml
        mlq_ref[:, j * LANES:(j + 1) * LANES] = (r * ml_scale).astype(BF16)
    ckv = _rms(c[:, MLA_Q_RANK:MLA_Q_RANK + MLA_KV_RANK], gkv_ref[...], NORM_EPS).astype(BF16)
    kr = c[:, MLA_Q_RANK + MLA_KV_RANK:]
    kr = kr * cml + _rot_ml(kr, lane) * sml
    kn = jnp.dot(ckv, wuk_ref[...], preferred_element_type=F32)
    for j in range(MLA_HEADS):
        mlk_ref[:, j * LANES:(j + 1) * LANES] = (kn[:, j * LANES:(j + 1) * LANES] + kr).astype(BF16)
    vm = jnp.dot(ckv, wuv_ref[...], preferred_element_type=F32)
    mlvt_ref[...] = vm.T.astype(BF16)


def _prep(x2, g_mix, w, tabs, batch, seq):
    t = x2.shape[0]
    tm = min(TM_PREP, seq)
    nps = seq // tm
    row = lambda width: pl.BlockSpec((tm, width), lambda i: (i, 0))
    full = lambda a: pl.BlockSpec(a.shape, lambda i: (0,) * a.ndim)
    vt_spec = pl.BlockSpec((None, None, DA_WIDTH, tm), lambda i: (i // nps, i % nps, 0, 0))
    tok = jax.ShapeDtypeStruct((t, DA_WIDTH), BF16)
    vts = jax.ShapeDtypeStruct((batch, nps, DA_WIDTH, tm), BF16)
    weights = [w["wqk"], w["wv"], w["wc"], w["gq"], w["wuq"], w["gkv"], w["wuk"], w["wuv"]]
    return pl.pallas_call(
        _prep_kernel,
        grid=(t // tm,),
        in_specs=[row(D_MODEL), full(g_mix)] + [full(a) for a in weights] + [row(LANES)] * 4,
        out_specs=[row(DA_WIDTH), row(DA_WIDTH), vt_spec, row(DA_WIDTH), row(DA_WIDTH), vt_spec],
        out_shape=[tok, tok, vts, tok, tok, vts],
        compiler_params=_cparams(("parallel",)),
        name="prep",
    )(x2, g_mix, *weights, *tabs)


def _attn_kernel(*refs, n_maps, one_minus_lambda_init):
    if n_maps == 2:
        lam_ref, q_ref, k_ref, vt_ref, g_ref, o_ref = refs
    else:
        q_ref, k_ref, vt_ref, o_ref = refs
    tq = q_ref.shape[0]
    n_chunks, _, tkc = vt_ref.shape
    q = q_ref[...]
    if n_maps == 2:
        lane = lax.broadcasted_iota(jnp.int32, q.shape, 1)
        zero = jnp.zeros_like(q)
        qs = [jnp.where(lane < DA_HEAD_DIM, q, zero), jnp.where(lane >= DA_HEAD_DIM, q, zero)]
    else:
        qs = [q]

    def chunk(c, carry):
        start = pl.multiple_of(c * tkc, tkc)
        k = k_ref[pl.ds(start, tkc), :]
        vt = vt_ref[c]
        new = []
        for qm, (m, l, acc) in zip(qs, carry):
            s = lax.dot_general(k, qm, (((1,), (1,)), ((), ())), preferred_element_type=F32)
            m_new = jnp.maximum(m, jnp.max(s, axis=0, keepdims=True))
            alpha = jnp.exp2(m - m_new)
            p = jnp.exp2(s - m_new)
            l_new = alpha * l + jnp.sum(p, axis=0, keepdims=True)
            acc_new = alpha * acc + jnp.dot(vt, p.astype(BF16), preferred_element_type=F32)
            new.append((m_new, l_new, acc_new))
        return tuple(new)

    init = tuple((jnp.full((1, tq), -jnp.inf, F32), jnp.zeros((1, tq), F32),
                  jnp.zeros((DA_V_DIM, tq), F32)) for _ in qs)
    res = lax.fori_loop(0, n_chunks, chunk, init)
    if n_maps == 2:
        (_, l1, a1), (_, l2, a2) = res
        o = a1 / l1 - lam_ref[0] * (a2 / l2)
        ms = jnp.mean(o * o, axis=0, keepdims=True)
        o = o * lax.rsqrt(ms + SUBLN_EPS) * g_ref[...] * one_minus_lambda_init
    else:
        ((_, l1, a1),) = res
        o = a1 / l1
    o_ref[...] = o.T.astype(o_ref.dtype)


def _attention(q, k, vt, batch, seq, n_maps, lam=None, subln=None, one_minus_lambda_init=1.0):
    t = q.shape[0]
    tq = min(TQ, seq)
    nq = seq // tq
    heads = q.shape[1] // LANES
    q_spec = pl.BlockSpec((tq, LANES), lambda b, h, i: (b * nq + i, h))
    k_spec = pl.BlockSpec((seq, LANES), lambda b, h, i: (b, h))
    vt_spec = pl.BlockSpec((None, vt.shape[1], LANES, vt.shape[3]), lambda b, h, i: (b, 0, h, 0))
    in_specs = [q_spec, k_spec, vt_spec]
    args = [q, k, vt]
    if n_maps == 2:
        in_specs = [pl.BlockSpec(memory_space=pltpu.SMEM)] + in_specs + [
            pl.BlockSpec((DA_V_DIM, 1), lambda b, h, i: (0, 0))]
        args = [lam] + args + [subln]
    return pl.pallas_call(
        functools.partial(_attn_kernel, n_maps=n_maps, one_minus_lambda_init=one_minus_lambda_init),
        grid=(batch, heads, nq),
        in_specs=in_specs,
        out_specs=q_spec,
        out_shape=jax.ShapeDtypeStruct((t, q.shape[1]), BF16),
        compiler_params=_cparams(("parallel", "parallel", "parallel")),
        name="diff_attn" if n_maps == 2 else "mla_attn",
    )(*args)


def _outproj_kernel(x_ref, oda_ref, oml_ref, wa_ref, wb_ref, g_ref, xo_ref, h_ref):
    y = x_ref[...] + jnp.dot(oda_ref[...], wa_ref[...], preferred_element_type=F32)
    y = y + jnp.dot(oml_ref[...], wb_ref[...], preferred_element_type=F32)
    xo_ref[...] = y
    h_ref[...] = _rms(y, g_ref[...], NORM_EPS).astype(h_ref.dtype)


def _outproj(x2, o_da, o_ml, w_a, w_b, g_ffn, h_dtype):
    t = x2.shape[0]
    tm = min(TM_PREP, t)
    row = lambda width: pl.BlockSpec((tm, width), lambda i: (i, 0))
    full = lambda a: pl.BlockSpec(a.shape, lambda i: (0,) * a.ndim)
    return pl.pallas_call(
        _outproj_kernel,
        grid=(t // tm,),
        in_specs=[row(D_MODEL), row(DA_WIDTH), row(MLA_WIDTH), full(w_a), full(w_b), full(g_ffn)],
        out_specs=[row(D_MODEL), row(D_MODEL)],
        out_shape=[jax.ShapeDtypeStruct((t, D_MODEL), F32), jax.ShapeDtypeStruct((t, D_MODEL), h_dtype)],
        compiler_params=_cparams(("parallel",)),
        name="outproj",
    )(x2, o_da, o_ml, w_a, w_b, g_ffn)


def _ffn_kernel(x_ref, h_ref, wg_ref, wu_ref, wd_ref, o_ref):
    h = h_ref[...]
    g = jnp.dot(h, wg_ref[...], preferred_element_type=F32)
    u = jnp.dot(h, wu_ref[...], preferred_element_type=F32)
    a = (g * jax.nn.sigmoid(g) * u).astype(BF16)
    o_ref[...] = x_ref[...] + jnp.dot(a, wd_ref[...], preferred_element_type=F32)


def _dense_ffn(x2, h, wg, wu, wd):
    t = x2.shape[0]
    tm = min(TM_FFN, t)
    row = pl.BlockSpec((tm, D_MODEL), lambda i: (i, 0))
    res = lambda a: pl.BlockSpec(a.shape, lambda i: (0, 0), pipeline_mode=pl.Buffered(1))
    return pl.pallas_call(
        _ffn_kernel,
        grid=(t // tm,),
        in_specs=[row, row, res(wg), res(wu), res(wd)],
        out_specs=row,
        out_shape=jax.ShapeDtypeStruct((t, D_MODEL), F32),
        compiler_params=_cparams(("parallel",)),
        name="dense_ffn",
    )(x2, h, wg, wu, wd)


def _router_kernel(h_ref, wr_ref, gates_ref, rank_ref, cnt_ref):
    tm = h_ref.shape[0]

    @pl.when(pl.program_id(0) == 0)
    def _():
        cnt_ref[...] = jnp.zeros_like(cnt_ref)

    lane = lax.broadcasted_iota(jnp.int32, (tm, LANES), 1)
    logits = jnp.dot(h_ref[...], wr_ref[...], preferred_element_type=F32, precision=lax.Precision.HIGHEST)
    neg = jnp.float32(-jnp.inf)
    logits = jnp.where(lane < N_EXPERTS, logits, neg)
    v1 = jnp.max(logits, axis=1, keepdims=True)
    i1 = jnp.min(jnp.where(logits == v1, lane, LANES), axis=1, keepdims=True)
    sel1 = lane == i1
    rest = jnp.where(sel1, neg, logits)
    v2 = jnp.max(rest, axis=1, keepdims=True)
    i2 = jnp.min(jnp.where(rest == v2, lane, LANES), axis=1, keepdims=True)
    sel2 = lane == i2
    e = jnp.exp(v2 - v1)
    g1 = 1.0 / (1.0 + e)
    g2 = e / (1.0 + e)
    gates_ref[...] = jnp.where(sel1, g1, jnp.where(sel2, g2, 0.0))
    sel = jnp.where(sel1 | sel2, 1.0, 0.0)
    r_i = lax.broadcasted_iota(jnp.int32, (tm, tm), 0)
    c_i = lax.broadcasted_iota(jnp.int32, (tm, tm), 1)
    tri = jnp.where(c_i < r_i, 1.0, 0.0).astype(BF16)
    before = jnp.dot(tri, sel.astype(BF16), preferred_element_type=F32) + cnt_ref[0:1, :]
    rank_ref[...] = jnp.where(sel > 0, before, -1.0).astype(jnp.int32)
    cnt_ref[...] = cnt_ref[...] + jnp.sum(sel, axis=0, keepdims=True)


def _router(h, w_router_pad):
    t = h.shape[0]
    tm = min(TM_ROUTE, t)
    row = lambda width: pl.BlockSpec((tm, width), lambda i: (i, 0))
    return pl.pallas_call(
        _router_kernel,
        grid=(t // tm,),
        in_specs=[row(D_MODEL), pl.BlockSpec(w_router_pad.shape, lambda i: (0, 0))],
        out_specs=[row(LANES), row(LANES), pl.BlockSpec((8, LANES), lambda i: (0, 0))],
        out_shape=[jax.ShapeDtypeStruct((t, LANES), F32), jax.ShapeDtypeStruct((t, LANES), jnp.int32),
                   jax.ShapeDtypeStruct((8, LANES), F32)],
        compiler_params=_cparams(("arbitrary",)),
        name="router",
    )(h, w_router_pad)


def _scatter_kernel(ends_ref, dest_ref, h_ref, xs_ref, zero_ref, sem):
    tm = h_ref.shape[0]

    def row_copy(src, dst_row):
        return pltpu.make_async_copy(src, xs_ref.at[pl.ds(dst_row, 1), :], sem)

    @pl.when(pl.program_id(0) == 0)
    def _():
        zero_ref[...] = jnp.zeros_like(zero_ref)
        copies = []
        for e in range(N_EXPERTS):
            start = pl.multiple_of(jnp.maximum(ends_ref[e] - TM_EXP, 0), TM_EXP)
            cp = pltpu.make_async_copy(zero_ref, xs_ref.at[pl.ds(start, TM_EXP), :], sem)
            cp.start()
            copies.append(cp)
        for cp in copies:
            cp.wait()

        def zero_tail(i, _):
            cp = pltpu.make_async_copy(zero_ref, xs_ref.at[pl.ds(pl.multiple_of(i * TM_EXP, TM_EXP), TM_EXP), :], sem)
            cp.start()
            cp.wait()
            return 0

        lax.fori_loop(ends_ref[N_EXPERTS - 1] // TM_EXP, xs_ref.shape[0] // TM_EXP, zero_tail, 0)

    def issue(r, _):
        src = h_ref.at[pl.ds(r, 1), :]
        row_copy(src, dest_ref[0, 2 * r]).start()
        row_copy(src, dest_ref[0, 2 * r + 1]).start()
        return 0

    lax.fori_loop(0, tm, issue, 0)

    def drain(r, _):
        src = h_ref.at[pl.ds(r, 1), :]
        row_copy(src, dest_ref[0, 2 * r]).wait()
        row_copy(src, dest_ref[0, 2 * r + 1]).wait()
        return 0

    lax.fori_loop(0, tm, drain, 0)


def _scatter_rows(h, dest2, ends, n_rows):
    t = h.shape[0]
    tm = min(TM_ROW, t)
    nt = t // tm
    dest3 = dest2.reshape(nt, 1, 2 * tm)
    grid_spec = pltpu.PrefetchScalarGridSpec(
        num_scalar_prefetch=1,
        grid=(nt,),
        in_specs=[pl.BlockSpec((None, 1, 2 * tm), lambda i, ends: (i, 0, 0), memory_space=pltpu.SMEM),
                  pl.BlockSpec((tm, D_MODEL), lambda i, ends: (i, 0))],
        out_specs=pl.BlockSpec(memory_space=pl.ANY),
        scratch_shapes=[pltpu.VMEM((TM_EXP, D_MODEL), F32), pltpu.SemaphoreType.DMA(())],
    )
    return pl.pallas_call(
        _scatter_kernel,
        grid_spec=grid_spec,
        out_shape=jax.ShapeDtypeStruct((n_rows, D_MODEL), F32),
        compiler_params=_cparams(("arbitrary",)),
        name="scatter_rows",
    )(ends, dest3, h)


def _expert_kernel(te_ref, na_ref, xs_ref, wg_ref, wu_ref, wd_ref, ys_ref, acc_ref):
    i = pl.program_id(0)
    j = pl.program_id(1)

    @pl.when(i < na_ref[0])
    def _():
        xb = xs_ref[...].astype(BF16)
        g = jnp.dot(xb, wg_ref[...], preferred_element_type=F32)
        u = jnp.dot(xb, wu_ref[...], preferred_element_type=F32)
        a = (g * jax.nn.sigmoid(g) * u).astype(BF16)
        part = jnp.dot(a, wd_ref[...], preferred_element_type=F32)

        @pl.when(j == 0)
        def _():
            acc_ref[...] = part

        @pl.when(j > 0)
        def _():
            acc_ref[...] = acc_ref[...] + part

        @pl.when(j == pl.num_programs(1) - 1)
        def _():
            ys_ref[...] = acc_ref[...]

    @pl.when(i >= na_ref[0])
    def _():
        ys_ref[...] = jnp.zeros_like(ys_ref)


def _expert_ffn(xs, tile_expert, n_active, wg, wu, wd):
    n_rows = xs.shape[0]
    ff = wg.shape[2]
    tf = ff // 2
    nj = ff // tf
    nt = n_rows // TM_EXP

    def wj(i, j, na):
        return jnp.where(i < na[0], j, nj - 1)

    row = pl.BlockSpec((TM_EXP, D_MODEL), lambda i, j, te, na: (i, 0))
    grid_spec = pltpu.PrefetchScalarGridSpec(
        num_scalar_prefetch=2,
        grid=(nt, nj),
        in_specs=[row,
                  pl.BlockSpec((None, D_MODEL, tf), lambda i, j, te, na: (te[i], 0, wj(i, j, na))),
                  pl.BlockSpec((None, D_MODEL, tf), lambda i, j, te, na: (te[i], 0, wj(i, j, na))),
                  pl.BlockSpec((None, tf, D_MODEL), lambda i, j, te, na: (te[i], wj(i, j, na), 0))],
        out_specs=row,
        scratch_shapes=[pltpu.VMEM((TM_EXP, D_MODEL), F32)],
    )
    return pl.pallas_call(
        _expert_kernel,
        grid_spec=grid_spec,
        out_shape=jax.ShapeDtypeStruct((n_rows, D_MODEL), F32),
        compiler_params=_cparams(("arbitrary", "arbitrary")),
        name="expert_ffn",
    )(tile_expert, n_active, xs, wg, wu, wd)


def _combine_kernel(dest_ref, x_ref, g12_ref, gfin_ref, ys_ref, o_ref, buf_ref, sem, *, final):
    tm = x_ref.shape[0]

    def row_copy(r, k):
        return pltpu.make_async_copy(ys_ref.at[pl.ds(dest_ref[0, 2 * r + k], 1), :],
                                     buf_ref.at[k, pl.ds(r, 1), :], sem)

    def issue(r, _):
        row_copy(r, 0).start()
        row_copy(r, 1).start()
        return 0

    lax.fori_loop(0, tm, issue, 0)

    def drain(r, _):
        row_copy(r, 0).wait()
        row_copy(r, 1).wait()
        return 0

    lax.fori_loop(0, tm, drain, 0)
    g12 = g12_ref[...]
    y = x_ref[...] + g12[:, 0:1] * buf_ref[0] + g12[:, 1:2] * buf_ref[1]
    o_ref[...] = _rms(y, gfin_ref[...], NORM_EPS) if final else y


def _combine(x2, ys, dest2, g12, g_final, final):
    t = x2.shape[0]
    tm = min(TM_ROW, t)
    nt = t // tm
    dest3 = dest2.reshape(nt, 1, 2 * tm)
    row = pl.BlockSpec((tm, D_MODEL), lambda i: (i, 0))
    return pl.pallas_call(
        functools.partial(_combine_kernel, final=final),
        grid=(nt,),
        in_specs=[pl.BlockSpec((None, 1, 2 * tm), lambda i: (i, 0, 0), memory_space=pltpu.SMEM),
                  row, pl.BlockSpec((tm, 2), lambda i: (i, 0)),
                  pl.BlockSpec((1, D_MODEL), lambda i: (0, 0)),
                  pl.BlockSpec(memory_space=pl.ANY)],
        out_specs=row,
        out_shape=jax.ShapeDtypeStruct((t, D_MODEL), F32),
        scratch_shapes=[pltpu.VMEM((2, tm, D_MODEL), F32), pltpu.SemaphoreType.DMA(())],
        compiler_params=_cparams(("arbitrary",)),
        name="combine",
    )(dest3, x2, g12, g_final, ys)


def _final_norm_kernel(x_ref, g_ref, o_ref):
    o_ref[...] = _rms(x_ref[...], g_ref[...], NORM_EPS)


def _final_norm(x2, g):
    t = x2.shape[0]
    tm = min(TM_PREP, t)
    row = pl.BlockSpec((tm, D_MODEL), lambda i: (i, 0))
    return pl.pallas_call(
        _final_norm_kernel, grid=(t // tm,),
        in_specs=[row, pl.BlockSpec((1, D_MODEL), lambda i: (0, 0))], out_specs=row,
        out_shape=jax.ShapeDtypeStruct((t, D_MODEL), F32),
        compiler_params=_cparams(("parallel",)), name="final_norm",
    )(x2, g)


def _moe_layer(x2, h, router, wg, wu, wd, g_final, final):
    t = x2.shape[0]
    w_router_pad = jnp.zeros((D_MODEL, LANES), F32).at[:, :N_EXPERTS].set(router.astype(F32))
    gates, rank, cnt = _router(h, w_router_pad)
    counts = cnt[0, :N_EXPERTS].astype(jnp.int32)
    padded = ((counts + TM_EXP - 1) // TM_EXP) * TM_EXP
    ends = jnp.cumsum(padded)
    starts = ends - padded
    n_rows = TOP_K * t + N_EXPERTS * TM_EXP
    nt = n_rows // TM_EXP
    n_active = (ends[-1] // TM_EXP).astype(jnp.int32)
    tile_start = jnp.minimum(jnp.arange(nt, dtype=jnp.int32), n_active - 1) * TM_EXP
    tile_expert = jnp.minimum(jnp.searchsorted(ends, tile_start, side="right"), N_EXPERTS - 1).astype(jnp.int32)
    rank8 = rank[:, :N_EXPERTS]
    dest8 = jnp.where(rank8 >= 0, starts[None, :] + rank8, -1)
    dest2, idx2 = lax.top_k(dest8, TOP_K)
    g12 = jnp.take_along_axis(gates[:, :N_EXPERTS], idx2, axis=1)
    dest2 = dest2.astype(jnp.int32)
    xs = _scatter_rows(h, dest2, ends.astype(jnp.int32), n_rows)
    ys = _expert_ffn(xs, tile_expert, n_active.reshape(1), wg, wu, wd)
    return _combine(x2, ys, dest2, g12, g_final, final)


def kernel(x, positions, w_in, w_out, norm_mix, norm_ffn, da_lambda_q1, da_lambda_k1, da_lambda_q2,
           da_lambda_k2, da_subln, mla_q_norm, mla_w_uq, mla_kv_norm, mla_w_ukv, ffn_w_gate, ffn_w_up,
           ffn_w_down, moe_router, moe_w_gate, moe_w_up, moe_w_down, norm_final):
    batch, seq, d = x.shape
    depth = w_in.shape[0]
    t = batch * seq
    x2 = x.reshape(t, d).astype(F32)
    tabs = _rope_tables(positions)

    n_qk = 2 * DA_HEADS * 2 * DA_HEAD_DIM
    n_v = DA_HEADS * DA_V_DIM
    c0 = n_qk + n_v
    kv_per_head = MLA_NOPE_DIM + MLA_V_DIM

    for l in range(depth):
        wl = w_in[l]
        w_kr = jnp.zeros((d, LANES), F32).at[:, MLA_NOPE_DIM:MLA_QK_DIM].set(
            wl[:, c0 + MLA_Q_RANK + MLA_KV_RANK:])
        wc = jnp.concatenate([wl[:, c0:c0 + MLA_Q_RANK + MLA_KV_RANK], w_kr], axis=1)
        wuq = jnp.pad(mla_w_uq[l].reshape(MLA_Q_RANK, MLA_HEADS, MLA_QK_DIM),
                      ((0, 0), (0, 0), (0, LANES - MLA_QK_DIM))).reshape(MLA_Q_RANK, MLA_HEADS * LANES)
        wukv = mla_w_ukv[l].reshape(MLA_KV_RANK, MLA_HEADS, kv_per_head)
        wuk = jnp.pad(wukv[:, :, :MLA_NOPE_DIM],
                      ((0, 0), (0, 0), (0, LANES - MLA_NOPE_DIM))).reshape(MLA_KV_RANK, MLA_HEADS * LANES)
        wuv = wukv[:, :, MLA_NOPE_DIM:].reshape(MLA_KV_RANK, MLA_HEADS * MLA_V_DIM)
        w = dict(wqk=wl[:, :n_qk].astype(BF16), wv=wl[:, n_qk:c0].astype(BF16), wc=wc.astype(BF16),
                 gq=mla_q_norm[l].reshape(1, -1).astype(F32), wuq=wuq.astype(BF16),
                 gkv=mla_kv_norm[l].reshape(1, -1).astype(F32), wuk=wuk.astype(BF16), wuv=wuv.astype(BF16))

        da_q, da_k, da_vt, ml_q, ml_k, ml_vt = _prep(
            x2, norm_mix[l].reshape(1, d).astype(F32), w, tabs, batch, seq)

        lambda_init = 0.8 - 0.6 * float(np.exp(-0.3 * l))
        lam = (jnp.exp(jnp.sum(da_lambda_q1[l].astype(F32) * da_lambda_k1[l].astype(F32)))
               - jnp.exp(jnp.sum(da_lambda_q2[l].astype(F32) * da_lambda_k2[l].astype(F32)))
               + lambda_init).reshape(1).astype(F32)
        o_da = _attention(da_q, da_k, da_vt, batch, seq, 2, lam=lam,
                          subln=da_subln[l].reshape(DA_V_DIM, 1).astype(F32),
                          one_minus_lambda_init=1.0 - lambda_init)
        o_ml = _attention(ml_q, ml_k, ml_vt, batch, seq, 1)

        is_moe = l % 2 == 1
        wo = w_out[l].astype(BF16)
        x2, h = _outproj(x2, o_da, o_ml, wo[:DA_WIDTH], wo[DA_WIDTH:],
                         norm_ffn[l].reshape(1, d).astype(F32), F32 if is_moe else BF16)
        i = l // 2
        last = l == depth - 1
        if not is_moe:
            x2 = _dense_ffn(x2, h, ffn_w_gate[i].astype(BF16), ffn_w_up[i].astype(BF16),
                            ffn_w_down[i].astype(BF16))
            if last:
                x2 = _final_norm(x2, norm_final.reshape(1, d).astype(F32))
        else:
            x2 = _moe_layer(x2, h, moe_router[i], moe_w_gate[i].astype(BF16), moe_w_up[i].astype(BF16),
                            moe_w_down[i].astype(BF16), norm_final.reshape(1, d).astype(F32), last)
    return x2.reshape(batch, seq, d).astype(x.dtype)
```

```python
import functools
import math

import numpy as np
import jax
import jax.numpy as jnp
from jax import lax
from jax.experimental import pallas as pl
from jax.experimental.pallas import tpu as pltpu

D_MODEL = 1024
ROPE_THETA = 10000.0
NORM_EPS = 1e-6
SUBLN_EPS = 1e-5
DA_HEAD_DIM = 64
DA_V_DIM = 128
DA_HEADS = 4
DA_WIDTH = 512
MLA_HEADS = 4
MLA_V_DIM = 128
MLA_WIDTH = 512
MLA_Q_RANK = 256
MLA_KV_RANK = 128
MLA_NOPE_DIM = 64
MLA_ROPE_DIM = 32
MLA_QK_DIM = 96
N_EXPERTS = 8
TOP_K = 2
LOG2E = 1.4426950408889634

LANES = 128
ONES_ROWS = 16
VMEM_LIMIT = 56 * 1024 * 1024

TM_PREP = 512
TQ = 256
TKC = TM_PREP
ATTN_SLOTS = 4
ATTN_TRIP = 8
ATTN_LEADS = {1: (4, 2), 2: (2, 1)}
TM_FFN = 512
TM_EXP = 512
TM_ROW = 256
TM_ROUTE = 512

BF16 = jnp.bfloat16
F32 = jnp.float32


def _cparams(sem):
    return pltpu.CompilerParams(dimension_semantics=sem, vmem_limit_bytes=VMEM_LIMIT)


def _rms(x, g, eps):
    return x * lax.rsqrt(jnp.mean(x * x, axis=-1, keepdims=True) + eps) * g


def _rope_table_kernel(pos_ref, inv_da_ref, sgn_da_ref, inv_ml_ref, sgn_ml_ref,
                       cos_da_ref, sin_da_ref, cos_ml_ref, sin_ml_ref):
    pos = pos_ref[...]
    ang_da = pos * inv_da_ref[...]
    cos_da_ref[...] = jnp.cos(ang_da)
    sin_da_ref[...] = jnp.sin(ang_da) * sgn_da_ref[...]
    ang_ml = pos * inv_ml_ref[...]
    cos_ml_ref[...] = jnp.cos(ang_ml)
    sin_ml_ref[...] = jnp.sin(ang_ml) * sgn_ml_ref[...]


def _rope_tables(positions):
    t = positions.size
    tm = min(TM_PREP, t)
    pos = positions.reshape(t, 1).astype(F32)
    lane = np.arange(LANES)
    inv_da = 1.0 / (ROPE_THETA ** (jnp.arange(0, DA_HEAD_DIM, 2, dtype=F32) / DA_HEAD_DIM))
    inv_ml = 1.0 / (ROPE_THETA ** (jnp.arange(0, MLA_ROPE_DIM, 2, dtype=F32) / MLA_ROPE_DIM))
    half_da = DA_HEAD_DIM // 2
    half_ml = MLA_ROPE_DIM // 2
    inv_da_l = inv_da[lane % half_da][None, :]
    sgn_da = jnp.asarray(np.where(lane % DA_HEAD_DIM < half_da, -1.0, 1.0), F32)[None, :]
    in_rope = (lane >= MLA_NOPE_DIM) & (lane < MLA_QK_DIM)
    inv_ml_l = jnp.where(in_rope, inv_ml[(lane - MLA_NOPE_DIM) % half_ml], 0.0)[None, :]
    sgn_ml = jnp.asarray(
        np.where(in_rope, np.where(lane < MLA_NOPE_DIM + half_ml, -1.0, 1.0), 0.0), F32)[None, :]
    row = pl.BlockSpec((tm, LANES), lambda i: (i, 0))
    const = pl.BlockSpec((1, LANES), lambda i: (0, 0))
    out = jax.ShapeDtypeStruct((t, LANES), F32)
    return pl.pallas_call(
        _rope_table_kernel,
        grid=(t // tm,),
        in_specs=[pl.BlockSpec((tm, 1), lambda i: (i, 0)), const, const, const, const],
        out_specs=[row, row, row, row],
        out_shape=[out, out, out, out],
        compiler_params=_cparams(("parallel",)),
        name="rope_tables",
    )(pos, inv_da_l, sgn_da, inv_ml_l, sgn_ml)


def _rot_da(blk, lane):
    fwd = pltpu.roll(blk, LANES - DA_HEAD_DIM // 2, 1)
    bwd = pltpu.roll(blk, DA_HEAD_DIM // 2, 1)
    return jnp.where(lane % DA_HEAD_DIM < DA_HEAD_DIM // 2, fwd, bwd)


def _rot_ml(blk, lane):
    fwd = pltpu.roll(blk, LANES - MLA_ROPE_DIM // 2, 1)
    bwd = pltpu.roll(blk, MLA_ROPE_DIM // 2, 1)
    return jnp.where(lane < MLA_NOPE_DIM + MLA_ROPE_DIM // 2, fwd, bwd)


def _prep_kernel(x_ref, g_ref, wqk_ref, wv_ref, wc_ref, gq_ref, wuq_ref, gkv_ref, wuk_ref, wuv_ref,
                 cda_ref, sda_ref, cml_ref, sml_ref,
                 daq_ref, dak_ref, davt_ref, mlq_ref, mlk_ref, mlvt_ref):
    tm = x_ref.shape[0]
    lane = lax.broadcasted_iota(jnp.int32, (tm, LANES), 1)
    hb = _rms(x_ref[...], g_ref[...], NORM_EPS).astype(BF16)

    qk = jnp.dot(hb, wqk_ref[...], preferred_element_type=F32)
    cda, sda = cda_ref[...], sda_ref[...]
    q_scale = DA_HEAD_DIM ** -0.5 * LOG2E
    for j in range(2 * DA_HEADS):
        blk = qk[:, j * LANES:(j + 1) * LANES]
        r = blk * cda + _rot_da(blk, lane) * sda
        if j < DA_HEADS:
            daq_ref[:, j * LANES:(j + 1) * LANES] = (r * q_scale).astype(BF16)
        else:
            jj = j - DA_HEADS
            dak_ref[:, jj * LANES:(jj + 1) * LANES] = r.astype(BF16)
    v = jnp.dot(hb, wv_ref[...], preferred_element_type=F32)
    davt_ref[...] = v.T.astype(BF16)

    c = jnp.dot(hb, wc_ref[...], preferred_element_type=F32)
    cml, sml = cml_ref[...], sml_ref[...]
    cq = _rms(c[:, :MLA_Q_RANK], gq_ref[...], NORM_EPS).astype(BF16)
    qm = jnp.dot(cq, wuq_ref[...], preferred_element_type=F32)
    ml_scale = MLA_QK_DIM ** -0.5 * LOG2E
    for j in range(MLA_HEADS):
        blk = qm[:, j * LANES:(j + 1) * LANES]
        r = blk * cml + _rot_ml(blk, lane) * sml
        mlq_ref[:, j * LANES:(j + 1) * LANES] = (r * ml_scale).astype(BF16)
    ckv = _rms(c[:, MLA_Q_RANK:MLA_Q_RANK + MLA_KV_RANK], gkv_ref[...], NORM_EPS).astype(BF16)
    kr = c[:, MLA_Q_RANK + MLA_KV_RANK:]
    kr = kr * cml + _rot_ml(kr, lane) * sml
    kn = jnp.dot(ckv, wuk_ref[...], preferred_element_type=F32)
    for j in range(MLA_HEADS):
        mlk_ref[:, j * LANES:(j + 1) * LANES] = (kn[:, j * LANES:(j + 1) * LANES] + kr).astype(BF16)
    vm = jnp.dot(ckv, wuv_ref[...], preferred_element_type=F32)
    mlvt_ref[...] = vm.T.astype(BF16)


def _prep(x2, g_mix, w, tabs, batch, seq):
    t = x2.shape[0]
    tm = min(TM_PREP, seq)
    nps = seq // tm
    row = lambda width: pl.BlockSpec((tm, width), lambda i: (i, 0))
    full = lambda a: pl.BlockSpec(a.shape, lambda i: (0,) * a.ndim)
    vt_spec = pl.BlockSpec((None, None, DA_WIDTH, tm), lambda i: (i // nps, i % nps, 0, 0))
    tok = jax.ShapeDtypeStruct((t, DA_WIDTH), BF16)
    vts = jax.ShapeDtypeStruct((batch, nps, DA_WIDTH, tm), BF16)
    weights = [w["wqk"], w["wv"], w["wc"], w["gq"], w["wuq"], w["gkv"], w["wuk"], w["wuv"]]
    return pl.pallas_call(
        _prep_kernel,
        grid=(t // tm,),
        in_specs=[row(D_MODEL), full(g_mix)] + [full(a) for a in weights] + [row(LANES)] * 4,
        out_specs=[row(DA_WIDTH), row(DA_WIDTH), vt_spec, row(DA_WIDTH), row(DA_WIDTH), vt_spec],
        out_shape=[tok, tok, vts, tok, tok, vts],
        compiler_params=_cparams(("parallel",)),
        name="prep",
    )(x2, g_mix, *weights, *tabs)


def _attn_kernel(*refs, n_maps, one_minus_lambda_init):
    if n_maps == 2:
        lam_ref, q_ref, k_ref, vt_ref, g_ref, o_ref = refs[:6]
    else:
        q_ref, k_ref, vt_ref, o_ref = refs[:4]
    ns = ATTN_SLOTS
    scr = refs[-(4 * ns + 3):]
    s_scr, cm_scr, p_scr, al_scr = scr[:ns], scr[ns:2 * ns], scr[2 * ns:3 * ns], scr[3 * ns:4 * ns]
    acc_ref, m_ref, qm_ref = scr[4 * ns:]
    n_slabs, _, slab = vt_ref.shape
    tkc = slab // 2
    n_chunks = 2 * n_slabs
    q = q_ref[...]
    if n_maps == 2:
        lane = lax.broadcasted_iota(jnp.int32, q.shape, 1)
        zero = jnp.zeros_like(q)
        qm_ref[0] = jnp.where(lane < DA_HEAD_DIM, q, zero)
        qm_ref[1] = jnp.where(lane >= DA_HEAD_DIM, q, zero)
    else:
        qm_ref[0] = q
    m_ref[...] = jnp.full(m_ref.shape, -jnp.inf, F32)
    acc_ref[...] = jnp.zeros(acc_ref.shape, F32)

    def stage_a(base, off):
        slot = off % ns
        k = k_ref[pl.ds(pl.multiple_of((base + off) * tkc, tkc), tkc), :]
        for mp in range(n_maps):
            s = lax.dot_general(k, qm_ref[mp], (((1,), (1,)), ((), ())), preferred_element_type=F32)
            s_scr[slot][mp] = s
            cm_scr[slot][mp] = jnp.max(s, axis=0, keepdims=True)

    def stage_b(off):
        slot = off % ns
        for mp in range(n_maps):
            m_old = m_ref[mp]
            m_new = jnp.maximum(m_old, cm_scr[slot][mp])
            alpha = jnp.exp2(m_old - m_new)
            p = jnp.exp2(s_scr[slot][mp] - m_new)
            m_ref[mp] = m_new
            al_scr[slot][mp] = alpha
            p_scr[slot][mp] = p.astype(BF16)

    ones_rows = jnp.ones((ONES_ROWS, tkc), BF16)

    def stage_c(base, off):
        slot = off % ns
        half = off % 2
        slab_base = base // 2 if isinstance(base, int) else lax.shift_right_logical(base, 1)
        vt = jnp.concatenate([vt_ref[slab_base + off // 2, :, half * tkc:(half + 1) * tkc], ones_rows], axis=0)
        for mp in range(n_maps):
            acc_ref[mp] = al_scr[slot][mp] * acc_ref[mp] + jnp.dot(
                vt, p_scr[slot][mp], preferred_element_type=F32)

    lead_a, lead_b = ATTN_LEADS[n_maps]

    def pipeline_step(base, off):
        static = isinstance(base, int)
        in_range = lambda c: not static or 0 <= c < n_chunks
        if in_range(base + off + lead_a):
            stage_a(base, off + lead_a)
        if in_range(base + off + lead_b):
            stage_b(off + lead_b)
        if in_range(base + off):
            stage_c(base, off)

    n_main = (n_chunks - lead_a) // ATTN_TRIP * ATTN_TRIP
    for off in range(-lead_a, 0):
        pipeline_step(0, off)

    def trip(t, _):
        base = pl.multiple_of(t * ATTN_TRIP, ATTN_TRIP)
        for off in range(ATTN_TRIP):
            pipeline_step(base, off)
        return 0

    if n_main:
        lax.fori_loop(0, n_main // ATTN_TRIP, trip, 0)
    for c in range(n_main, n_chunks):
        pipeline_step(n_main, c - n_main)

    def normalized(mp):
        return acc_ref[mp, :DA_V_DIM, :] / acc_ref[mp, DA_V_DIM:DA_V_DIM + 1, :]

    if n_maps == 2:
        o = normalized(0) - lam_ref[0] * normalized(1)
        ms = jnp.mean(o * o, axis=0, keepdims=True)
        o = o * lax.rsqrt(ms + SUBLN_EPS) * g_ref[...] * one_minus_lambda_init
    else:
        o = normalized(0)
    o_ref[...] = o.T.astype(o_ref.dtype)


def _attention(q, k, vt, batch, seq, n_maps, lam=None, subln=None, one_minus_lambda_init=1.0):
    t = q.shape[0]
    tq = min(TQ, seq)
    nq = seq // tq
    heads = q.shape[1] // LANES
    q_spec = pl.BlockSpec((tq, LANES), lambda b, h, i: (b * nq + i, h))
    k_spec = pl.BlockSpec((seq, LANES), lambda b, h, i: (b, h))
    vt_spec = pl.BlockSpec((None, vt.shape[1], LANES, vt.shape[3]), lambda b, h, i: (b, 0, h, 0))
    in_specs = [q_spec, k_spec, vt_spec]
    args = [q, k, vt]
    if n_maps == 2:
        in_specs = [pl.BlockSpec(memory_space=pltpu.SMEM)] + in_specs + [
            pl.BlockSpec((DA_V_DIM, 1), lambda b, h, i: (0, 0))]
        args = [lam] + args + [subln]
    tkc = vt.shape[3] // 2
    stat = pltpu.VMEM((n_maps, 1, tq), F32)
    ns = ATTN_SLOTS
    scratch = ([pltpu.VMEM((n_maps, tkc, tq), F32)] * ns + [stat] * ns
               + [pltpu.VMEM((n_maps, tkc, tq), BF16)] * ns + [stat] * ns
               + [pltpu.VMEM((n_maps, DA_V_DIM + ONES_ROWS, tq), F32), stat,
                  pltpu.VMEM((n_maps, tq, LANES), BF16)])
    return pl.pallas_call(
        functools.partial(_attn_kernel, n_maps=n_maps, one_minus_lambda_init=one_minus_lambda_init),
        grid=(batch, heads, nq),
        in_specs=in_specs,
        out_specs=q_spec,
        scratch_shapes=scratch,
        out_shape=jax.ShapeDtypeStruct((t, q.shape[1]), BF16),
        compiler_params=_cparams(("parallel", "parallel", "parallel")),
        name="diff_attn" if n_maps == 2 else "mla_attn",
    )(*args)


def _outproj_kernel(x_ref, oda_ref, oml_ref, wa_ref, wb_ref, g_ref, xo_ref, h_ref):
    y = x_ref[...] + jnp.dot(oda_ref[...], wa_ref[...], preferred_element_type=F32)
    y = y + jnp.dot(oml_ref[...], wb_ref[...], preferred_element_type=F32)
    xo_ref[...] = y
    h_ref[...] = _rms(y, g_ref[...], NORM_EPS).astype(h_ref.dtype)


def _outproj(x2, o_da, o_ml, w_a, w_b, g_ffn, h_dtype):
    t = x2.shape[0]
    tm = min(TM_PREP, t)
    row = lambda width: pl.BlockSpec((tm, width), lambda i: (i, 0))
    full = lambda a: pl.BlockSpec(a.shape, lambda i: (0,) * a.ndim)
    return pl.pallas_call(
        _outproj_kernel,
        grid=(t // tm,),
        in_specs=[row(D_MODEL), row(DA_WIDTH), row(MLA_WIDTH), full(w_a), full(w_b), full(g_ffn)],
        out_specs=[row(D_MODEL), row(D_MODEL)],
        out_shape=[jax.ShapeDtypeStruct((t, D_MODEL), F32), jax.ShapeDtypeStruct((t, D_MODEL), h_dtype)],
        compiler_params=_cparams(("parallel",)),
        name="outproj",
    )(x2, o_da, o_ml, w_a, w_b, g_ffn)


def _ffn_kernel(x_ref, h_ref, wg_ref, wu_ref, wd_ref, o_ref):
    h = h_ref[...]
    g = jnp.dot(h, wg_ref[...], preferred_element_type=F32)
    u = jnp.dot(h, wu_ref[...], preferred_element_type=F32)
    a = (g * jax.nn.sigmoid(g) * u).astype(BF16)
    o_ref[...] = x_ref[...] + jnp.dot(a, wd_ref[...], preferred_element_type=F32)


def _dense_ffn(x2, h, wg, wu, wd):
    t = x2.shape[0]
    tm = min(TM_FFN, t)
    row = pl.BlockSpec((tm, D_MODEL), lambda i: (i, 0))
    res = lambda a: pl.BlockSpec(a.shape, lambda i: (0, 0), pipeline_mode=pl.Buffered(1))
    return pl.pallas_call(
        _ffn_kernel,
        grid=(t // tm,),
        in_specs=[row, row, res(wg), res(wu), res(wd)],
        out_specs=row,
        out_shape=jax.ShapeDtypeStruct((t, D_MODEL), F32),
        compiler_params=_cparams(("parallel",)),
        name="dense_ffn",
    )(x2, h, wg, wu, wd)


def _router_kernel(h_ref, wr_ref, gates_ref, rank_ref, cnt_ref):
    tm = h_ref.shape[0]

    @pl.when(pl.program_id(0) == 0)
    def _():
        cnt_ref[...] = jnp.zeros_like(cnt_ref)

    lane = lax.broadcasted_iota(jnp.int32, (tm, LANES), 1)
    logits = jnp.dot(h_ref[...], wr_ref[...], preferred_element_type=F32, precision=lax.Precision.HIGHEST)
    neg = jnp.float32(-jnp.inf)
    logits = jnp.where(lane < N_EXPERTS, logits, neg)
    v1 = jnp.max(logits, axis=1, keepdims=True)
    i1 = jnp.min(jnp.where(logits == v1, lane, LANES), axis=1, keepdims=True)
    sel1 = lane == i1
    rest = jnp.where(sel1, neg, logits)
    v2 = jnp.max(rest, axis=1, keepdims=True)
    i2 = jnp.min(jnp.where(rest == v2, lane, LANES), axis=1, keepdims=True)
    sel2 = lane == i2
    e = jnp.exp(v2 - v1)
    g1 = 1.0 / (1.0 + e)
    g2 = e / (1.0 + e)
    gates_ref[...] = jnp.where(sel1, g1, jnp.where(sel2, g2, 0.0))
    sel = jnp.where(sel1 | sel2, 1.0, 0.0)
    r_i = lax.broadcasted_iota(jnp.int32, (tm, tm), 0)
    c_i = lax.broadcasted_iota(jnp.int32, (tm, tm), 1)
    tri = jnp.where(c_i < r_i, 1.0, 0.0).astype(BF16)
    before = jnp.dot(tri, sel.astype(BF16), preferred_element_type=F32) + cnt_ref[0:1, :]
    rank_ref[...] = jnp.where(sel > 0, before, -1.0).astype(jnp.int32)
    cnt_ref[...] = cnt_ref[...] + jnp.sum(sel, axis=0, keepdims=True)


def _router(h, w_router_pad):
    t = h.shape[0]
    tm = min(TM_ROUTE, t)
    row = lambda width: pl.BlockSpec((tm, width), lambda i: (i, 0))
    return pl.pallas_call(
        _router_kernel,
        grid=(t // tm,),
        in_specs=[row(D_MODEL), pl.BlockSpec(w_router_pad.shape, lambda i: (0, 0))],
        out_specs=[row(LANES), row(LANES), pl.BlockSpec((8, LANES), lambda i: (0, 0))],
        out_shape=[jax.ShapeDtypeStruct((t, LANES), F32), jax.ShapeDtypeStruct((t, LANES), jnp.int32),
                   jax.ShapeDtypeStruct((8, LANES), F32)],
        compiler_params=_cparams(("arbitrary",)),
        name="router",
    )(h, w_router_pad)


def _scatter_kernel(ends_ref, dest_ref, h_ref, xs_ref, zero_ref, sem):
    tm = h_ref.shape[0]

    def row_copy(src, dst_row):
        return pltpu.make_async_copy(src, xs_ref.at[pl.ds(dst_row, 1), :], sem)

    @pl.when(pl.program_id(0) == 0)
    def _():
        zero_ref[...] = jnp.zeros_like(zero_ref)
        copies = []
        for e in range(N_EXPERTS):
            start = pl.multiple_of(jnp.maximum(ends_ref[e] - TM_EXP, 0), TM_EXP)
            cp = pltpu.make_async_copy(zero_ref, xs_ref.at[pl.ds(start, TM_EXP), :], sem)
            cp.start()
            copies.append(cp)
        for cp in copies:
            cp.wait()

        def zero_tail(i, _):
            cp = pltpu.make_async_copy(zero_ref, xs_ref.at[pl.ds(pl.multiple_of(i * TM_EXP, TM_EXP), TM_EXP), :], sem)
            cp.start()
            cp.wait()
            return 0

        lax.fori_loop(ends_ref[N_EXPERTS - 1] // TM_EXP, xs_ref.shape[0] // TM_EXP, zero_tail, 0)

    def issue(r, _):
        src = h_ref.at[pl.ds(r, 1), :]
        row_copy(src, dest_ref[0, 2 * r]).start()
        row_copy(src, dest_ref[0, 2 * r + 1]).start()
        return 0

    lax.fori_loop(0, tm, issue, 0)

    def drain(r, _):
        src = h_ref.at[pl.ds(r, 1), :]
        row_copy(src, dest_ref[0, 2 * r]).wait()
        row_copy(src, dest_ref[0, 2 * r + 1]).wait()
        return 0

    lax.fori_loop(0, tm, drain, 0)


def _scatter_rows(h, dest2, ends, n_rows):
    t = h.shape[0]
    tm = min(TM_ROW, t)
    nt = t // tm
    dest3 = dest2.reshape(nt, 1, 2 * tm)
    grid_spec = pltpu.PrefetchScalarGridSpec(
        num_scalar_prefetch=1,
        grid=(nt,),
        in_specs=[pl.BlockSpec((None, 1, 2 * tm), lambda i, ends: (i, 0, 0), memory_space=pltpu.SMEM),
                  pl.BlockSpec((tm, D_MODEL), lambda i, ends: (i, 0))],
        out_specs=pl.BlockSpec(memory_space=pl.ANY),
        scratch_shapes=[pltpu.VMEM((TM_EXP, D_MODEL), F32), pltpu.SemaphoreType.DMA(())],
    )
    return pl.pallas_call(
        _scatter_kernel,
        grid_spec=grid_spec,
        out_shape=jax.ShapeDtypeStruct((n_rows, D_MODEL), F32),
        compiler_params=_cparams(("arbitrary",)),
        name="scatter_rows",
    )(ends, dest3, h)


def _expert_kernel(te_ref, na_ref, xs_ref, wg_ref, wu_ref, wd_ref, ys_ref, acc_ref):
    i = pl.program_id(0)
    j = pl.program_id(1)

    @pl.when(i < na_ref[0])
    def _():
        xb = xs_ref[...].astype(BF16)
        g = jnp.dot(xb, wg_ref[...], preferred_element_type=F32)
        u = jnp.dot(xb, wu_ref[...], preferred_element_type=F32)
        a = (g * jax.nn.sigmoid(g) * u).astype(BF16)
        part = jnp.dot(a, wd_ref[...], preferred_element_type=F32)

        @pl.when(j == 0)
        def _():
            acc_ref[...] = part

        @pl.when(j > 0)
        def _():
            acc_ref[...] = acc_ref[...] + part

        @pl.when(j == pl.num_programs(1) - 1)
        def _():
            ys_ref[...] = acc_ref[...]

    @pl.when(i >= na_ref[0])
    def _():
        ys_ref[...] = jnp.zeros_like(ys_ref)


def _expert_ffn(xs, tile_expert, n_active, wg, wu, wd):
    n_rows = xs.shape[0]
    ff = wg.shape[2]
    tf = ff // 2
    nj = ff // tf
    nt = n_rows // TM_EXP

    def wj(i, j, na):
        return jnp.where(i < na[0], j, nj - 1)

    row = pl.BlockSpec((TM_EXP, D_MODEL), lambda i, j, te, na: (i, 0))
    grid_spec = pltpu.PrefetchScalarGridSpec(
        num_scalar_prefetch=2,
        grid=(nt, nj),
        in_specs=[row,
                  pl.BlockSpec((None, D_MODEL, tf), lambda i, j, te, na: (te[i], 0, wj(i, j, na))),
                  pl.BlockSpec((None, D_MODEL, tf), lambda i, j, te, na: (te[i], 0, wj(i, j, na))),
                  pl.BlockSpec((None, tf, D_MODEL), lambda i, j, te, na: (te[i], wj(i, j, na), 0))],
        out_specs=row,
        scratch_shapes=[pltpu.VMEM((TM_EXP, D_MODEL), F32)],
    )
    return pl.pallas_call(
        _expert_kernel,
        grid_spec=grid_spec,
        out_shape=jax.ShapeDtypeStruct((n_rows, D_MODEL), F32),
        compiler_params=_cparams(("arbitrary", "arbitrary")),
        name="expert_ffn",
    )(tile_expert, n_active, xs, wg, wu, wd)


def _combine_kernel(dest_ref, x_ref, g12_ref, gfin_ref, ys_ref, o_ref, buf_ref, sem, *, final):
    tm = x_ref.shape[0]

    def row_copy(r, k):
        return pltpu.make_async_copy(ys_ref.at[pl.ds(dest_ref[0, 2 * r + k], 1), :],
                                     buf_ref.at[k, pl.ds(r, 1), :], sem)

    def issue(r, _):
        row_copy(r, 0).start()
        row_copy(r, 1).start()
        return 0

    lax.fori_loop(0, tm, issue, 0)

    def drain(r, _):
        row_copy(r, 0).wait()
        row_copy(r, 1).wait()
        return 0

    lax.fori_loop(0, tm, drain, 0)
    g12 = g12_ref[...]
    y = x_ref[...] + g12[:, 0:1] * buf_ref[0] + g12[:, 1:2] * buf_ref[1]
    o_ref[...] = _rms(y, gfin_ref[...], NORM_EPS) if final else y


def _combine(x2, ys, dest2, g12, g_final, final):
    t = x2.shape[0]
    tm = min(TM_ROW, t)
    nt = t // tm
    dest3 = dest2.reshape(nt, 1, 2 * tm)
    row = pl.BlockSpec((tm, D_MODEL), lambda i: (i, 0))
    return pl.pallas_call(
        functools.partial(_combine_kernel, final=final),
        grid=(nt,),
        in_specs=[pl.BlockSpec((None, 1, 2 * tm), lambda i: (i, 0, 0), memory_space=pltpu.SMEM),
                  row, pl.BlockSpec((tm, 2), lambda i: (i, 0)),
                  pl.BlockSpec((1, D_MODEL), lambda i: (0, 0)),
                  pl.BlockSpec(memory_space=pl.ANY)],
        out_specs=row,
        out_shape=jax.ShapeDtypeStruct((t, D_MODEL), F32),
        scratch_shapes=[pltpu.VMEM((2, tm, D_MODEL), F32), pltpu.SemaphoreType.DMA(())],
        compiler_params=_cparams(("arbitrary",)),
        name="combine",
    )(dest3, x2, g12, g_final, ys)


def _final_norm_kernel(x_ref, g_ref, o_ref):
    o_ref[...] = _rms(x_ref[...], g_ref[...], NORM_EPS)


def _final_norm(x2, g):
    t = x2.shape[0]
    tm = min(TM_PREP, t)
    row = pl.BlockSpec((tm, D_MODEL), lambda i: (i, 0))
    return pl.pallas_call(
        _final_norm_kernel, grid=(t // tm,),
        in_specs=[row, pl.BlockSpec((1, D_MODEL), lambda i: (0, 0))], out_specs=row,
        out_shape=jax.ShapeDtypeStruct((t, D_MODEL), F32),
        compiler_params=_cparams(("parallel",)), name="final_norm",
    )(x2, g)


def _moe_layer(x2, h, router, wg, wu, wd, g_final, final):
    t = x2.shape[0]
    w_router_pad = jnp.zeros((D_MODEL, LANES), F32).at[:, :N_EXPERTS].set(router.astype(F32))
    gates, rank, cnt = _router(h, w_router_pad)
    counts = cnt[0, :N_EXPERTS].astype(jnp.int32)
    padded = ((counts + TM_EXP - 1) // TM_EXP) * TM_EXP
    ends = jnp.cumsum(padded)
    starts = ends - padded
    n_rows = TOP_K * t + N_EXPERTS * TM_EXP
    nt = n_rows // TM_EXP
    n_active = (ends[-1] // TM_EXP).astype(jnp.int32)
    tile_start = jnp.minimum(jnp.arange(nt, dtype=jnp.int32), n_active - 1) * TM_EXP
    tile_expert = jnp.minimum(jnp.sum(ends[None, :] <= tile_start[:, None], axis=1), N_EXPERTS - 1).astype(jnp.int32)
    rank8 = rank[:, :N_EXPERTS]
    sel8 = rank8 >= 0
    dest8 = starts[None, :] + rank8
    e_idx = jnp.arange(N_EXPERTS, dtype=jnp.int32)[None, :]
    first = jnp.min(jnp.where(sel8, e_idx, N_EXPERTS), axis=1, keepdims=True)
    second = jnp.max(jnp.where(sel8, e_idx, -1), axis=1, keepdims=True)
    pick = lambda a, e: jnp.sum(jnp.where(e_idx == e, a, 0), axis=1, keepdims=True)
    dest2 = jnp.concatenate([pick(dest8, first), pick(dest8, second)], axis=1).astype(jnp.int32)
    gates8 = gates[:, :N_EXPERTS]
    g12 = jnp.concatenate([pick(gates8, first), pick(gates8, second)], axis=1)
    xs = _scatter_rows(h, dest2, ends.astype(jnp.int32), n_rows)
    ys = _expert_ffn(xs, tile_expert, n_active.reshape(1), wg, wu, wd)
    return _combine(x2, ys, dest2, g12, g_final, final)


def kernel(x, positions, w_in, w_out, norm_mix, norm_ffn, da_lambda_q1, da_lambda_k1, da_lambda_q2,
           da_lambda_k2, da_subln, mla_q_norm, mla_w_uq, mla_kv_norm, mla_w_ukv, ffn_w_gate, ffn_w_up,
           ffn_w_down, moe_router, moe_w_gate, moe_w_up, moe_w_down, norm_final):
    batch, seq, d = x.shape
    depth = w_in.shape[0]
    t = batch * seq
    x2 = x.reshape(t, d).astype(F32)
    tabs = _rope_tables(positions)

    n_qk = 2 * DA_HEADS * 2 * DA_HEAD_DIM
    n_v = DA_HEADS * DA_V_DIM
    c0 = n_qk + n_v
    kv_per_head = MLA_NOPE_DIM + MLA_V_DIM

    for l in range(depth):
        wl = w_in[l]
        w_kr = jnp.zeros((d, LANES), F32).at[:, MLA_NOPE_DIM:MLA_QK_DIM].set(
            wl[:, c0 + MLA_Q_RANK + MLA_KV_RANK:])
        wc = jnp.concatenate([wl[:, c0:c0 + MLA_Q_RANK + MLA_KV_RANK], w_kr], axis=1)
        wuq = jnp.pad(mla_w_uq[l].reshape(MLA_Q_RANK, MLA_HEADS, MLA_QK_DIM),
                      ((0, 0), (0, 0), (0, LANES - MLA_QK_DIM))).reshape(MLA_Q_RANK, MLA_HEADS * LANES)
        wukv = mla_w_ukv[l].reshape(MLA_KV_RANK, MLA_HEADS, kv_per_head)
        wuk = jnp.pad(wukv[:, :, :MLA_NOPE_DIM],
                      ((0, 0), (0, 0), (0, LANES - MLA_NOPE_DIM))).reshape(MLA_KV_RANK, MLA_HEADS * LANES)
        wuv = wukv[:, :, MLA_NOPE_DIM:].reshape(MLA_KV_RANK, MLA_HEADS * MLA_V_DIM)
        w = dict(wqk=wl[:, :n_qk].astype(BF16), wv=wl[:, n_qk:c0].astype(BF16), wc=wc.astype(BF16),
                 gq=mla_q_norm[l].reshape(1, -1).astype(F32), wuq=wuq.astype(BF16),
                 gkv=mla_kv_norm[l].reshape(1, -1).astype(F32), wuk=wuk.astype(BF16), wuv=wuv.astype(BF16))

        da_q, da_k, da_vt, ml_q, ml_k, ml_vt = _prep(
            x2, norm_mix[l].reshape(1, d).astype(F32), w, tabs, batch, seq)

        lambda_init = 0.8 - 0.6 * float(np.exp(-0.3 * l))
        lam = (jnp.exp(jnp.sum(da_lambda_q1[l].astype(F32) * da_lambda_k1[l].astype(F32)))
               - jnp.exp(jnp.sum(da_lambda_q2[l].astype(F32) * da_lambda_k2[l].astype(F32)))
               + lambda_init).reshape(1).astype(F32)
        o_da = _attention(da_q, da_k, da_vt, batch, seq, 2, lam=lam,
                          subln=da_subln[l].reshape(DA_V_DIM, 1).astype(F32),
                          one_minus_lambda_init=1.0 - lambda_init)
        o_ml = _attention(ml_q, ml_k, ml_vt, batch, seq, 1)

        is_moe = l % 2 == 1
        wo = w_out[l].astype(BF16)
        x2, h = _outproj(x2, o_da, o_ml, wo[:DA_WIDTH], wo[DA_WIDTH:],
                         norm_ffn[l].reshape(1, d).astype(F32), F32 if is_moe else BF16)
        i = l // 2
        last = l == depth - 1
        if not is_moe:
            x2 = _dense_ffn(x2, h, ffn_w_gate[i].astype(BF16), ffn_w_up[i].astype(BF16),
                            ffn_w_down[i].astype(BF16))
            if last:
                x2 = _final_norm(x2, norm_final.reshape(1, d).astype(F32))
        else:
            x2 = _moe_layer(x2, h, moe_router[i], moe_w_gate[i].astype(BF16), moe_w_up[i].astype(BF16),
                            moe_w_down[i].astype(BF16), norm_final.reshape(1, d).astype(F32), last)
    return x2.reshape(batch, seq, d).astype(x.dtype)
```

```python
import functools
import math

import numpy as np
import jax
import jax.numpy as jnp
from jax import lax
from jax.experimental import pallas as pl
from jax.experimental.pallas import tpu as pltpu

D_MODEL = 1024
ROPE_THETA = 10000.0
NORM_EPS = 1e-6
SUBLN_EPS = 1e-5
DA_HEAD_DIM = 64
DA_V_DIM = 128
DA_HEADS = 4
DA_WIDTH = 512
MLA_HEADS = 4
MLA_V_DIM = 128
MLA_WIDTH = 512
MLA_Q_RANK = 256
MLA_KV_RANK = 128
MLA_NOPE_DIM = 64
MLA_ROPE_DIM = 32
MLA_QK_DIM = 96
N_EXPERTS = 8
TOP_K = 2
LOG2E = 1.4426950408889634

LANES = 128
ONES_ROWS = 16
VT_ROWS = DA_V_DIM + ONES_ROWS
VMEM_LIMIT = 56 * 1024 * 1024

TM_PREP = 512
TQ = 512
ATTN_CHUNK = 256
ATTN_TRIP = 16
TM_FFN = 512
TM_EXP = 512
TM_ROW = 256
TM_ROUTE = 512
ROW_DMA_UNROLL = 8

BF16 = jnp.bfloat16
F32 = jnp.float32


def _cparams(sem):
    return pltpu.CompilerParams(dimension_semantics=sem, vmem_limit_bytes=VMEM_LIMIT)


def _rms(x, g, eps):
    return x * lax.rsqrt(jnp.mean(x * x, axis=-1, keepdims=True) + eps) * g


def _rope_table_kernel(pos_ref, inv_da_ref, sgn_da_ref, inv_ml_ref, sgn_ml_ref,
                       cos_da_ref, sin_da_ref, cos_ml_ref, sin_ml_ref):
    pos = pos_ref[...]
    ang_da = pos * inv_da_ref[...]
    cos_da_ref[...] = jnp.cos(ang_da)
    sin_da_ref[...] = jnp.sin(ang_da) * sgn_da_ref[...]
    ang_ml = pos * inv_ml_ref[...]
    cos_ml_ref[...] = jnp.cos(ang_ml)
    sin_ml_ref[...] = jnp.sin(ang_ml) * sgn_ml_ref[...]


def _rope_tables(positions):
    t = positions.size
    tm = min(TM_PREP, t)
    pos = positions.reshape(t, 1).astype(F32)
    lane = np.arange(LANES)
    inv_da = 1.0 / (ROPE_THETA ** (jnp.arange(0, DA_HEAD_DIM, 2, dtype=F32) / DA_HEAD_DIM))
    inv_ml = 1.0 / (ROPE_THETA ** (jnp.arange(0, MLA_ROPE_DIM, 2, dtype=F32) / MLA_ROPE_DIM))
    half_da = DA_HEAD_DIM // 2
    half_ml = MLA_ROPE_DIM // 2
    inv_da_l = inv_da[lane % half_da][None, :]
    sgn_da = jnp.asarray(np.where(lane % DA_HEAD_DIM < half_da, -1.0, 1.0), F32)[None, :]
    in_rope = (lane >= MLA_NOPE_DIM) & (lane < MLA_QK_DIM)
    inv_ml_l = jnp.where(in_rope, inv_ml[(lane - MLA_NOPE_DIM) % half_ml], 0.0)[None, :]
    sgn_ml = jnp.asarray(
        np.where(in_rope, np.where(lane < MLA_NOPE_DIM + half_ml, -1.0, 1.0), 0.0), F32)[None, :]
    row = pl.BlockSpec((tm, LANES), lambda i: (i, 0))
    const = pl.BlockSpec((1, LANES), lambda i: (0, 0))
    out = jax.ShapeDtypeStruct((t, LANES), F32)
    return pl.pallas_call(
        _rope_table_kernel,
        grid=(t // tm,),
        in_specs=[pl.BlockSpec((tm, 1), lambda i: (i, 0)), const, const, const, const],
        out_specs=[row, row, row, row],
        out_shape=[out, out, out, out],
        compiler_params=_cparams(("parallel",)),
        name="rope_tables",
    )(pos, inv_da_l, sgn_da, inv_ml_l, sgn_ml)


def _rot_da(blk, lane):
    fwd = pltpu.roll(blk, LANES - DA_HEAD_DIM // 2, 1)
    bwd = pltpu.roll(blk, DA_HEAD_DIM // 2, 1)
    return jnp.where(lane % DA_HEAD_DIM < DA_HEAD_DIM // 2, fwd, bwd)


def _rot_ml(blk, lane):
    fwd = pltpu.roll(blk, LANES - MLA_ROPE_DIM // 2, 1)
    bwd = pltpu.roll(blk, MLA_ROPE_DIM // 2, 1)
    return jnp.where(lane < MLA_NOPE_DIM + MLA_ROPE_DIM // 2, fwd, bwd)


def _store_vt(vt_ref, v):
    vt = v.T.astype(BF16)
    ones = jnp.ones((ONES_ROWS, vt.shape[1]), BF16)
    for h in range(vt.shape[0] // DA_V_DIM):
        vt_ref[h * VT_ROWS:h * VT_ROWS + DA_V_DIM, :] = vt[h * DA_V_DIM:(h + 1) * DA_V_DIM, :]
        vt_ref[h * VT_ROWS + DA_V_DIM:(h + 1) * VT_ROWS, :] = ones


def _prep_kernel(x_ref, g_ref, wqk_ref, wv_ref, wc_ref, gq_ref, wuq_ref, gkv_ref, wuk_ref, wuv_ref,
                 cda_ref, sda_ref, cml_ref, sml_ref,
                 daq_ref, dak_ref, davt_ref, mlq_ref, mlk_ref, mlvt_ref):
    tm = x_ref.shape[0]
    lane = lax.broadcasted_iota(jnp.int32, (tm, LANES), 1)
    hb = _rms(x_ref[...], g_ref[...], NORM_EPS).astype(BF16)

    qk = jnp.dot(hb, wqk_ref[...], preferred_element_type=F32)
    cda, sda = cda_ref[...], sda_ref[...]
    q_scale = DA_HEAD_DIM ** -0.5 * LOG2E
    for j in range(2 * DA_HEADS):
        blk = qk[:, j * LANES:(j + 1) * LANES]
        r = blk * cda + _rot_da(blk, lane) * sda
        if j < DA_HEADS:
            daq_ref[:, j * LANES:(j + 1) * LANES] = (r * q_scale).astype(BF16)
        else:
            jj = j - DA_HEADS
            dak_ref[:, jj * LANES:(jj + 1) * LANES] = r.astype(BF16)
    v = jnp.dot(hb, wv_ref[...], preferred_element_type=F32)
    _store_vt(davt_ref, v)

    c = jnp.dot(hb, wc_ref[...], preferred_element_type=F32)
    cml, sml = cml_ref[...], sml_ref[...]
    cq = _rms(c[:, :MLA_Q_RANK], gq_ref[...], NORM_EPS).astype(BF16)
    qm = jnp.dot(cq, wuq_ref[...], preferred_element_type=F32)
    ml_scale = MLA_QK_DIM ** -0.5 * LOG2E
    for j in range(MLA_HEADS):
        blk = qm[:, j * LANES:(j + 1) * LANES]
        r = blk * cml + _rot_ml(blk, lane) * sml
        mlq_ref[:, j * LANES:(j + 1) * LANES] = (r * ml_scale).astype(BF16)
    ckv = _rms(c[:, MLA_Q_RANK:MLA_Q_RANK + MLA_KV_RANK], gkv_ref[...], NORM_EPS).astype(BF16)
    kr = c[:, MLA_Q_RANK + MLA_KV_RANK:]
    kr = kr * cml + _rot_ml(kr, lane) * sml
    kn = jnp.dot(ckv, wuk_ref[...], preferred_element_type=F32)
    for j in range(MLA_HEADS):
        mlk_ref[:, j * LANES:(j + 1) * LANES] = (kn[:, j * LANES:(j + 1) * LANES] + kr).astype(BF16)
    vm = jnp.dot(ckv, wuv_ref[...], preferred_element_type=F32)
    _store_vt(mlvt_ref, vm)


def _prep(x2, g_mix, w, tabs, batch, seq):
    t = x2.shape[0]
    tm = min(TM_PREP, seq)
    nps = seq // tm
    row = lambda width: pl.BlockSpec((tm, width), lambda i: (i, 0))
    full = lambda a: pl.BlockSpec(a.shape, lambda i: (0,) * a.ndim)
    vt_spec = pl.BlockSpec((None, DA_HEADS * VT_ROWS, tm), lambda i: (i // nps, 0, i % nps))
    tok = jax.ShapeDtypeStruct((t, DA_WIDTH), BF16)
    vts = jax.ShapeDtypeStruct((batch, DA_HEADS * VT_ROWS, seq), BF16)
    weights = [w["wqk"], w["wv"], w["wc"], w["gq"], w["wuq"], w["gkv"], w["wuk"], w["wuv"]]
    return pl.pallas_call(
        _prep_kernel,
        grid=(t // tm,),
        in_specs=[row(D_MODEL), full(g_mix)] + [full(a) for a in weights] + [row(LANES)] * 4,
        out_specs=[row(DA_WIDTH), row(DA_WIDTH), vt_spec, row(DA_WIDTH), row(DA_WIDTH), vt_spec],
        out_shape=[tok, tok, vts, tok, tok, vts],
        compiler_params=_cparams(("parallel",)),
        name="prep",
    )(x2, g_mix, *weights, *tabs)


def _attn_kernel(*refs, n_maps, one_minus_lambda_init):
    if n_maps == 2:
        lam_ref, q_ref, k_ref, vt_ref, g_ref, o_ref = refs[:6]
    else:
        q_ref, k_ref, vt_ref, o_ref = refs[:4]
    p_ref, acc_ref, m_ref, qm_ref = refs[-4:]
    seq = k_ref.shape[0]
    tkc = ATTN_CHUNK
    n_chunks = seq // tkc
    q = q_ref[...]
    if n_maps == 2:
        lane = lax.broadcasted_iota(jnp.int32, q.shape, 1)
        zero = jnp.zeros_like(q)
        qm_ref[0] = jnp.where(lane < DA_HEAD_DIM, q, zero)
        qm_ref[1] = jnp.where(lane >= DA_HEAD_DIM, q, zero)
    else:
        qm_ref[0] = q

    def scores(c, mp):
        k = k_ref[pl.ds(pl.multiple_of(c * tkc, tkc), tkc), :]
        return lax.dot_general(k, qm_ref[mp], (((1,), (1,)), ((), ())), preferred_element_type=F32)

    for mp in range(n_maps):
        m_ref[mp] = jnp.max(scores(0, mp), axis=0, keepdims=True)

    per_trip = math.gcd(ATTN_TRIP, n_chunks)

    def trip(t, _):
        for off in range(per_trip):
            c = t * per_trip + off
            for mp in range(n_maps):
                p_ref[mp, pl.ds(pl.multiple_of(c * tkc, tkc), tkc), :] = jnp.exp2(
                    scores(c, mp) - m_ref[mp]).astype(BF16)
        return 0

    lax.fori_loop(0, n_chunks // per_trip, trip, 0)
    for mp in range(n_maps):
        acc_ref[mp] = jnp.dot(vt_ref[...], p_ref[mp], preferred_element_type=F32)

    overflowed = jnp.max(jnp.where(jnp.isfinite(acc_ref[...]), 0.0, 1.0)) > 0.0

    @pl.when(overflowed)
    def _():
        m_ref[...] = jnp.full(m_ref.shape, -jnp.inf, F32)
        acc_ref[...] = jnp.zeros(acc_ref.shape, F32)

        def chunk(c, _):
            vt = vt_ref[:, pl.ds(pl.multiple_of(c * tkc, tkc), tkc)]
            for mp in range(n_maps):
                s = scores(c, mp)
                m_old = m_ref[mp]
                m_new = jnp.maximum(m_old, jnp.max(s, axis=0, keepdims=True))
                p = jnp.exp2(s - m_new).astype(BF16)
                acc_ref[mp] = jnp.exp2(m_old - m_new) * acc_ref[mp] + jnp.dot(vt, p, preferred_element_type=F32)
                m_ref[mp] = m_new
            return 0

        lax.fori_loop(0, n_chunks, chunk, 0)

    def normalized(mp):
        return acc_ref[mp, :DA_V_DIM, :] / acc_ref[mp, DA_V_DIM:DA_V_DIM + 1, :]

    if n_maps == 2:
        o = normalized(0) - lam_ref[0] * normalized(1)
        ms = jnp.mean(o * o, axis=0, keepdims=True)
        o = o * lax.rsqrt(ms + SUBLN_EPS) * g_ref[...] * one_minus_lambda_init
    else:
        o = normalized(0)
    o_ref[...] = o.T.astype(o_ref.dtype)


def _attention(q, k, vt, batch, seq, n_maps, lam=None, subln=None, one_minus_lambda_init=1.0):
    t = q.shape[0]
    tq = min(TQ, seq)
    nq = seq // tq
    heads = q.shape[1] // LANES
    q_spec = pl.BlockSpec((tq, LANES), lambda b, h, i: (b * nq + i, h))
    k_spec = pl.BlockSpec((seq, LANES), lambda b, h, i: (b, h))
    vt_spec = pl.BlockSpec((None, VT_ROWS, seq), lambda b, h, i: (b, h, 0))
    in_specs = [q_spec, k_spec, vt_spec]
    args = [q, k, vt]
    if n_maps == 2:
        in_specs = [pl.BlockSpec(memory_space=pltpu.SMEM)] + in_specs + [
            pl.BlockSpec((DA_V_DIM, 1), lambda b, h, i: (0, 0))]
        args = [lam] + args + [subln]
    scratch = [pltpu.VMEM((n_maps, seq, tq), BF16),
               pltpu.VMEM((n_maps, VT_ROWS, tq), F32),
               pltpu.VMEM((n_maps, 1, tq), F32),
               pltpu.VMEM((n_maps, tq, LANES), BF16)]
    return pl.pallas_call(
        functools.partial(_attn_kernel, n_maps=n_maps, one_minus_lambda_init=one_minus_lambda_init),
        grid=(batch, heads, nq),
        in_specs=in_specs,
        out_specs=q_spec,
        scratch_shapes=scratch,
        out_shape=jax.ShapeDtypeStruct((t, q.shape[1]), BF16),
        compiler_params=_cparams(("parallel", "parallel", "parallel")),
        name="diff_attn" if n_maps == 2 else "mla_attn",
    )(*args)


def _outproj_kernel(x_ref, oda_ref, oml_ref, wa_ref, wb_ref, g_ref, xo_ref, h_ref):
    y = x_ref[...] + jnp.dot(oda_ref[...], wa_ref[...], preferred_element_type=F32)
    y = y + jnp.dot(oml_ref[...], wb_ref[...], preferred_element_type=F32)
    xo_ref[...] = y
    h_ref[...] = _rms(y, g_ref[...], NORM_EPS).astype(h_ref.dtype)


def _outproj(x2, o_da, o_ml, w_a, w_b, g_ffn, h_dtype):
    t = x2.shape[0]
    tm = min(TM_PREP, t)
    row = lambda width: pl.BlockSpec((tm, width), lambda i: (i, 0))
    full = lambda a: pl.BlockSpec(a.shape, lambda i: (0,) * a.ndim)
    return pl.pallas_call(
        _outproj_kernel,
        grid=(t // tm,),
        in_specs=[row(D_MODEL), row(DA_WIDTH), row(MLA_WIDTH), full(w_a), full(w_b), full(g_ffn)],
        out_specs=[row(D_MODEL), row(D_MODEL)],
        out_shape=[jax.ShapeDtypeStruct((t, D_MODEL), F32), jax.ShapeDtypeStruct((t, D_MODEL), h_dtype)],
        compiler_params=_cparams(("parallel",)),
        name="outproj",
    )(x2, o_da, o_ml, w_a, w_b, g_ffn)


def _ffn_kernel(x_ref, h_ref, wg_ref, wu_ref, wd_ref, o_ref):
    h = h_ref[...]
    g = jnp.dot(h, wg_ref[...], preferred_element_type=F32)
    u = jnp.dot(h, wu_ref[...], preferred_element_type=F32)
    a = (g * jax.nn.sigmoid(g) * u).astype(BF16)
    o_ref[...] = x_ref[...] + jnp.dot(a, wd_ref[...], preferred_element_type=F32)


def _dense_ffn(x2, h, wg, wu, wd):
    t = x2.shape[0]
    tm = min(TM_FFN, t)
    row = pl.BlockSpec((tm, D_MODEL), lambda i: (i, 0))
    res = lambda a: pl.BlockSpec(a.shape, lambda i: (0, 0), pipeline_mode=pl.Buffered(1))
    return pl.pallas_call(
        _ffn_kernel,
        grid=(t // tm,),
        in_specs=[row, row, res(wg), res(wu), res(wd)],
        out_specs=row,
        out_shape=jax.ShapeDtypeStruct((t, D_MODEL), F32),
        compiler_params=_cparams(("parallel",)),
        name="dense_ffn",
    )(x2, h, wg, wu, wd)


def _router_kernel(h_ref, wr_ref, gates_ref, rank_ref, cnt_ref):
    tm = h_ref.shape[0]

    @pl.when(pl.program_id(0) == 0)
    def _():
        cnt_ref[...] = jnp.zeros_like(cnt_ref)

    lane = lax.broadcasted_iota(jnp.int32, (tm, LANES), 1)
    logits = jnp.dot(h_ref[...], wr_ref[...], preferred_element_type=F32, precision=lax.Precision.HIGHEST)
    neg = jnp.float32(-jnp.inf)
    logits = jnp.where(lane < N_EXPERTS, logits, neg)
    v1 = jnp.max(logits, axis=1, keepdims=True)
    i1 = jnp.min(jnp.where(logits == v1, lane, LANES), axis=1, keepdims=True)
    sel1 = lane == i1
    rest = jnp.where(sel1, neg, logits)
    v2 = jnp.max(rest, axis=1, keepdims=True)
    i2 = jnp.min(jnp.where(rest == v2, lane, LANES), axis=1, keepdims=True)
    sel2 = lane == i2
    e = jnp.exp(v2 - v1)
    g1 = 1.0 / (1.0 + e)
    g2 = e / (1.0 + e)
    gates_ref[...] = jnp.where(sel1, g1, jnp.where(sel2, g2, 0.0))
    sel = jnp.where(sel1 | sel2, 1.0, 0.0)
    r_i = lax.broadcasted_iota(jnp.int32, (tm, tm), 0)
    c_i = lax.broadcasted_iota(jnp.int32, (tm, tm), 1)
    tri = jnp.where(c_i < r_i, 1.0, 0.0).astype(BF16)
    before = jnp.dot(tri, sel.astype(BF16), preferred_element_type=F32) + cnt_ref[0:1, :]
    rank_ref[...] = jnp.where(sel > 0, before, -1.0).astype(jnp.int32)
    cnt_ref[...] = cnt_ref[...] + jnp.sum(sel, axis=0, keepdims=True)


def _router(h, w_router_pad):
    t = h.shape[0]
    tm = min(TM_ROUTE, t)
    row = lambda width: pl.BlockSpec((tm, width), lambda i: (i, 0))
    return pl.pallas_call(
        _router_kernel,
        grid=(t // tm,),
        in_specs=[row(D_MODEL), pl.BlockSpec(w_router_pad.shape, lambda i: (0, 0))],
        out_specs=[row(LANES), row(LANES), pl.BlockSpec((8, LANES), lambda i: (0, 0))],
        out_shape=[jax.ShapeDtypeStruct((t, LANES), F32), jax.ShapeDtypeStruct((t, LANES), jnp.int32),
                   jax.ShapeDtypeStruct((8, LANES), F32)],
        compiler_params=_cparams(("arbitrary",)),
        name="router",
    )(h, w_router_pad)


def _scatter_kernel(ends_ref, dest_ref, h_ref, xs_ref, zero_ref, sem):
    tm = h_ref.shape[0]

    def row_copy(src, dst_row):
        return pltpu.make_async_copy(src, xs_ref.at[pl.ds(dst_row, 1), :], sem)

    @pl.when(pl.program_id(0) == 0)
    def _():
        zero_ref[...] = jnp.zeros_like(zero_ref)
        copies = []
        for e in range(N_EXPERTS):
            start = pl.multiple_of(jnp.maximum(ends_ref[e] - TM_EXP, 0), TM_EXP)
            cp = pltpu.make_async_copy(zero_ref, xs_ref.at[pl.ds(start, TM_EXP), :], sem)
            cp.start()
            copies.append(cp)
        for cp in copies:
            cp.wait()

        def zero_tail(i, _):
            cp = pltpu.make_async_copy(zero_ref, xs_ref.at[pl.ds(pl.multiple_of(i * TM_EXP, TM_EXP), TM_EXP), :], sem)
            cp.start()
            cp.wait()
            return 0

        lax.fori_loop(ends_ref[N_EXPERTS - 1] // TM_EXP, xs_ref.shape[0] // TM_EXP, zero_tail, 0)

    def issue(r, _):
        src = h_ref.at[pl.ds(r, 1), :]
        row_copy(src, dest_ref[0, 2 * r]).start(priority=0)
        row_copy(src, dest_ref[0, 2 * r + 1]).start(priority=1)
        return 0

    lax.fori_loop(0, tm, issue, 0, unroll=ROW_DMA_UNROLL)

    def drain(r, _):
        src = h_ref.at[pl.ds(r, 1), :]
        row_copy(src, dest_ref[0, 2 * r]).wait()
        row_copy(src, dest_ref[0, 2 * r + 1]).wait()
        return 0

    lax.fori_loop(0, tm, drain, 0, unroll=ROW_DMA_UNROLL)


def _scatter_rows(h, dest2, ends, n_rows):
    t = h.shape[0]
    tm = min(TM_ROW, t)
    nt = t // tm
    dest3 = dest2.reshape(nt, 1, 2 * tm)
    grid_spec = pltpu.PrefetchScalarGridSpec(
        num_scalar_prefetch=1,
        grid=(nt,),
        in_specs=[pl.BlockSpec((None, 1, 2 * tm), lambda i, ends: (i, 0, 0), memory_space=pltpu.SMEM),
                  pl.BlockSpec((tm, D_MODEL), lambda i, ends: (i, 0))],
        out_specs=pl.BlockSpec(memory_space=pl.ANY),
        scratch_shapes=[pltpu.VMEM((TM_EXP, D_MODEL), F32), pltpu.SemaphoreType.DMA(())],
    )
    return pl.pallas_call(
        _scatter_kernel,
        grid_spec=grid_spec,
        out_shape=jax.ShapeDtypeStruct((n_rows, D_MODEL), F32),
        compiler_params=_cparams(("arbitrary",)),
        name="scatter_rows",
    )(ends, dest3, h)


def _expert_kernel(te_ref, na_ref, xs_ref, wg_ref, wu_ref, wd_ref, ys_ref, acc_ref):
    i = pl.program_id(0)
    j = pl.program_id(1)

    @pl.when(i < na_ref[0])
    def _():
        xb = xs_ref[...].astype(BF16)
        g = jnp.dot(xb, wg_ref[...], preferred_element_type=F32)
        u = jnp.dot(xb, wu_ref[...], preferred_element_type=F32)
        a = (g * jax.nn.sigmoid(g) * u).astype(BF16)
        part = jnp.dot(a, wd_ref[...], preferred_element_type=F32)

        @pl.when(j == 0)
        def _():
            acc_ref[...] = part

        @pl.when(j > 0)
        def _():
            acc_ref[...] = acc_ref[...] + part

        @pl.when(j == pl.num_programs(1) - 1)
        def _():
            ys_ref[...] = acc_ref[...]

    @pl.when(i >= na_ref[0])
    def _():
        ys_ref[...] = jnp.zeros_like(ys_ref)


def _expert_ffn(xs, tile_expert, n_active, wg, wu, wd):
    n_rows = xs.shape[0]
    ff = wg.shape[2]
    tf = ff // 2
    nj = ff // tf
    nt = n_rows // TM_EXP

    def wj(i, j, na):
        return jnp.where(i < na[0], j, nj - 1)

    row = pl.BlockSpec((TM_EXP, D_MODEL), lambda i, j, te, na: (i, 0))
    grid_spec = pltpu.PrefetchScalarGridSpec(
        num_scalar_prefetch=2,
        grid=(nt, nj),
        in_specs=[row,
                  pl.BlockSpec((None, D_MODEL, tf), lambda i, j, te, na: (te[i], 0, wj(i, j, na))),
                  pl.BlockSpec((None, D_MODEL, tf), lambda i, j, te, na: (te[i], 0, wj(i, j, na))),
                  pl.BlockSpec((None, tf, D_MODEL), lambda i, j, te, na: (te[i], wj(i, j, na), 0))],
        out_specs=row,
        scratch_shapes=[pltpu.VMEM((TM_EXP, D_MODEL), F32)],
    )
    return pl.pallas_call(
        _expert_kernel,
        grid_spec=grid_spec,
        out_shape=jax.ShapeDtypeStruct((n_rows, D_MODEL), F32),
        compiler_params=_cparams(("arbitrary", "arbitrary")),
        name="expert_ffn",
    )(tile_expert, n_active, xs, wg, wu, wd)


def _combine_kernel(dest_ref, x_ref, g12_ref, gfin_ref, ys_ref, o_ref, buf_ref, sem, *, final):
    tm = x_ref.shape[0]

    def row_copy(r, k):
        return pltpu.make_async_copy(ys_ref.at[pl.ds(dest_ref[0, 2 * r + k], 1), :],
                                     buf_ref.at[k, pl.ds(r, 1), :], sem)

    def issue(r, _):
        row_copy(r, 0).start(priority=0)
        row_copy(r, 1).start(priority=1)
        return 0

    lax.fori_loop(0, tm, issue, 0, unroll=ROW_DMA_UNROLL)

    def drain(r, _):
        row_copy(r, 0).wait()
        row_copy(r, 1).wait()
        return 0

    lax.fori_loop(0, tm, drain, 0, unroll=ROW_DMA_UNROLL)
    g12 = g12_ref[...]
    y = x_ref[...] + g12[:, 0:1] * buf_ref[0] + g12[:, 1:2] * buf_ref[1]
    o_ref[...] = _rms(y, gfin_ref[...], NORM_EPS) if final else y


def _combine(x2, ys, dest2, g12, g_final, final):
    t = x2.shape[0]
    tm = min(TM_ROW, t)
    nt = t // tm
    dest3 = dest2.reshape(nt, 1, 2 * tm)
    row = pl.BlockSpec((tm, D_MODEL), lambda i: (i, 0))
    return pl.pallas_call(
        functools.partial(_combine_kernel, final=final),
        grid=(nt,),
        in_specs=[pl.BlockSpec((None, 1, 2 * tm), lambda i: (i, 0, 0), memory_space=pltpu.SMEM),
                  row, pl.BlockSpec((tm, 2), lambda i: (i, 0)),
                  pl.BlockSpec((1, D_MODEL), lambda i: (0, 0)),
                  pl.BlockSpec(memory_space=pl.ANY)],
        out_specs=row,
        out_shape=jax.ShapeDtypeStruct((t, D_MODEL), F32),
        scratch_shapes=[pltpu.VMEM((2, tm, D_MODEL), F32), pltpu.SemaphoreType.DMA(())],
        compiler_params=_cparams(("arbitrary",)),
        name="combine",
    )(dest3, x2, g12, g_final, ys)


def _final_norm_kernel(x_ref, g_ref, o_ref):
    o_ref[...] = _rms(x_ref[...], g_ref[...], NORM_EPS)


def _final_norm(x2, g):
    t = x2.shape[0]
    tm = min(TM_PREP, t)
    row = pl.BlockSpec((tm, D_MODEL), lambda i: (i, 0))
    return pl.pallas_call(
        _final_norm_kernel, grid=(t // tm,),
        in_specs=[row, pl.BlockSpec((1, D_MODEL), lambda i: (0, 0))], out_specs=row,
        out_shape=jax.ShapeDtypeStruct((t, D_MODEL), F32),
        compiler_params=_cparams(("parallel",)), name="final_norm",
    )(x2, g)


def _moe_layer(x2, h, router, wg, wu, wd, g_final, final):
    t = x2.shape[0]
    w_router_pad = jnp.zeros((D_MODEL, LANES), F32).at[:, :N_EXPERTS].set(router.astype(F32))
    gates, rank, cnt = _router(h, w_router_pad)
    counts = cnt[0, :N_EXPERTS].astype(jnp.int32)
    padded = ((counts + TM_EXP - 1) // TM_EXP) * TM_EXP
    ends = jnp.cumsum(padded)
    starts = ends - padded
    n_rows = TOP_K * t + N_EXPERTS * TM_EXP
    nt = n_rows // TM_EXP
    n_active = (ends[-1] // TM_EXP).astype(jnp.int32)
    tile_start = jnp.minimum(jnp.arange(nt, dtype=jnp.int32), n_active - 1) * TM_EXP
    tile_expert = jnp.minimum(jnp.sum(ends[None, :] <= tile_start[:, None], axis=1), N_EXPERTS - 1).astype(jnp.int32)
    rank8 = rank[:, :N_EXPERTS]
    sel8 = rank8 >= 0
    dest8 = starts[None, :] + rank8
    e_idx = jnp.arange(N_EXPERTS, dtype=jnp.int32)[None, :]
    first = jnp.min(jnp.where(sel8, e_idx, N_EXPERTS), axis=1, keepdims=True)
    second = jnp.max(jnp.where(sel8, e_idx, -1), axis=1, keepdims=True)
    pick = lambda a, e: jnp.sum(jnp.where(e_idx == e, a, 0), axis=1, keepdims=True)
    dest2 = jnp.concatenate([pick(dest8, first), pick(dest8, second)], axis=1).astype(jnp.int32)
    gates8 = gates[:, :N_EXPERTS]
    g12 = jnp.concatenate([pick(gates8, first), pick(gates8, second)], axis=1)
    xs = _scatter_rows(h, dest2, ends.astype(jnp.int32), n_rows)
    ys = _expert_ffn(xs, tile_expert, n_active.reshape(1), wg, wu, wd)
    return _combine(x2, ys, dest2, g12, g_final, final)


def kernel(x, positions, w_in, w_out, norm_mix, norm_ffn, da_lambda_q1, da_lambda_k1, da_lambda_q2,
           da_lambda_k2, da_subln, mla_q_norm, mla_w_uq, mla_kv_norm, mla_w_ukv, ffn_w_gate, ffn_w_up,
           ffn_w_down, moe_router, moe_w_gate, moe_w_up, moe_w_down, norm_final):
    batch, seq, d = x.shape
    depth = w_in.shape[0]
    t = batch * seq
    x2 = x.reshape(t, d).astype(F32)
    tabs = _rope_tables(positions)

    n_qk = 2 * DA_HEADS * 2 * DA_HEAD_DIM
    n_v = DA_HEADS * DA_V_DIM
    c0 = n_qk + n_v
    kv_per_head = MLA_NOPE_DIM + MLA_V_DIM

    for l in range(depth):
        wl = w_in[l]
        w_kr = jnp.zeros((d, LANES), F32).at[:, MLA_NOPE_DIM:MLA_QK_DIM].set(
            wl[:, c0 + MLA_Q_RANK + MLA_KV_RANK:])
        wc = jnp.concatenate([wl[:, c0:c0 + MLA_Q_RANK + MLA_KV_RANK], w_kr], axis=1)
        wuq = jnp.pad(mla_w_uq[l].reshape(MLA_Q_RANK, MLA_HEADS, MLA_QK_DIM),
                      ((0, 0), (0, 0), (0, LANES - MLA_QK_DIM))).reshape(MLA_Q_RANK, MLA_HEADS * LANES)
        wukv = mla_w_ukv[l].reshape(MLA_KV_RANK, MLA_HEADS, kv_per_head)
        wuk = jnp.pad(wukv[:, :, :MLA_NOPE_DIM],
                      ((0, 0), (0, 0), (0, LANES - MLA_NOPE_DIM))).reshape(MLA_KV_RANK, MLA_HEADS * LANES)
        wuv = wukv[:, :, MLA_NOPE_DIM:].reshape(MLA_KV_RANK, MLA_HEADS * MLA_V_DIM)
        w = dict(wqk=wl[:, :n_qk].astype(BF16), wv=wl[:, n_qk:c0].astype(BF16), wc=wc.astype(BF16),
                 gq=mla_q_norm[l].reshape(1, -1).astype(F32), wuq=wuq.astype(BF16),
                 gkv=mla_kv_norm[l].reshape(1, -1).astype(F32), wuk=wuk.astype(BF16), wuv=wuv.astype(BF16))

        da_q, da_k, da_vt, ml_q, ml_k, ml_vt = _prep(
            x2, norm_mix[l].reshape(1, d).astype(F32), w, tabs, batch, seq)

        lambda_init = 0.8 - 0.6 * float(np.exp(-0.3 * l))
        lam = (jnp.exp(jnp.sum(da_lambda_q1[l].astype(F32) * da_lambda_k1[l].astype(F32)))
               - jnp.exp(jnp.sum(da_lambda_q2[l].astype(F32) * da_lambda_k2[l].astype(F32)))
               + lambda_init).reshape(1).astype(F32)
        o_da = _attention(da_q, da_k, da_vt, batch, seq, 2, lam=lam,
                          subln=da_subln[l].reshape(DA_V_DIM, 1).astype(F32),
                          one_minus_lambda_init=1.0 - lambda_init)
        o_ml = _attention(ml_q, ml_k, ml_vt, batch, seq, 1)

        is_moe = l % 2 == 1
        wo = w_out[l].astype(BF16)
        x2, h = _outproj(x2, o_da, o_ml, wo[:DA_WIDTH], wo[DA_WIDTH:],
                         norm_ffn[l].reshape(1, d).astype(F32), F32 if is_moe else BF16)
        i = l // 2
        last = l == depth - 1
        if not is_moe:
            x2 = _dense_ffn(x2, h, ffn_w_gate[i].astype(BF16), ffn_w_up[i].astype(BF16),
                            ffn_w_down[i].astype(BF16))
            if last:
                x2 = _final_norm(x2, norm_final.reshape(1, d).astype(F32))
        else:
            x2 = _moe_layer(x2, h, moe_router[i], moe_w_gate[i].astype(BF16), moe_w_up[i].astype(BF16),
                            moe_w_down[i].astype(BF16), norm_final.reshape(1, d).astype(F32), last)
    return x2.reshape(batch, seq, d).astype(x.dtype)
```

```python
import functools
import math

import numpy as np
import jax
import jax.numpy as jnp
from jax import lax
from jax.experimental import pallas as pl
from jax.experimental.pallas import tpu as pltpu

D_MODEL = 1024
ROPE_THETA = 10000.0
NORM_EPS = 1e-6
SUBLN_EPS = 1e-5
DA_HEAD_DIM = 64
DA_V_DIM = 128
DA_HEADS = 4
DA_WIDTH = 512
MLA_HEADS = 4
MLA_V_DIM = 128
MLA_WIDTH = 512
MLA_Q_RANK = 256
MLA_KV_RANK = 128
MLA_NOPE_DIM = 64
MLA_ROPE_DIM = 32
MLA_QK_DIM = 96
N_EXPERTS = 8
TOP_K = 2
LOG2E = 1.4426950408889634

LANES = 128
SUBLANES = 8
VMEM_LIMIT = 56 * 1024 * 1024

TM_PREP = 512
PREP_SPLIT = 1
TQ = 512
ATTN_CHUNK = 256
SHIFT_KEYS = 64
ATTN_TRIP = 16
TM_FFN = 512
TM_EXP = 512
TM_ROW = 256
TM_ROUTE = 512
ROW_DMA_UNROLL = 8

BF16 = jnp.bfloat16
F32 = jnp.float32


def _cparams(sem):
    return pltpu.CompilerParams(dimension_semantics=sem, vmem_limit_bytes=VMEM_LIMIT)


def _rms(x, g, eps):
    return x * lax.rsqrt(jnp.mean(x * x, axis=-1, keepdims=True) + eps) * g


def _rope_table_kernel(pos_ref, inv_da_ref, sgn_da_ref, inv_ml_ref, sgn_ml_ref,
                       cos_da_ref, sin_da_ref, cos_ml_ref, sin_ml_ref):
    pos = pos_ref[...]
    ang_da = pos * inv_da_ref[...]
    cos_da_ref[...] = jnp.cos(ang_da)
    sin_da_ref[...] = jnp.sin(ang_da) * sgn_da_ref[...]
    ang_ml = pos * inv_ml_ref[...]
    cos_ml_ref[...] = jnp.cos(ang_ml)
    sin_ml_ref[...] = jnp.sin(ang_ml) * sgn_ml_ref[...]


def _rope_tables(positions):
    t = positions.size
    tm = min(TM_PREP, t)
    pos = positions.reshape(t, 1).astype(F32)
    lane = np.arange(LANES)
    inv_da = 1.0 / (ROPE_THETA ** (jnp.arange(0, DA_HEAD_DIM, 2, dtype=F32) / DA_HEAD_DIM))
    inv_ml = 1.0 / (ROPE_THETA ** (jnp.arange(0, MLA_ROPE_DIM, 2, dtype=F32) / MLA_ROPE_DIM))
    half_da = DA_HEAD_DIM // 2
    half_ml = MLA_ROPE_DIM // 2
    inv_da_l = inv_da[lane % half_da][None, :]
    sgn_da = jnp.asarray(np.where(lane % DA_HEAD_DIM < half_da, -1.0, 1.0), F32)[None, :]
    in_rope = (lane >= MLA_NOPE_DIM) & (lane < MLA_QK_DIM)
    inv_ml_l = jnp.where(in_rope, inv_ml[(lane - MLA_NOPE_DIM) % half_ml], 0.0)[None, :]
    sgn_ml = jnp.asarray(
        np.where(in_rope, np.where(lane < MLA_NOPE_DIM + half_ml, -1.0, 1.0), 0.0), F32)[None, :]
    row = pl.BlockSpec((tm, LANES), lambda i: (i, 0))
    const = pl.BlockSpec((1, LANES), lambda i: (0, 0))
    out = jax.ShapeDtypeStruct((t, LANES), F32)
    return pl.pallas_call(
        _rope_table_kernel,
        grid=(t // tm,),
        in_specs=[pl.BlockSpec((tm, 1), lambda i: (i, 0)), const, const, const, const],
        out_specs=[row, row, row, row],
        out_shape=[out, out, out, out],
        compiler_params=_cparams(("parallel",)),
        name="rope_tables",
    )(pos, inv_da_l, sgn_da, inv_ml_l, sgn_ml)


def _rot_da(blk, lane):
    fwd = pltpu.roll(blk, LANES - DA_HEAD_DIM // 2, 1)
    bwd = pltpu.roll(blk, DA_HEAD_DIM // 2, 1)
    return jnp.where(lane % DA_HEAD_DIM < DA_HEAD_DIM // 2, fwd, bwd)


def _rot_ml(blk, lane):
    fwd = pltpu.roll(blk, LANES - MLA_ROPE_DIM // 2, 1)
    bwd = pltpu.roll(blk, MLA_ROPE_DIM // 2, 1)
    return jnp.where(lane < MLA_NOPE_DIM + MLA_ROPE_DIM // 2, fwd, bwd)


def _prep_kernel(x_ref, g_ref, wqk_ref, wv_ref, wc_ref, gq_ref, wuq_ref, gkv_ref, wuk_ref, wuv_ref,
                 cda_ref, sda_ref, cml_ref, sml_ref,
                 daq_ref, dak_ref, davt_ref, mlq_ref, mlk_ref, mlvt_ref):
    tm = x_ref.shape[0]
    sub = tm // PREP_SPLIT
    lane = lax.broadcasted_iota(jnp.int32, (sub, LANES), 1)
    q_scale = DA_HEAD_DIM ** -0.5 * LOG2E
    ml_scale = MLA_QK_DIM ** -0.5 * LOG2E

    for part in range(PREP_SPLIT):
        rows = slice(part * sub, (part + 1) * sub)
        hb = _rms(x_ref[rows, :], g_ref[...], NORM_EPS).astype(BF16)

        qk = jnp.dot(hb, wqk_ref[...], preferred_element_type=F32)
        cda, sda = cda_ref[rows, :], sda_ref[rows, :]
        for j in range(2 * DA_HEADS):
            blk = qk[:, j * LANES:(j + 1) * LANES]
            r = blk * cda + _rot_da(blk, lane) * sda
            if j < DA_HEADS:
                daq_ref[rows, j * LANES:(j + 1) * LANES] = (r * q_scale).astype(BF16)
            else:
                jj = j - DA_HEADS
                dak_ref[rows, jj * LANES:(jj + 1) * LANES] = r.astype(BF16)
        davt_ref[:, rows] = lax.dot_general(wv_ref[...], hb, (((1,), (1,)), ((), ())),
                                            preferred_element_type=F32).astype(BF16)

        c = jnp.dot(hb, wc_ref[...], preferred_element_type=F32)
        cml, sml = cml_ref[rows, :], sml_ref[rows, :]
        cq = _rms(c[:, :MLA_Q_RANK], gq_ref[...], NORM_EPS).astype(BF16)
        qm = jnp.dot(cq, wuq_ref[...], preferred_element_type=F32)
        for j in range(MLA_HEADS):
            blk = qm[:, j * LANES:(j + 1) * LANES]
            r = blk * cml + _rot_ml(blk, lane) * sml
            mlq_ref[rows, j * LANES:(j + 1) * LANES] = (r * ml_scale).astype(BF16)
        ckv = _rms(c[:, MLA_Q_RANK:MLA_Q_RANK + MLA_KV_RANK], gkv_ref[...], NORM_EPS).astype(BF16)
        kr = c[:, MLA_Q_RANK + MLA_KV_RANK:]
        kr = kr * cml + _rot_ml(kr, lane) * sml
        kn = jnp.dot(ckv, wuk_ref[...], preferred_element_type=F32)
        for j in range(MLA_HEADS):
            mlk_ref[rows, j * LANES:(j + 1) * LANES] = (kn[:, j * LANES:(j + 1) * LANES] + kr).astype(BF16)
        mlvt_ref[:, rows] = lax.dot_general(wuv_ref[...], ckv, (((1,), (1,)), ((), ())),
                                            preferred_element_type=F32).astype(BF16)


def _prep(x2, g_mix, w, tabs, batch, seq):
    t = x2.shape[0]
    tm = min(TM_PREP, seq)
    nps = seq // tm
    row = lambda width: pl.BlockSpec((tm, width), lambda i: (i, 0))
    full = lambda a: pl.BlockSpec(a.shape, lambda i: (0,) * a.ndim)
    vt_spec = pl.BlockSpec((None, DA_WIDTH, tm), lambda i: (i // nps, 0, i % nps))
    tok = jax.ShapeDtypeStruct((t, DA_WIDTH), BF16)
    vts = jax.ShapeDtypeStruct((batch, DA_WIDTH, seq), BF16)
    weights = [w["wqk"], w["wv"], w["wc"], w["gq"], w["wuq"], w["gkv"], w["wuk"], w["wuv"]]
    return pl.pallas_call(
        _prep_kernel,
        grid=(t // tm,),
        in_specs=[row(D_MODEL), full(g_mix)] + [full(a) for a in weights] + [row(LANES)] * 4,
        out_specs=[row(DA_WIDTH), row(DA_WIDTH), vt_spec, row(DA_WIDTH), row(DA_WIDTH), vt_spec],
        out_shape=[tok, tok, vts, tok, tok, vts],
        compiler_params=_cparams(("parallel",)),
        name="prep",
    )(x2, g_mix, *weights, *tabs)


def _attn_kernel(*refs, n_maps, one_minus_lambda_init):
    if n_maps == 2:
        lam_ref, q_ref, k_ref, vt_ref, g_ref, o_ref = refs[:6]
    else:
        q_ref, k_ref, vt_ref, o_ref = refs[:4]
    p_ref, acc_ref, l_ref, m_ref, qm_ref = refs[-5:]
    seq = k_ref.shape[0]
    tq = q_ref.shape[0]
    tkc = ATTN_CHUNK
    n_chunks = seq // tkc
    q = q_ref[...]
    if n_maps == 2:
        lane = lax.broadcasted_iota(jnp.int32, q.shape, 1)
        zero = jnp.zeros_like(q)
        qm_ref[0] = jnp.where(lane < DA_HEAD_DIM, q, zero)
        qm_ref[1] = jnp.where(lane >= DA_HEAD_DIM, q, zero)
    else:
        qm_ref[0] = q

    def key_rows(c, rows):
        return k_ref[pl.ds(pl.multiple_of(c * tkc, rows), rows), :]

    def scores(k, mp):
        return lax.dot_general(k, qm_ref[mp], (((1,), (1,)), ((), ())), preferred_element_type=F32)

    def sublane_sums(p):
        return jnp.sum(p.reshape(p.shape[0] // SUBLANES, SUBLANES, tq), axis=0)

    for mp in range(n_maps):
        m_ref[mp] = jnp.max(scores(key_rows(0, SHIFT_KEYS), mp), axis=0, keepdims=True)
    l_ref[...] = jnp.zeros(l_ref.shape, F32)

    per_trip = math.gcd(ATTN_TRIP, n_chunks)

    def trip(t, _):
        for off in range(per_trip):
            c = t * per_trip + off
            k = key_rows(c, tkc)
            for mp in range(n_maps):
                p = jnp.exp2(scores(k, mp) - m_ref[mp])
                l_ref[mp] = l_ref[mp] + sublane_sums(p)
                p_ref[mp, pl.ds(pl.multiple_of(c * tkc, tkc), tkc), :] = p.astype(BF16)
        return 0

    lax.fori_loop(0, n_chunks // per_trip, trip, 0)
    for mp in range(n_maps):
        acc_ref[mp] = jnp.dot(vt_ref[...], p_ref[mp], preferred_element_type=F32)

    def non_finite(x):
        return jnp.max(jnp.where(jnp.isfinite(x), 0.0, 1.0)) > 0.0

    @pl.when(non_finite(acc_ref[...]) | non_finite(l_ref[...]))
    def _():
        m_ref[...] = jnp.full(m_ref.shape, -jnp.inf, F32)
        l_ref[...] = jnp.zeros(l_ref.shape, F32)
        acc_ref[...] = jnp.zeros(acc_ref.shape, F32)

        def chunk(c, _):
            vt = vt_ref[:, pl.ds(pl.multiple_of(c * tkc, tkc), tkc)]
            k = key_rows(c, tkc)
            for mp in range(n_maps):
                s = scores(k, mp)
                m_old = m_ref[mp]
                m_new = jnp.maximum(m_old, jnp.max(s, axis=0, keepdims=True))
                alpha = jnp.exp2(m_old - m_new)
                p = jnp.exp2(s - m_new)
                l_ref[mp] = alpha * l_ref[mp] + sublane_sums(p)
                acc_ref[mp] = alpha * acc_ref[mp] + jnp.dot(vt, p.astype(BF16), preferred_element_type=F32)
                m_ref[mp] = m_new
            return 0

        lax.fori_loop(0, n_chunks, chunk, 0)

    def normalized(mp):
        return acc_ref[mp] / jnp.sum(l_ref[mp], axis=0, keepdims=True)

    if n_maps == 2:
        o = normalized(0) - lam_ref[0] * normalized(1)
        ms = jnp.mean(o * o, axis=0, keepdims=True)
        o = o * lax.rsqrt(ms + SUBLN_EPS) * g_ref[...] * one_minus_lambda_init
    else:
        o = normalized(0)
    o_ref[...] = o.T.astype(o_ref.dtype)


def _attention(q, k, vt, batch, seq, n_maps, lam=None, subln=None, one_minus_lambda_init=1.0):
    t = q.shape[0]
    tq = min(TQ, seq)
    nq = seq // tq
    heads = q.shape[1] // LANES
    q_spec = pl.BlockSpec((tq, LANES), lambda b, h, i: (b * nq + i, h))
    k_spec = pl.BlockSpec((seq, LANES), lambda b, h, i: (b, h))
    vt_spec = pl.BlockSpec((None, DA_V_DIM, seq), lambda b, h, i: (b, h, 0))
    in_specs = [q_spec, k_spec, vt_spec]
    args = [q, k, vt]
    if n_maps == 2:
        in_specs = [pl.BlockSpec(memory_space=pltpu.SMEM)] + in_specs + [
            pl.BlockSpec((DA_V_DIM, 1), lambda b, h, i: (0, 0))]
        args = [lam] + args + [subln]
    scratch = [pltpu.VMEM((n_maps, seq, tq), BF16),
               pltpu.VMEM((n_maps, DA_V_DIM, tq), F32),
               pltpu.VMEM((n_maps, SUBLANES, tq), F32),
               pltpu.VMEM((n_maps, 1, tq), F32),
               pltpu.VMEM((n_maps, tq, LANES), BF16)]
    return pl.pallas_call(
        functools.partial(_attn_kernel, n_maps=n_maps, one_minus_lambda_init=one_minus_lambda_init),
        grid=(batch, heads, nq),
        in_specs=in_specs,
        out_specs=q_spec,
        scratch_shapes=scratch,
        out_shape=jax.ShapeDtypeStruct((t, q.shape[1]), BF16),
        compiler_params=_cparams(("parallel", "parallel", "parallel")),
        name="diff_attn" if n_maps == 2 else "mla_attn",
    )(*args)


def _outproj_kernel(x_ref, oda_ref, oml_ref, wa_ref, wb_ref, g_ref, xo_ref, h_ref):
    y = x_ref[...] + jnp.dot(oda_ref[...], wa_ref[...], preferred_element_type=F32)
    y = y + jnp.dot(oml_ref[...], wb_ref[...], preferred_element_type=F32)
    xo_ref[...] = y
    h_ref[...] = _rms(y, g_ref[...], NORM_EPS).astype(h_ref.dtype)


def _outproj(x2, o_da, o_ml, w_a, w_b, g_ffn, h_dtype):
    t = x2.shape[0]
    tm = min(TM_PREP, t)
    row = lambda width: pl.BlockSpec((tm, width), lambda i: (i, 0))
    full = lambda a: pl.BlockSpec(a.shape, lambda i: (0,) * a.ndim)
    return pl.pallas_call(
        _outproj_kernel,
        grid=(t // tm,),
        in_specs=[row(D_MODEL), row(DA_WIDTH), row(MLA_WIDTH), full(w_a), full(w_b), full(g_ffn)],
        out_specs=[row(D_MODEL), row(D_MODEL)],
        out_shape=[jax.ShapeDtypeStruct((t, D_MODEL), F32), jax.ShapeDtypeStruct((t, D_MODEL), h_dtype)],
        compiler_params=_cparams(("parallel",)),
        name="outproj",
    )(x2, o_da, o_ml, w_a, w_b, g_ffn)


def _ffn_kernel(x_ref, h_ref, wg_ref, wu_ref, wd_ref, o_ref):
    h = h_ref[...]
    g = jnp.dot(h, wg_ref[...], preferred_element_type=F32)
    u = jnp.dot(h, wu_ref[...], preferred_element_type=F32)
    a = (g * jax.nn.sigmoid(g) * u).astype(BF16)
    o_ref[...] = x_ref[...] + jnp.dot(a, wd_ref[...], preferred_element_type=F32)


def _dense_ffn(x2, h, wg, wu, wd):
    t = x2.shape[0]
    tm = min(TM_FFN, t)
    row = pl.BlockSpec((tm, D_MODEL), lambda i: (i, 0))
    res = lambda a: pl.BlockSpec(a.shape, lambda i: (0, 0), pipeline_mode=pl.Buffered(1))
    return pl.pallas_call(
        _ffn_kernel,
        grid=(t // tm,),
        in_specs=[row, row, res(wg), res(wu), res(wd)],
        out_specs=row,
        out_shape=jax.ShapeDtypeStruct((t, D_MODEL), F32),
        compiler_params=_cparams(("parallel",)),
        name="dense_ffn",
    )(x2, h, wg, wu, wd)


def _router_kernel(h_ref, wr_ref, gates_ref, rank_ref, cnt_ref):
    tm = h_ref.shape[0]

    @pl.when(pl.program_id(0) == 0)
    def _():
        cnt_ref[...] = jnp.zeros_like(cnt_ref)

    lane = lax.broadcasted_iota(jnp.int32, (tm, LANES), 1)
    logits = jnp.dot(h_ref[...], wr_ref[...], preferred_element_type=F32, precision=lax.Precision.HIGHEST)
    neg = jnp.float32(-jnp.inf)
    logits = jnp.where(lane < N_EXPERTS, logits, neg)
    v1 = jnp.max(logits, axis=1, keepdims=True)
    i1 = jnp.min(jnp.where(logits == v1, lane, LANES), axis=1, keepdims=True)
    sel1 = lane == i1
    rest = jnp.where(sel1, neg, logits)
    v2 = jnp.max(rest, axis=1, keepdims=True)
    i2 = jnp.min(jnp.where(rest == v2, lane, LANES), axis=1, keepdims=True)
    sel2 = lane == i2
    e = jnp.exp(v2 - v1)
    g1 = 1.0 / (1.0 + e)
    g2 = e / (1.0 + e)
    gates_ref[...] = jnp.where(sel1, g1, jnp.where(sel2, g2, 0.0))
    sel = jnp.where(sel1 | sel2, 1.0, 0.0)
    r_i = lax.broadcasted_iota(jnp.int32, (tm, tm), 0)
    c_i = lax.broadcasted_iota(jnp.int32, (tm, tm), 1)
    tri = jnp.where(c_i < r_i, 1.0, 0.0).astype(BF16)
    before = jnp.dot(tri, sel.astype(BF16), preferred_element_type=F32) + cnt_ref[0:1, :]
    rank_ref[...] = jnp.where(sel > 0, before, -1.0).astype(jnp.int32)
    cnt_ref[...] = cnt_ref[...] + jnp.sum(sel, axis=0, keepdims=True)


def _router(h, w_router_pad):
    t = h.shape[0]
    tm = min(TM_ROUTE, t)
    row = lambda width: pl.BlockSpec((tm, width), lambda i: (i, 0))
    return pl.pallas_call(
        _router_kernel,
        grid=(t // tm,),
        in_specs=[row(D_MODEL), pl.BlockSpec(w_router_pad.shape, lambda i: (0, 0))],
        out_specs=[row(LANES), row(LANES), pl.BlockSpec((8, LANES), lambda i: (0, 0))],
        out_shape=[jax.ShapeDtypeStruct((t, LANES), F32), jax.ShapeDtypeStruct((t, LANES), jnp.int32),
                   jax.ShapeDtypeStruct((8, LANES), F32)],
        compiler_params=_cparams(("arbitrary",)),
        name="router",
    )(h, w_router_pad)


def _scatter_kernel(ends_ref, dest_ref, h_ref, xs_ref, zero_ref, sem):
    tm = h_ref.shape[0]

    def row_copy(src, dst_row):
        return pltpu.make_async_copy(src, xs_ref.at[pl.ds(dst_row, 1), :], sem)

    @pl.when(pl.program_id(0) == 0)
    def _():
        zero_ref[...] = jnp.zeros_like(zero_ref)
        copies = []
        for e in range(N_EXPERTS):
            start = pl.multiple_of(jnp.maximum(ends_ref[e] - TM_EXP, 0), TM_EXP)
            cp = pltpu.make_async_copy(zero_ref, xs_ref.at[pl.ds(start, TM_EXP), :], sem)
            cp.start()
            copies.append(cp)
        for cp in copies:
            cp.wait()

        def zero_tail(i, _):
            cp = pltpu.make_async_copy(zero_ref, xs_ref.at[pl.ds(pl.multiple_of(i * TM_EXP, TM_EXP), TM_EXP), :], sem)
            cp.start()
            cp.wait()
            return 0

        lax.fori_loop(ends_ref[N_EXPERTS - 1] // TM_EXP, xs_ref.shape[0] // TM_EXP, zero_tail, 0)

    def issue(r, _):
        src = h_ref.at[pl.ds(r, 1), :]
        row_copy(src, dest_ref[0, 2 * r]).start(priority=0)
        row_copy(src, dest_ref[0, 2 * r + 1]).start(priority=1)
        return 0

    lax.fori_loop(0, tm, issue, 0, unroll=ROW_DMA_UNROLL)

    def drain(r, _):
        src = h_ref.at[pl.ds(r, 1), :]
        row_copy(src, dest_ref[0, 2 * r]).wait()
        row_copy(src, dest_ref[0, 2 * r + 1]).wait()
        return 0

    lax.fori_loop(0, tm, drain, 0, unroll=ROW_DMA_UNROLL)


def _scatter_rows(h, dest2, ends, n_rows):
    t = h.shape[0]
    tm = min(TM_ROW, t)
    nt = t // tm
    dest3 = dest2.reshape(nt, 1, 2 * tm)
    grid_spec = pltpu.PrefetchScalarGridSpec(
        num_scalar_prefetch=1,
        grid=(nt,),
        in_specs=[pl.BlockSpec((None, 1, 2 * tm), lambda i, ends: (i, 0, 0), memory_space=pltpu.SMEM),
                  pl.BlockSpec((tm, D_MODEL), lambda i, ends: (i, 0))],
        out_specs=pl.BlockSpec(memory_space=pl.ANY),
        scratch_shapes=[pltpu.VMEM((TM_EXP, D_MODEL), F32), pltpu.SemaphoreType.DMA(())],
    )
    return pl.pallas_call(
        _scatter_kernel,
        grid_spec=grid_spec,
        out_shape=jax.ShapeDtypeStruct((n_rows, D_MODEL), F32),
        compiler_params=_cparams(("arbitrary",)),
        name="scatter_rows",
    )(ends, dest3, h)


def _expert_kernel(te_ref, na_ref, xs_ref, wg_ref, wu_ref, wd_ref, ys_ref, acc_ref):
    i = pl.program_id(0)
    j = pl.program_id(1)

    @pl.when(i < na_ref[0])
    def _():
        xb = xs_ref[...].astype(BF16)
        g = jnp.dot(xb, wg_ref[...], preferred_element_type=F32)
        u = jnp.dot(xb, wu_ref[...], preferred_element_type=F32)
        a = (g * jax.nn.sigmoid(g) * u).astype(BF16)
        part = jnp.dot(a, wd_ref[...], preferred_element_type=F32)

        @pl.when(j == 0)
        def _():
            acc_ref[...] = part

        @pl.when(j > 0)
        def _():
            acc_ref[...] = acc_ref[...] + part

        @pl.when(j == pl.num_programs(1) - 1)
        def _():
            ys_ref[...] = acc_ref[...]

    @pl.when(i >= na_ref[0])
    def _():
        ys_ref[...] = jnp.zeros_like(ys_ref)


def _expert_ffn(xs, tile_expert, n_active, wg, wu, wd):
    n_rows = xs.shape[0]
    ff = wg.shape[2]
    tf = ff // 2
    nj = ff // tf
    nt = n_rows // TM_EXP

    def wj(i, j, na):
        return jnp.where(i < na[0], j, nj - 1)

    row = pl.BlockSpec((TM_EXP, D_MODEL), lambda i, j, te, na: (i, 0))
    grid_spec = pltpu.PrefetchScalarGridSpec(
        num_scalar_prefetch=2,
        grid=(nt, nj),
        in_specs=[row,
                  pl.BlockSpec((None, D_MODEL, tf), lambda i, j, te, na: (te[i], 0, wj(i, j, na))),
                  pl.BlockSpec((None, D_MODEL, tf), lambda i, j, te, na: (te[i], 0, wj(i, j, na))),
                  pl.BlockSpec((None, tf, D_MODEL), lambda i, j, te, na: (te[i], wj(i, j, na), 0))],
        out_specs=row,
        scratch_shapes=[pltpu.VMEM((TM_EXP, D_MODEL), F32)],
    )
    return pl.pallas_call(
        _expert_kernel,
        grid_spec=grid_spec,
        out_shape=jax.ShapeDtypeStruct((n_rows, D_MODEL), F32),
        compiler_params=_cparams(("arbitrary", "arbitrary")),
        name="expert_ffn",
    )(tile_expert, n_active, xs, wg, wu, wd)


def _combine_kernel(dest_ref, x_ref, g12_ref, gfin_ref, ys_ref, o_ref, buf_ref, sem, *, final):
    tm = x_ref.shape[0]

    def row_copy(r, k):
        return pltpu.make_async_copy(ys_ref.at[pl.ds(dest_ref[0, 2 * r + k], 1), :],
                                     buf_ref.at[k, pl.ds(r, 1), :], sem)

    def issue(r, _):
        row_copy(r, 0).start(priority=0)
        row_copy(r, 1).start(priority=1)
        return 0

    lax.fori_loop(0, tm, issue, 0, unroll=ROW_DMA_UNROLL)

    def drain(r, _):
        row_copy(r, 0).wait()
        row_copy(r, 1).wait()
        return 0

    lax.fori_loop(0, tm, drain, 0, unroll=ROW_DMA_UNROLL)
    g12 = g12_ref[...]
    y = x_ref[...] + g12[:, 0:1] * buf_ref[0] + g12[:, 1:2] * buf_ref[1]
    o_ref[...] = _rms(y, gfin_ref[...], NORM_EPS) if final else y


def _combine(x2, ys, dest2, g12, g_final, final):
    t = x2.shape[0]
    tm = min(TM_ROW, t)
    nt = t // tm
    dest3 = dest2.reshape(nt, 1, 2 * tm)
    row = pl.BlockSpec((tm, D_MODEL), lambda i: (i, 0))
    return pl.pallas_call(
        functools.partial(_combine_kernel, final=final),
        grid=(nt,),
        in_specs=[pl.BlockSpec((None, 1, 2 * tm), lambda i: (i, 0, 0), memory_space=pltpu.SMEM),
                  row, pl.BlockSpec((tm, 2), lambda i: (i, 0)),
                  pl.BlockSpec((1, D_MODEL), lambda i: (0, 0)),
                  pl.BlockSpec(memory_space=pl.ANY)],
        out_specs=row,
        out_shape=jax.ShapeDtypeStruct((t, D_MODEL), F32),
        scratch_shapes=[pltpu.VMEM((2, tm, D_MODEL), F32), pltpu.SemaphoreType.DMA(())],
        compiler_params=_cparams(("arbitrary",)),
        name="combine",
    )(dest3, x2, g12, g_final, ys)


def _final_norm_kernel(x_ref, g_ref, o_ref):
    o_ref[...] = _rms(x_ref[...], g_ref[...], NORM_EPS)


def _final_norm(x2, g):
    t = x2.shape[0]
    tm = min(TM_PREP, t)
    row = pl.BlockSpec((tm, D_MODEL), lambda i: (i, 0))
    return pl.pallas_call(
        _final_norm_kernel, grid=(t // tm,),
        in_specs=[row, pl.BlockSpec((1, D_MODEL), lambda i: (0, 0))], out_specs=row,
        out_shape=jax.ShapeDtypeStruct((t, D_MODEL), F32),
        compiler_params=_cparams(("parallel",)), name="final_norm",
    )(x2, g)


def _moe_layer(x2, h, router, wg, wu, wd, g_final, final):
    t = x2.shape[0]
    w_router_pad = jnp.zeros((D_MODEL, LANES), F32).at[:, :N_EXPERTS].set(router.astype(F32))
    gates, rank, cnt = _router(h, w_router_pad)
    counts = cnt[0, :N_EXPERTS].astype(jnp.int32)
    padded = ((counts + TM_EXP - 1) // TM_EXP) * TM_EXP
    ends = jnp.cumsum(padded)
    starts = ends - padded
    n_rows = TOP_K * t + N_EXPERTS * TM_EXP
    nt = n_rows // TM_EXP
    n_active = (ends[-1] // TM_EXP).astype(jnp.int32)
    tile_start = jnp.minimum(jnp.arange(nt, dtype=jnp.int32), n_active - 1) * TM_EXP
    tile_expert = jnp.minimum(jnp.sum(ends[None, :] <= tile_start[:, None], axis=1), N_EXPERTS - 1).astype(jnp.int32)
    rank8 = rank[:, :N_EXPERTS]
    sel8 = rank8 >= 0
    dest8 = starts[None, :] + rank8
    e_idx = jnp.arange(N_EXPERTS, dtype=jnp.int32)[None, :]
    first = jnp.min(jnp.where(sel8, e_idx, N_EXPERTS), axis=1, keepdims=True)
    second = jnp.max(jnp.where(sel8, e_idx, -1), axis=1, keepdims=True)
    pick = lambda a, e: jnp.sum(jnp.where(e_idx == e, a, 0), axis=1, keepdims=True)
    dest2 = jnp.concatenate([pick(dest8, first), pick(dest8, second)], axis=1).astype(jnp.int32)
    gates8 = gates[:, :N_EXPERTS]
    g12 = jnp.concatenate([pick(gates8, first), pick(gates8, second)], axis=1)
    xs = _scatter_rows(h, dest2, ends.astype(jnp.int32), n_rows)
    ys = _expert_ffn(xs, tile_expert, n_active.reshape(1), wg, wu, wd)
    return _combine(x2, ys, dest2, g12, g_final, final)


def kernel(x, positions, w_in, w_out, norm_mix, norm_ffn, da_lambda_q1, da_lambda_k1, da_lambda_q2,
           da_lambda_k2, da_subln, mla_q_norm, mla_w_uq, mla_kv_norm, mla_w_ukv, ffn_w_gate, ffn_w_up,
           ffn_w_down, moe_router, moe_w_gate, moe_w_up, moe_w_down, norm_final):
    batch, seq, d = x.shape
    depth = w_in.shape[0]
    t = batch * seq
    x2 = x.reshape(t, d).astype(F32)
    tabs = _rope_tables(positions)

    n_qk = 2 * DA_HEADS * 2 * DA_HEAD_DIM
    n_v = DA_HEADS * DA_V_DIM
    c0 = n_qk + n_v
    kv_per_head = MLA_NOPE_DIM + MLA_V_DIM

    for l in range(depth):
        wl = w_in[l]
        w_kr = jnp.zeros((d, LANES), F32).at[:, MLA_NOPE_DIM:MLA_QK_DIM].set(
            wl[:, c0 + MLA_Q_RANK + MLA_KV_RANK:])
        wc = jnp.concatenate([wl[:, c0:c0 + MLA_Q_RANK + MLA_KV_RANK], w_kr], axis=1)
        wuq = jnp.pad(mla_w_uq[l].reshape(MLA_Q_RANK, MLA_HEADS, MLA_QK_DIM),
                      ((0, 0), (0, 0), (0, LANES - MLA_QK_DIM))).reshape(MLA_Q_RANK, MLA_HEADS * LANES)
        wukv = mla_w_ukv[l].reshape(MLA_KV_RANK, MLA_HEADS, kv_per_head)
        wuk = jnp.pad(wukv[:, :, :MLA_NOPE_DIM],
                      ((0, 0), (0, 0), (0, LANES - MLA_NOPE_DIM))).reshape(MLA_KV_RANK, MLA_HEADS * LANES)
        wuv = wukv[:, :, MLA_NOPE_DIM:].reshape(MLA_KV_RANK, MLA_HEADS * MLA_V_DIM)
        w = dict(wqk=wl[:, :n_qk].astype(BF16), wv=wl[:, n_qk:c0].T.astype(BF16), wc=wc.astype(BF16),
                 gq=mla_q_norm[l].reshape(1, -1).astype(F32), wuq=wuq.astype(BF16),
                 gkv=mla_kv_norm[l].reshape(1, -1).astype(F32), wuk=wuk.astype(BF16), wuv=wuv.T.astype(BF16))

        da_q, da_k, da_vt, ml_q, ml_k, ml_vt = _prep(
            x2, norm_mix[l].reshape(1, d).astype(F32), w, tabs, batch, seq)

        lambda_init = 0.8 - 0.6 * float(np.exp(-0.3 * l))
        lam = (jnp.exp(jnp.sum(da_lambda_q1[l].astype(F32) * da_lambda_k1[l].astype(F32)))
               - jnp.exp(jnp.sum(da_lambda_q2[l].astype(F32) * da_lambda_k2[l].astype(F32)))
               + lambda_init).reshape(1).astype(F32)
        o_da = _attention(da_q, da_k, da_vt, batch, seq, 2, lam=lam,
                          subln=da_subln[l].reshape(DA_V_DIM, 1).astype(F32),
                          one_minus_lambda_init=1.0 - lambda_init)
        o_ml = _attention(ml_q, ml_k, ml_vt, batch, seq, 1)

        is_moe = l % 2 == 1
        wo = w_out[l].astype(BF16)
        x2, h = _outproj(x2, o_da, o_ml, wo[:DA_WIDTH], wo[DA_WIDTH:],
                         norm_ffn[l].reshape(1, d).astype(F32), F32 if is_moe else BF16)
        i = l // 2
        last = l == depth - 1
        if not is_moe:
            x2 = _dense_ffn(x2, h, ffn_w_gate[i].astype(BF16), ffn_w_up[i].astype(BF16),
                            ffn_w_down[i].astype(BF16))
            if last:
                x2 = _final_norm(x2, norm_final.reshape(1, d).astype(F32))
        else:
            x2 = _moe_layer(x2, h, moe_router[i], moe_w_gate[i].astype(BF16), moe_w_up[i].astype(BF16),
                            moe_w_down[i].astype(BF16), norm_final.reshape(1, d).astype(F32), last)
    return x2.reshape(batch, seq, d).astype(x.dtype)
```

```python
import functools
import math

import numpy as np
import jax
import jax.numpy as jnp
from jax import lax
from jax.experimental import pallas as pl
from jax.experimental.pallas import tpu as pltpu

D_MODEL = 1024
ROPE_THETA = 10000.0
NORM_EPS = 1e-6
SUBLN_EPS = 1e-5
DA_HEAD_DIM = 64
DA_V_DIM = 128
DA_HEADS = 4
DA_WIDTH = 512
MLA_HEADS = 4
MLA_V_DIM = 128
MLA_WIDTH = 512
MLA_Q_RANK = 256
MLA_KV_RANK = 128
MLA_NOPE_DIM = 64
MLA_ROPE_DIM = 32
MLA_QK_DIM = 96
N_EXPERTS = 8
TOP_K = 2
LOG2E = 1.4426950408889634

LANES = 128
SUBLANES = 8
VMEM_LIMIT = 56 * 1024 * 1024

TM_PREP = 512
PREP_SPLIT = 1
TQ = 512
ATTN_CHUNK = 256
SHIFT_KEYS = 64
ATTN_TRIP = 16
TM_FFN = 512
TM_EXP = 512
TM_ROW = 256
TM_ROUTE = 512
ROW_DMA_UNROLL = 8

BF16 = jnp.bfloat16
F32 = jnp.float32


def _cparams(sem):
    return pltpu.CompilerParams(dimension_semantics=sem, vmem_limit_bytes=VMEM_LIMIT)


def _rms(x, g, eps):
    return x * lax.rsqrt(jnp.mean(x * x, axis=-1, keepdims=True) + eps) * g


def _rope_table_kernel(pos_ref, inv_da_ref, sgn_da_ref, inv_ml_ref, sgn_ml_ref,
                       cos_da_ref, sin_da_ref, cos_ml_ref, sin_ml_ref):
    pos = pos_ref[...]
    ang_da = pos * inv_da_ref[...]
    cos_da_ref[...] = jnp.cos(ang_da)
    sin_da_ref[...] = jnp.sin(ang_da) * sgn_da_ref[...]
    ang_ml = pos * inv_ml_ref[...]
    cos_ml_ref[...] = jnp.cos(ang_ml)
    sin_ml_ref[...] = jnp.sin(ang_ml) * sgn_ml_ref[...]


def _rope_tables(positions):
    t = positions.size
    tm = min(TM_PREP, t)
    pos = positions.reshape(t, 1).astype(F32)
    lane = np.arange(LANES)
    inv_da = 1.0 / (ROPE_THETA ** (jnp.arange(0, DA_HEAD_DIM, 2, dtype=F32) / DA_HEAD_DIM))
    inv_ml = 1.0 / (ROPE_THETA ** (jnp.arange(0, MLA_ROPE_DIM, 2, dtype=F32) / MLA_ROPE_DIM))
    half_da = DA_HEAD_DIM // 2
    half_ml = MLA_ROPE_DIM // 2
    inv_da_l = inv_da[lane % half_da][None, :]
    sgn_da = jnp.asarray(np.where(lane % DA_HEAD_DIM < half_da, -1.0, 1.0), F32)[None, :]
    in_rope = (lane >= MLA_NOPE_DIM) & (lane < MLA_QK_DIM)
    inv_ml_l = jnp.where(in_rope, inv_ml[(lane - MLA_NOPE_DIM) % half_ml], 0.0)[None, :]
    sgn_ml = jnp.asarray(
        np.where(in_rope, np.where(lane < MLA_NOPE_DIM + half_ml, -1.0, 1.0), 0.0), F32)[None, :]
    row = pl.BlockSpec((tm, LANES), lambda i: (i, 0))
    const = pl.BlockSpec((1, LANES), lambda i: (0, 0))
    out = jax.ShapeDtypeStruct((t, LANES), F32)
    return pl.pallas_call(
        _rope_table_kernel,
        grid=(t // tm,),
        in_specs=[pl.BlockSpec((tm, 1), lambda i: (i, 0)), const, const, const, const],
        out_specs=[row, row, row, row],
        out_shape=[out, out, out, out],
        compiler_params=_cparams(("parallel",)),
        name="rope_tables",
    )(pos, inv_da_l, sgn_da, inv_ml_l, sgn_ml)


def _rot_da(blk, lane):
    fwd = pltpu.roll(blk, LANES - DA_HEAD_DIM // 2, 1)
    bwd = pltpu.roll(blk, DA_HEAD_DIM // 2, 1)
    return jnp.where(lane % DA_HEAD_DIM < DA_HEAD_DIM // 2, fwd, bwd)


def _rot_ml(blk, lane):
    fwd = pltpu.roll(blk, LANES - MLA_ROPE_DIM // 2, 1)
    bwd = pltpu.roll(blk, MLA_ROPE_DIM // 2, 1)
    return jnp.where(lane < MLA_NOPE_DIM + MLA_ROPE_DIM // 2, fwd, bwd)


def _prep_kernel(x_ref, g_ref, wqk_ref, wv_ref, wc_ref, gq_ref, wuq_ref, gkv_ref, wuk_ref, wuv_ref,
                 cda_ref, sda_ref, cml_ref, sml_ref,
                 daq_ref, dak_ref, davt_ref, mlq_ref, mlk_ref, mlvt_ref):
    tm = x_ref.shape[0]
    sub = tm // PREP_SPLIT
    lane = lax.broadcasted_iota(jnp.int32, (sub, LANES), 1)
    q_scale = DA_HEAD_DIM ** -0.5 * LOG2E
    ml_scale = MLA_QK_DIM ** -0.5 * LOG2E

    for part in range(PREP_SPLIT):
        rows = slice(part * sub, (part + 1) * sub)
        hb = _rms(x_ref[rows, :], g_ref[...], NORM_EPS).astype(BF16)

        qk = jnp.dot(hb, wqk_ref[...], preferred_element_type=F32)
        cda, sda = cda_ref[rows, :], sda_ref[rows, :]
        for j in range(2 * DA_HEADS):
            blk = qk[:, j * LANES:(j + 1) * LANES]
            r = blk * cda + _rot_da(blk, lane) * sda
            if j < DA_HEADS:
                daq_ref[rows, j * LANES:(j + 1) * LANES] = (r * q_scale).astype(BF16)
            else:
                jj = j - DA_HEADS
                dak_ref[rows, jj * LANES:(jj + 1) * LANES] = r.astype(BF16)
        davt_ref[:, rows] = lax.dot_general(wv_ref[...], hb, (((1,), (1,)), ((), ())),
                                            preferred_element_type=F32).astype(BF16)

        c = jnp.dot(hb, wc_ref[...], preferred_element_type=F32)
        cml, sml = cml_ref[rows, :], sml_ref[rows, :]
        cq = _rms(c[:, :MLA_Q_RANK], gq_ref[...], NORM_EPS).astype(BF16)
        qm = jnp.dot(cq, wuq_ref[...], preferred_element_type=F32)
        for j in range(MLA_HEADS):
            blk = qm[:, j * LANES:(j + 1) * LANES]
            r = blk * cml + _rot_ml(blk, lane) * sml
            mlq_ref[rows, j * LANES:(j + 1) * LANES] = (r * ml_scale).astype(BF16)
        ckv = _rms(c[:, MLA_Q_RANK:MLA_Q_RANK + MLA_KV_RANK], gkv_ref[...], NORM_EPS).astype(BF16)
        kr = c[:, MLA_Q_RANK + MLA_KV_RANK:]
        kr = kr * cml + _rot_ml(kr, lane) * sml
        kn = jnp.dot(ckv, wuk_ref[...], preferred_element_type=F32)
        for j in range(MLA_HEADS):
            mlk_ref[rows, j * LANES:(j + 1) * LANES] = (kn[:, j * LANES:(j + 1) * LANES] + kr).astype(BF16)
        mlvt_ref[:, rows] = lax.dot_general(wuv_ref[...], ckv, (((1,), (1,)), ((), ())),
                                            preferred_element_type=F32).astype(BF16)


def _prep(x2, g_mix, w, tabs, batch, seq):
    t = x2.shape[0]
    tm = min(TM_PREP, seq)
    nps = seq // tm
    row = lambda width: pl.BlockSpec((tm, width), lambda i: (i, 0))
    full = lambda a: pl.BlockSpec(a.shape, lambda i: (0,) * a.ndim)
    vt_spec = pl.BlockSpec((None, DA_WIDTH, tm), lambda i: (i // nps, 0, i % nps))
    tok = jax.ShapeDtypeStruct((t, DA_WIDTH), BF16)
    vts = jax.ShapeDtypeStruct((batch, DA_WIDTH, seq), BF16)
    weights = [w["wqk"], w["wv"], w["wc"], w["gq"], w["wuq"], w["gkv"], w["wuk"], w["wuv"]]
    return pl.pallas_call(
        _prep_kernel,
        grid=(t // tm,),
        in_specs=[row(D_MODEL), full(g_mix)] + [full(a) for a in weights] + [row(LANES)] * 4,
        out_specs=[row(DA_WIDTH), row(DA_WIDTH), vt_spec, row(DA_WIDTH), row(DA_WIDTH), vt_spec],
        out_shape=[tok, tok, vts, tok, tok, vts],
        compiler_params=_cparams(("parallel",)),
        name="prep",
    )(x2, g_mix, *weights, *tabs)


def _attn_kernel(*refs, n_maps, one_minus_lambda_init):
    if n_maps == 2:
        lam_ref, q_ref, k_ref, vt_ref, g_ref, o_ref = refs[:6]
    else:
        q_ref, k_ref, vt_ref, o_ref = refs[:4]
    p_ref, acc_ref, l_ref, m_ref, qm_ref = refs[-5:]
    seq = k_ref.shape[0]
    tq = q_ref.shape[0]
    tkc = ATTN_CHUNK
    n_chunks = seq // tkc
    q = q_ref[...]
    if n_maps == 2:
        lane = lax.broadcasted_iota(jnp.int32, q.shape, 1)
        zero = jnp.zeros_like(q)
        qm_ref[0] = jnp.where(lane < DA_HEAD_DIM, q, zero)
        qm_ref[1] = jnp.where(lane >= DA_HEAD_DIM, q, zero)
    else:
        qm_ref[0] = q

    def key_rows(c, rows):
        return k_ref[pl.ds(pl.multiple_of(c * tkc, rows), rows), :]

    def scores(k, mp):
        return lax.dot_general(k, qm_ref[mp], (((1,), (1,)), ((), ())), preferred_element_type=F32)

    def sublane_sums(p):
        return jnp.sum(p.reshape(p.shape[0] // SUBLANES, SUBLANES, tq), axis=0)

    for mp in range(n_maps):
        m_ref[mp] = jnp.max(scores(key_rows(0, SHIFT_KEYS), mp), axis=0, keepdims=True)
    l_ref[...] = jnp.zeros(l_ref.shape, F32)

    per_trip = math.gcd(ATTN_TRIP, n_chunks)

    def trip(t, _):
        for off in range(per_trip):
            c = t * per_trip + off
            k = key_rows(c, tkc)
            for mp in range(n_maps):
                p = jnp.exp2(scores(k, mp) - m_ref[mp])
                l_ref[mp] = l_ref[mp] + sublane_sums(p)
                p_ref[mp, pl.ds(pl.multiple_of(c * tkc, tkc), tkc), :] = p.astype(BF16)
        return 0

    lax.fori_loop(0, n_chunks // per_trip, trip, 0)
    for mp in range(n_maps):
        acc_ref[mp] = jnp.dot(vt_ref[...], p_ref[mp], preferred_element_type=F32)

    def non_finite(x):
        return jnp.max(jnp.where(jnp.isfinite(x), 0.0, 1.0)) > 0.0

    @pl.when(non_finite(acc_ref[...]) | non_finite(l_ref[...]))
    def _():
        m_ref[...] = jnp.full(m_ref.shape, -jnp.inf, F32)
        l_ref[...] = jnp.zeros(l_ref.shape, F32)
        acc_ref[...] = jnp.zeros(acc_ref.shape, F32)

        def chunk(c, _):
            vt = vt_ref[:, pl.ds(pl.multiple_of(c * tkc, tkc), tkc)]
            k = key_rows(c, tkc)
            for mp in range(n_maps):
                s = scores(k, mp)
                m_old = m_ref[mp]
                m_new = jnp.maximum(m_old, jnp.max(s, axis=0, keepdims=True))
                alpha = jnp.exp2(m_old - m_new)
                p = jnp.exp2(s - m_new)
                l_ref[mp] = alpha * l_ref[mp] + sublane_sums(p)
                acc_ref[mp] = alpha * acc_ref[mp] + jnp.dot(vt, p.astype(BF16), preferred_element_type=F32)
                m_ref[mp] = m_new
            return 0

        lax.fori_loop(0, n_chunks, chunk, 0)

    def normalized(mp):
        return acc_ref[mp] / jnp.sum(l_ref[mp], axis=0, keepdims=True)

    if n_maps == 2:
        o = normalized(0) - lam_ref[0] * normalized(1)
        ms = jnp.mean(o * o, axis=0, keepdims=True)
        o = o * lax.rsqrt(ms + SUBLN_EPS) * g_ref[...] * one_minus_lambda_init
    else:
        o = normalized(0)
    o_ref[...] = o.T.astype(o_ref.dtype)


def _attention(q, k, vt, batch, seq, n_maps, lam=None, subln=None, one_minus_lambda_init=1.0):
    t = q.shape[0]
    tq = min(TQ, seq)
    nq = seq // tq
    heads = q.shape[1] // LANES
    q_spec = pl.BlockSpec((tq, LANES), lambda b, h, i: (b * nq + i, h))
    k_spec = pl.BlockSpec((seq, LANES), lambda b, h, i: (b, h))
    vt_spec = pl.BlockSpec((None, DA_V_DIM, seq), lambda b, h, i: (b, h, 0))
    in_specs = [q_spec, k_spec, vt_spec]
    args = [q, k, vt]
    if n_maps == 2:
        in_specs = [pl.BlockSpec(memory_space=pltpu.SMEM)] + in_specs + [
            pl.BlockSpec((DA_V_DIM, 1), lambda b, h, i: (0, 0))]
        args = [lam] + args + [subln]
    scratch = [pltpu.VMEM((n_maps, seq, tq), BF16),
               pltpu.VMEM((n_maps, DA_V_DIM, tq), F32),
               pltpu.VMEM((n_maps, SUBLANES, tq), F32),
               pltpu.VMEM((n_maps, 1, tq), F32),
               pltpu.VMEM((n_maps, tq, LANES), BF16)]
    return pl.pallas_call(
        functools.partial(_attn_kernel, n_maps=n_maps, one_minus_lambda_init=one_minus_lambda_init),
        grid=(batch, heads, nq),
        in_specs=in_specs,
        out_specs=q_spec,
        scratch_shapes=scratch,
        out_shape=jax.ShapeDtypeStruct((t, q.shape[1]), BF16),
        compiler_params=_cparams(("parallel", "parallel", "parallel")),
        name="diff_attn" if n_maps == 2 else "mla_attn",
    )(*args)


def _to_row_tiles(ref, val):
    groups = val.shape[0] // SUBLANES
    for j in range(D_MODEL // LANES):
        ref[:, j] = val[:, j * LANES:(j + 1) * LANES].reshape(groups, SUBLANES, LANES)


def _from_row_tiles(ref):
    rows = ref.shape[0] * SUBLANES
    return jnp.concatenate([ref[:, j].reshape(rows, LANES) for j in range(D_MODEL // LANES)], axis=1)


def _row_of(ref, r):
    return ref.at[lax.shift_right_logical(r, 3), :, pl.ds(jnp.bitwise_and(r, SUBLANES - 1), 1), :]


ROW_TILES = (D_MODEL // LANES, SUBLANES, LANES)


def _outproj_kernel(x_ref, oda_ref, oml_ref, wa_ref, wb_ref, g_ref, xo_ref, h_ref, *ht_ref):
    y = x_ref[...] + jnp.dot(oda_ref[...], wa_ref[...], preferred_element_type=F32)
    y = y + jnp.dot(oml_ref[...], wb_ref[...], preferred_element_type=F32)
    xo_ref[...] = y
    h = _rms(y, g_ref[...], NORM_EPS)
    h_ref[...] = h.astype(h_ref.dtype)
    if ht_ref:
        _to_row_tiles(ht_ref[0], h)


def _outproj(x2, o_da, o_ml, w_a, w_b, g_ffn, h_dtype, row_tiles):
    t = x2.shape[0]
    tm = min(TM_PREP, t)
    row = lambda width: pl.BlockSpec((tm, width), lambda i: (i, 0))
    full = lambda a: pl.BlockSpec(a.shape, lambda i: (0,) * a.ndim)
    out_specs = [row(D_MODEL), row(D_MODEL)]
    out_shape = [jax.ShapeDtypeStruct((t, D_MODEL), F32), jax.ShapeDtypeStruct((t, D_MODEL), h_dtype)]
    if row_tiles:
        out_specs.append(pl.BlockSpec((tm // SUBLANES,) + ROW_TILES, lambda i: (i, 0, 0, 0)))
        out_shape.append(jax.ShapeDtypeStruct((t // SUBLANES,) + ROW_TILES, F32))
    return pl.pallas_call(
        _outproj_kernel,
        grid=(t // tm,),
        in_specs=[row(D_MODEL), row(DA_WIDTH), row(MLA_WIDTH), full(w_a), full(w_b), full(g_ffn)],
        out_specs=out_specs,
        out_shape=out_shape,
        compiler_params=_cparams(("parallel",)),
        name="outproj",
    )(x2, o_da, o_ml, w_a, w_b, g_ffn)


def _ffn_kernel(x_ref, h_ref, wg_ref, wu_ref, wd_ref, o_ref):
    h = h_ref[...]
    g = jnp.dot(h, wg_ref[...], preferred_element_type=F32)
    u = jnp.dot(h, wu_ref[...], preferred_element_type=F32)
    a = (g * jax.nn.sigmoid(g) * u).astype(BF16)
    o_ref[...] = x_ref[...] + jnp.dot(a, wd_ref[...], preferred_element_type=F32)


def _dense_ffn(x2, h, wg, wu, wd):
    t = x2.shape[0]
    tm = min(TM_FFN, t)
    row = pl.BlockSpec((tm, D_MODEL), lambda i: (i, 0))
    res = lambda a: pl.BlockSpec(a.shape, lambda i: (0, 0), pipeline_mode=pl.Buffered(1))
    return pl.pallas_call(
        _ffn_kernel,
        grid=(t // tm,),
        in_specs=[row, row, res(wg), res(wu), res(wd)],
        out_specs=row,
        out_shape=jax.ShapeDtypeStruct((t, D_MODEL), F32),
        compiler_params=_cparams(("parallel",)),
        name="dense_ffn",
    )(x2, h, wg, wu, wd)


def _router_kernel(h_ref, wr_ref, gates_ref, rank_ref, cnt_ref):
    tm = h_ref.shape[0]

    @pl.when(pl.program_id(0) == 0)
    def _():
        cnt_ref[...] = jnp.zeros_like(cnt_ref)

    lane = lax.broadcasted_iota(jnp.int32, (tm, LANES), 1)
    logits = jnp.dot(h_ref[...], wr_ref[...], preferred_element_type=F32, precision=lax.Precision.HIGHEST)
    neg = jnp.float32(-jnp.inf)
    logits = jnp.where(lane < N_EXPERTS, logits, neg)
    v1 = jnp.max(logits, axis=1, keepdims=True)
    i1 = jnp.min(jnp.where(logits == v1, lane, LANES), axis=1, keepdims=True)
    sel1 = lane == i1
    rest = jnp.where(sel1, neg, logits)
    v2 = jnp.max(rest, axis=1, keepdims=True)
    i2 = jnp.min(jnp.where(rest == v2, lane, LANES), axis=1, keepdims=True)
    sel2 = lane == i2
    e = jnp.exp(v2 - v1)
    g1 = 1.0 / (1.0 + e)
    g2 = e / (1.0 + e)
    gates_ref[...] = jnp.where(sel1, g1, jnp.where(sel2, g2, 0.0))
    sel = jnp.where(sel1 | sel2, 1.0, 0.0)
    r_i = lax.broadcasted_iota(jnp.int32, (tm, tm), 0)
    c_i = lax.broadcasted_iota(jnp.int32, (tm, tm), 1)
    tri = jnp.where(c_i < r_i, 1.0, 0.0).astype(BF16)
    before = jnp.dot(tri, sel.astype(BF16), preferred_element_type=F32) + cnt_ref[0:1, :]
    rank_ref[...] = jnp.where(sel > 0, before, -1.0).astype(jnp.int32)
    cnt_ref[...] = cnt_ref[...] + jnp.sum(sel, axis=0, keepdims=True)


def _router(h, w_router_pad):
    t = h.shape[0]
    tm = min(TM_ROUTE, t)
    row = lambda width: pl.BlockSpec((tm, width), lambda i: (i, 0))
    return pl.pallas_call(
        _router_kernel,
        grid=(t // tm,),
        in_specs=[row(D_MODEL), pl.BlockSpec(w_router_pad.shape, lambda i: (0, 0))],
        out_specs=[row(LANES), row(LANES), pl.BlockSpec((8, LANES), lambda i: (0, 0))],
        out_shape=[jax.ShapeDtypeStruct((t, LANES), F32), jax.ShapeDtypeStruct((t, LANES), jnp.int32),
                   jax.ShapeDtypeStruct((8, LANES), F32)],
        compiler_params=_cparams(("arbitrary",)),
        name="router",
    )(h, w_router_pad)


def _scatter_kernel(ends_ref, dest_ref, h_ref, xs_ref, zero_ref, sem):
    tm = h_ref.shape[0] * SUBLANES
    tile_groups = TM_EXP // SUBLANES

    def row_copy(g, s, k):
        return pltpu.make_async_copy(h_ref.at[g, :, pl.ds(s, 1), :],
                                     _row_of(xs_ref, dest_ref[0, 2 * (SUBLANES * g + s) + k]), sem)

    @pl.when(pl.program_id(0) == 0)
    def _():
        zero_ref[...] = jnp.zeros_like(zero_ref)
        copies = []
        for e in range(N_EXPERTS):
            start = pl.multiple_of(jnp.maximum(ends_ref[e] - TM_EXP, 0) // SUBLANES, tile_groups)
            cp = pltpu.make_async_copy(zero_ref, xs_ref.at[pl.ds(start, tile_groups)], sem)
            cp.start()
            copies.append(cp)
        for cp in copies:
            cp.wait()

        def zero_tail(i, _):
            cp = pltpu.make_async_copy(
                zero_ref, xs_ref.at[pl.ds(pl.multiple_of(i * tile_groups, tile_groups), tile_groups)], sem)
            cp.start()
            cp.wait()
            return 0

        lax.fori_loop(ends_ref[N_EXPERTS - 1] // TM_EXP, xs_ref.shape[0] // tile_groups, zero_tail, 0)

    def issue(g, _):
        for s in range(SUBLANES):
            row_copy(g, s, 0).start(priority=0)
            row_copy(g, s, 1).start(priority=1)
        return 0

    lax.fori_loop(0, tm // SUBLANES, issue, 0)

    def drain(g, _):
        for s in range(SUBLANES):
            row_copy(g, s, 0).wait()
            row_copy(g, s, 1).wait()
        return 0

    lax.fori_loop(0, tm // SUBLANES, drain, 0)


def _scatter_rows(h, dest2, ends, n_rows):
    t = h.shape[0] * SUBLANES
    tm = min(TM_ROW, t)
    nt = t // tm
    dest3 = dest2.reshape(nt, 1, 2 * tm)
    grid_spec = pltpu.PrefetchScalarGridSpec(
        num_scalar_prefetch=1,
        grid=(nt,),
        in_specs=[pl.BlockSpec((None, 1, 2 * tm), lambda i, ends: (i, 0, 0), memory_space=pltpu.SMEM),
                  pl.BlockSpec((tm // SUBLANES,) + ROW_TILES, lambda i, ends: (i, 0, 0, 0))],
        out_specs=pl.BlockSpec(memory_space=pl.ANY),
        scratch_shapes=[pltpu.VMEM((TM_EXP // SUBLANES,) + ROW_TILES, F32), pltpu.SemaphoreType.DMA(())],
    )
    return pl.pallas_call(
        _scatter_kernel,
        grid_spec=grid_spec,
        out_shape=jax.ShapeDtypeStruct((n_rows // SUBLANES,) + ROW_TILES, F32),
        compiler_params=_cparams(("arbitrary",)),
        name="scatter_rows",
    )(ends, dest3, h)


def _expert_kernel(te_ref, na_ref, xs_ref, wg_ref, wu_ref, wd_ref, ys_ref, acc_ref):
    i = pl.program_id(0)
    j = pl.program_id(1)

    @pl.when(i < na_ref[0])
    def _():
        xb = _from_row_tiles(xs_ref).astype(BF16)
        g = jnp.dot(xb, wg_ref[...], preferred_element_type=F32)
        u = jnp.dot(xb, wu_ref[...], preferred_element_type=F32)
        a = (g * jax.nn.sigmoid(g) * u).astype(BF16)
        part = jnp.dot(a, wd_ref[...], preferred_element_type=F32)

        @pl.when(j == 0)
        def _():
            acc_ref[...] = part

        @pl.when(j > 0)
        def _():
            acc_ref[...] = acc_ref[...] + part

        @pl.when(j == pl.num_programs(1) - 1)
        def _():
            _to_row_tiles(ys_ref, acc_ref[...])

    @pl.when(i >= na_ref[0])
    def _():
        ys_ref[...] = jnp.zeros_like(ys_ref)


def _expert_ffn(xs, tile_expert, n_active, wg, wu, wd):
    n_rows = xs.shape[0] * SUBLANES
    ff = wg.shape[2]
    tf = ff // 2
    nj = ff // tf
    nt = n_rows // TM_EXP

    def wj(i, j, na):
        return jnp.where(i < na[0], j, nj - 1)

    row = pl.BlockSpec((TM_EXP // SUBLANES,) + ROW_TILES, lambda i, j, te, na: (i, 0, 0, 0))
    grid_spec = pltpu.PrefetchScalarGridSpec(
        num_scalar_prefetch=2,
        grid=(nt, nj),
        in_specs=[row,
                  pl.BlockSpec((None, D_MODEL, tf), lambda i, j, te, na: (te[i], 0, wj(i, j, na))),
                  pl.BlockSpec((None, D_MODEL, tf), lambda i, j, te, na: (te[i], 0, wj(i, j, na))),
                  pl.BlockSpec((None, tf, D_MODEL), lambda i, j, te, na: (te[i], wj(i, j, na), 0))],
        out_specs=row,
        scratch_shapes=[pltpu.VMEM((TM_EXP, D_MODEL), F32)],
    )
    return pl.pallas_call(
        _expert_kernel,
        grid_spec=grid_spec,
        out_shape=jax.ShapeDtypeStruct((n_rows // SUBLANES,) + ROW_TILES, F32),
        compiler_params=_cparams(("arbitrary", "arbitrary")),
        name="expert_ffn",
    )(tile_expert, n_active, xs, wg, wu, wd)


def _combine_kernel(dest_ref, x_ref, g12_ref, gfin_ref, ys_ref, o_ref, buf_ref, sem, *, final):
    tm = x_ref.shape[0]

    def row_copy(g, s, k):
        return pltpu.make_async_copy(_row_of(ys_ref, dest_ref[0, 2 * (SUBLANES * g + s) + k]),
                                     buf_ref.at[k, g, :, pl.ds(s, 1), :], sem)

    def issue(g, _):
        for s in range(SUBLANES):
            row_copy(g, s, 0).start(priority=0)
            row_copy(g, s, 1).start(priority=1)
        return 0

    lax.fori_loop(0, tm // SUBLANES, issue, 0)

    def drain(g, _):
        for s in range(SUBLANES):
            row_copy(g, s, 0).wait()
            row_copy(g, s, 1).wait()
        return 0

    lax.fori_loop(0, tm // SUBLANES, drain, 0)
    g12 = g12_ref[...]
    y = x_ref[...] + g12[:, 0:1] * _from_row_tiles(buf_ref.at[0]) + g12[:, 1:2] * _from_row_tiles(buf_ref.at[1])
    o_ref[...] = _rms(y, gfin_ref[...], NORM_EPS) if final else y


def _combine(x2, ys, dest2, g12, g_final, final):
    t = x2.shape[0]
    tm = min(TM_ROW, t)
    nt = t // tm
    dest3 = dest2.reshape(nt, 1, 2 * tm)
    row = pl.BlockSpec((tm, D_MODEL), lambda i: (i, 0))
    return pl.pallas_call(
        functools.partial(_combine_kernel, final=final),
        grid=(nt,),
        in_specs=[pl.BlockSpec((None, 1, 2 * tm), lambda i: (i, 0, 0), memory_space=pltpu.SMEM),
                  row, pl.BlockSpec((tm, 2), lambda i: (i, 0)),
                  pl.BlockSpec((1, D_MODEL), lambda i: (0, 0)),
                  pl.BlockSpec(memory_space=pl.ANY)],
        out_specs=row,
        out_shape=jax.ShapeDtypeStruct((t, D_MODEL), F32),
        scratch_shapes=[pltpu.VMEM((2, tm // SUBLANES) + ROW_TILES, F32), pltpu.SemaphoreType.DMA(())],
        compiler_params=_cparams(("arbitrary",)),
        name="combine",
    )(dest3, x2, g12, g_final, ys)


def _final_norm_kernel(x_ref, g_ref, o_ref):
    o_ref[...] = _rms(x_ref[...], g_ref[...], NORM_EPS)


def _final_norm(x2, g):
    t = x2.shape[0]
    tm = min(TM_PREP, t)
    row = pl.BlockSpec((tm, D_MODEL), lambda i: (i, 0))
    return pl.pallas_call(
        _final_norm_kernel, grid=(t // tm,),
        in_specs=[row, pl.BlockSpec((1, D_MODEL), lambda i: (0, 0))], out_specs=row,
        out_shape=jax.ShapeDtypeStruct((t, D_MODEL), F32),
        compiler_params=_cparams(("parallel",)), name="final_norm",
    )(x2, g)


def _moe_layer(x2, h, h_tiles, router, wg, wu, wd, g_final, final):
    t = x2.shape[0]
    w_router_pad = jnp.zeros((D_MODEL, LANES), F32).at[:, :N_EXPERTS].set(router.astype(F32))
    gates, rank, cnt = _router(h, w_router_pad)
    counts = cnt[0, :N_EXPERTS].astype(jnp.int32)
    padded = ((counts + TM_EXP - 1) // TM_EXP) * TM_EXP
    ends = jnp.cumsum(padded)
    starts = ends - padded
    n_rows = TOP_K * t + N_EXPERTS * TM_EXP
    nt = n_rows // TM_EXP
    n_active = (ends[-1] // TM_EXP).astype(jnp.int32)
    tile_start = jnp.minimum(jnp.arange(nt, dtype=jnp.int32), n_active - 1) * TM_EXP
    tile_expert = jnp.minimum(jnp.sum(ends[None, :] <= tile_start[:, None], axis=1), N_EXPERTS - 1).astype(jnp.int32)
    rank8 = rank[:, :N_EXPERTS]
    sel8 = rank8 >= 0
    dest8 = starts[None, :] + rank8
    e_idx = jnp.arange(N_EXPERTS, dtype=jnp.int32)[None, :]
    first = jnp.min(jnp.where(sel8, e_idx, N_EXPERTS), axis=1, keepdims=True)
    second = jnp.max(jnp.where(sel8, e_idx, -1), axis=1, keepdims=True)
    pick = lambda a, e: jnp.sum(jnp.where(e_idx == e, a, 0), axis=1, keepdims=True)
    dest2 = jnp.concatenate([pick(dest8, first), pick(dest8, second)], axis=1).astype(jnp.int32)
    gates8 = gates[:, :N_EXPERTS]
    g12 = jnp.concatenate([pick(gates8, first), pick(gates8, second)], axis=1)
    xs = _scatter_rows(h_tiles, dest2, ends.astype(jnp.int32), n_rows)
    ys = _expert_ffn(xs, tile_expert, n_active.reshape(1), wg, wu, wd)
    return _combine(x2, ys, dest2, g12, g_final, final)


def kernel(x, positions, w_in, w_out, norm_mix, norm_ffn, da_lambda_q1, da_lambda_k1, da_lambda_q2,
           da_lambda_k2, da_subln, mla_q_norm, mla_w_uq, mla_kv_norm, mla_w_ukv, ffn_w_gate, ffn_w_up,
           ffn_w_down, moe_router, moe_w_gate, moe_w_up, moe_w_down, norm_final):
    batch, seq, d = x.shape
    depth = w_in.shape[0]
    t = batch * seq
    x2 = x.reshape(t, d).astype(F32)
    tabs = _rope_tables(positions)

    n_qk = 2 * DA_HEADS * 2 * DA_HEAD_DIM
    n_v = DA_HEADS * DA_V_DIM
    c0 = n_qk + n_v
    kv_per_head = MLA_NOPE_DIM + MLA_V_DIM

    for l in range(depth):
        wl = w_in[l]
        w_kr = jnp.zeros((d, LANES), F32).at[:, MLA_NOPE_DIM:MLA_QK_DIM].set(
            wl[:, c0 + MLA_Q_RANK + MLA_KV_RANK:])
        wc = jnp.concatenate([wl[:, c0:c0 + MLA_Q_RANK + MLA_KV_RANK], w_kr], axis=1)
        wuq = jnp.pad(mla_w_uq[l].reshape(MLA_Q_RANK, MLA_HEADS, MLA_QK_DIM),
                      ((0, 0), (0, 0), (0, LANES - MLA_QK_DIM))).reshape(MLA_Q_RANK, MLA_HEADS * LANES)
        wukv = mla_w_ukv[l].reshape(MLA_KV_RANK, MLA_HEADS, kv_per_head)
        wuk = jnp.pad(wukv[:, :, :MLA_NOPE_DIM],
                      ((0, 0), (0, 0), (0, LANES - MLA_NOPE_DIM))).reshape(MLA_KV_RANK, MLA_HEADS * LANES)
        wuv = wukv[:, :, MLA_NOPE_DIM:].reshape(MLA_KV_RANK, MLA_HEADS * MLA_V_DIM)
        w = dict(wqk=wl[:, :n_qk].astype(BF16), wv=wl[:, n_qk:c0].T.astype(BF16), wc=wc.astype(BF16),
                 gq=mla_q_norm[l].reshape(1, -1).astype(F32), wuq=wuq.astype(BF16),
                 gkv=mla_kv_norm[l].reshape(1, -1).astype(F32), wuk=wuk.astype(BF16), wuv=wuv.T.astype(BF16))

        da_q, da_k, da_vt, ml_q, ml_k, ml_vt = _prep(
            x2, norm_mix[l].reshape(1, d).astype(F32), w, tabs, batch, seq)

        lambda_init = 0.8 - 0.6 * float(np.exp(-0.3 * l))
        lam = (jnp.exp(jnp.sum(da_lambda_q1[l].astype(F32) * da_lambda_k1[l].astype(F32)))
               - jnp.exp(jnp.sum(da_lambda_q2[l].astype(F32) * da_lambda_k2[l].astype(F32)))
               + lambda_init).reshape(1).astype(F32)
        o_da = _attention(da_q, da_k, da_vt, batch, seq, 2, lam=lam,
                          subln=da_subln[l].reshape(DA_V_DIM, 1).astype(F32),
                          one_minus_lambda_init=1.0 - lambda_init)
        o_ml = _attention(ml_q, ml_k, ml_vt, batch, seq, 1)

        is_moe = l % 2 == 1
        wo = w_out[l].astype(BF16)
        x2, h, *h_tiles = _outproj(x2, o_da, o_ml, wo[:DA_WIDTH], wo[DA_WIDTH:],
                                   norm_ffn[l].reshape(1, d).astype(F32), F32 if is_moe else BF16, is_moe)
        i = l // 2
        last = l == depth - 1
        if not is_moe:
            x2 = _dense_ffn(x2, h, ffn_w_gate[i].astype(BF16), ffn_w_up[i].astype(BF16),
                            ffn_w_down[i].astype(BF16))
            if last:
                x2 = _final_norm(x2, norm_final.reshape(1, d).astype(F32))
        else:
            x2 = _moe_layer(x2, h, h_tiles[0], moe_router[i], moe_w_gate[i].astype(BF16),
                            moe_w_up[i].astype(BF16), moe_w_down[i].astype(BF16),
                            norm_final.reshape(1, d).astype(F32), last)
    return x2.reshape(batch, seq, d).astype(x.dtype)
```

```python
import functools
import math

import numpy as np
import jax
import jax.numpy as jnp
from jax import lax
from jax.experimental import pallas as pl
from jax.experimental.pallas import tpu as pltpu

D_MODEL = 1024
ROPE_THETA = 10000.0
NORM_EPS = 1e-6
SUBLN_EPS = 1e-5
DA_HEAD_DIM = 64
DA_V_DIM = 128
DA_HEADS = 4
DA_WIDTH = 512
MLA_HEADS = 4
MLA_V_DIM = 128
MLA_WIDTH = 512
MLA_Q_RANK = 256
MLA_KV_RANK = 128
MLA_NOPE_DIM = 64
MLA_ROPE_DIM = 32
MLA_QK_DIM = 96
N_EXPERTS = 8
TOP_K = 2
LOG2E = 1.4426950408889634

LANES = 128
SUBLANES = 8
VMEM_LIMIT = 56 * 1024 * 1024

TM_PREP = 512
PREP_SPLIT = 1
TQ = {1: 1024, 2: 512}
ATTN_CHUNK = 256
SHIFT_KEYS = 64
VALUE_GROUP = 2048
ATTN_TRIP = 16
TM_FFN = 512
TM_EXP = 512
TM_ROW = 256
TM_ROUTE = 512
ROW_DMA_UNROLL = 8

BF16 = jnp.bfloat16
F32 = jnp.float32


def _cparams(sem):
    return pltpu.CompilerParams(dimension_semantics=sem, vmem_limit_bytes=VMEM_LIMIT)


def _rms(x, g, eps):
    return x * lax.rsqrt(jnp.mean(x * x, axis=-1, keepdims=True) + eps) * g


def _rope_table_kernel(pos_ref, inv_da_ref, sgn_da_ref, inv_ml_ref, sgn_ml_ref,
                       cos_da_ref, sin_da_ref, cos_ml_ref, sin_ml_ref):
    pos = pos_ref[...]
    ang_da = pos * inv_da_ref[...]
    cos_da_ref[...] = jnp.cos(ang_da)
    sin_da_ref[...] = jnp.sin(ang_da) * sgn_da_ref[...]
    ang_ml = pos * inv_ml_ref[...]
    cos_ml_ref[...] = jnp.cos(ang_ml)
    sin_ml_ref[...] = jnp.sin(ang_ml) * sgn_ml_ref[...]


def _rope_tables(positions):
    t = positions.size
    tm = min(TM_PREP, t)
    pos = positions.reshape(t, 1).astype(F32)
    lane = np.arange(LANES)
    inv_da = 1.0 / (ROPE_THETA ** (jnp.arange(0, DA_HEAD_DIM, 2, dtype=F32) / DA_HEAD_DIM))
    inv_ml = 1.0 / (ROPE_THETA ** (jnp.arange(0, MLA_ROPE_DIM, 2, dtype=F32) / MLA_ROPE_DIM))
    half_da = DA_HEAD_DIM // 2
    half_ml = MLA_ROPE_DIM // 2
    inv_da_l = inv_da[lane % half_da][None, :]
    sgn_da = jnp.asarray(np.where(lane % DA_HEAD_DIM < half_da, -1.0, 1.0), F32)[None, :]
    in_rope = (lane >= MLA_NOPE_DIM) & (lane < MLA_QK_DIM)
    inv_ml_l = jnp.where(in_rope, inv_ml[(lane - MLA_NOPE_DIM) % half_ml], 0.0)[None, :]
    sgn_ml = jnp.asarray(
        np.where(in_rope, np.where(lane < MLA_NOPE_DIM + half_ml, -1.0, 1.0), 0.0), F32)[None, :]
    row = pl.BlockSpec((tm, LANES), lambda i: (i, 0))
    const = pl.BlockSpec((1, LANES), lambda i: (0, 0))
    out = jax.ShapeDtypeStruct((t, LANES), F32)
    return pl.pallas_call(
        _rope_table_kernel,
        grid=(t // tm,),
        in_specs=[pl.BlockSpec((tm, 1), lambda i: (i, 0)), const, const, const, const],
        out_specs=[row, row, row, row],
        out_shape=[out, out, out, out],
        compiler_params=_cparams(("parallel",)),
        name="rope_tables",
    )(pos, inv_da_l, sgn_da, inv_ml_l, sgn_ml)


def _rot_da(blk, lane):
    fwd = pltpu.roll(blk, LANES - DA_HEAD_DIM // 2, 1)
    bwd = pltpu.roll(blk, DA_HEAD_DIM // 2, 1)
    return jnp.where(lane % DA_HEAD_DIM < DA_HEAD_DIM // 2, fwd, bwd)


def _rot_ml(blk, lane):
    fwd = pltpu.roll(blk, LANES - MLA_ROPE_DIM // 2, 1)
    bwd = pltpu.roll(blk, MLA_ROPE_DIM // 2, 1)
    return jnp.where(lane < MLA_NOPE_DIM + MLA_ROPE_DIM // 2, fwd, bwd)


def _prep_kernel(x_ref, g_ref, wqk_ref, wv_ref, wc_ref, gq_ref, wuq_ref, gkv_ref, wuk_ref, wuv_ref,
                 cda_ref, sda_ref, cml_ref, sml_ref,
                 daq_ref, dak_ref, davt_ref, mlq_ref, mlk_ref, mlvt_ref):
    tm = x_ref.shape[0]
    sub = tm // PREP_SPLIT
    lane = lax.broadcasted_iota(jnp.int32, (sub, LANES), 1)
    q_scale = DA_HEAD_DIM ** -0.5 * LOG2E
    ml_scale = MLA_QK_DIM ** -0.5 * LOG2E

    for part in range(PREP_SPLIT):
        rows = slice(part * sub, (part + 1) * sub)
        hb = _rms(x_ref[rows, :], g_ref[...], NORM_EPS).astype(BF16)

        qk = jnp.dot(hb, wqk_ref[...], preferred_element_type=F32)
        cda, sda = cda_ref[rows, :], sda_ref[rows, :]
        for j in range(2 * DA_HEADS):
            blk = qk[:, j * LANES:(j + 1) * LANES]
            r = blk * cda + _rot_da(blk, lane) * sda
            if j < DA_HEADS:
                daq_ref[rows, j * LANES:(j + 1) * LANES] = (r * q_scale).astype(BF16)
            else:
                jj = j - DA_HEADS
                dak_ref[rows, jj * LANES:(jj + 1) * LANES] = r.astype(BF16)
        davt_ref[:, rows] = lax.dot_general(wv_ref[...], hb, (((1,), (1,)), ((), ())),
                                            preferred_element_type=F32).astype(BF16)

        c = jnp.dot(hb, wc_ref[...], preferred_element_type=F32)
        cml, sml = cml_ref[rows, :], sml_ref[rows, :]
        cq = _rms(c[:, :MLA_Q_RANK], gq_ref[...], NORM_EPS).astype(BF16)
        qm = jnp.dot(cq, wuq_ref[...], preferred_element_type=F32)
        for j in range(MLA_HEADS):
            blk = qm[:, j * LANES:(j + 1) * LANES]
            r = blk * cml + _rot_ml(blk, lane) * sml
            mlq_ref[rows, j * LANES:(j + 1) * LANES] = (r * ml_scale).astype(BF16)
        ckv = _rms(c[:, MLA_Q_RANK:MLA_Q_RANK + MLA_KV_RANK], gkv_ref[...], NORM_EPS).astype(BF16)
        kr = c[:, MLA_Q_RANK + MLA_KV_RANK:]
        kr = kr * cml + _rot_ml(kr, lane) * sml
        kn = jnp.dot(ckv, wuk_ref[...], preferred_element_type=F32)
        for j in range(MLA_HEADS):
            mlk_ref[rows, j * LANES:(j + 1) * LANES] = (kn[:, j * LANES:(j + 1) * LANES] + kr).astype(BF16)
        mlvt_ref[:, rows] = lax.dot_general(wuv_ref[...], ckv, (((1,), (1,)), ((), ())),
                                            preferred_element_type=F32).astype(BF16)


def _prep(x2, g_mix, w, tabs, batch, seq):
    t = x2.shape[0]
    tm = min(TM_PREP, seq)
    nps = seq // tm
    row = lambda width: pl.BlockSpec((tm, width), lambda i: (i, 0))
    full = lambda a: pl.BlockSpec(a.shape, lambda i: (0,) * a.ndim)
    vt_spec = pl.BlockSpec((None, DA_WIDTH, tm), lambda i: (i // nps, 0, i % nps))
    tok = jax.ShapeDtypeStruct((t, DA_WIDTH), BF16)
    vts = jax.ShapeDtypeStruct((batch, DA_WIDTH, seq), BF16)
    weights = [w["wqk"], w["wv"], w["wc"], w["gq"], w["wuq"], w["gkv"], w["wuk"], w["wuv"]]
    return pl.pallas_call(
        _prep_kernel,
        grid=(t // tm,),
        in_specs=[row(D_MODEL), full(g_mix)] + [full(a) for a in weights] + [row(LANES)] * 4,
        out_specs=[row(DA_WIDTH), row(DA_WIDTH), vt_spec, row(DA_WIDTH), row(DA_WIDTH), vt_spec],
        out_shape=[tok, tok, vts, tok, tok, vts],
        compiler_params=_cparams(("parallel",)),
        name="prep",
    )(x2, g_mix, *weights, *tabs)


def _attn_kernel(*refs, n_maps, one_minus_lambda_init):
    if n_maps == 2:
        lam_ref, q_ref, k_ref, vt_ref, g_ref, o_ref = refs[:6]
    else:
        q_ref, k_ref, vt_ref, o_ref = refs[:4]
    p_ref, acc_ref, l_ref, m_ref, qm_ref = refs[-5:]
    seq = k_ref.shape[0]
    tq = q_ref.shape[0]
    tkc = ATTN_CHUNK
    n_chunks = seq // tkc
    q = q_ref[...]
    if n_maps == 2:
        lane = lax.broadcasted_iota(jnp.int32, q.shape, 1)
        zero = jnp.zeros_like(q)
        qm_ref[0] = jnp.where(lane < DA_HEAD_DIM, q, zero)
        qm_ref[1] = jnp.where(lane >= DA_HEAD_DIM, q, zero)
    else:
        qm_ref[0] = q

    def key_rows(c, rows):
        return k_ref[pl.ds(pl.multiple_of(c * tkc, rows), rows), :]

    def scores(k, mp):
        return lax.dot_general(k, qm_ref[mp], (((1,), (1,)), ((), ())), preferred_element_type=F32)

    def sublane_sums(p):
        return jnp.sum(p.reshape(p.shape[0] // SUBLANES, SUBLANES, tq), axis=0)

    for mp in range(n_maps):
        m_ref[mp] = jnp.max(scores(key_rows(0, SHIFT_KEYS), mp), axis=0, keepdims=True)
    l_ref[...] = jnp.zeros(l_ref.shape, F32)

    per_trip = math.gcd(ATTN_TRIP, n_chunks)

    def trip(t, _):
        for off in range(per_trip):
            c = t * per_trip + off
            k = key_rows(c, tkc)
            for mp in range(n_maps):
                p = jnp.exp2(scores(k, mp) - m_ref[mp])
                l_ref[mp] = l_ref[mp] + sublane_sums(p)
                p_ref[mp, pl.ds(pl.multiple_of(c * tkc, tkc), tkc), :] = p.astype(BF16)
        return 0

    lax.fori_loop(0, n_chunks // per_trip, trip, 0)

    def denominator(mp):
        return jnp.sum(l_ref[mp], axis=0, keepdims=True)

    def normalized(mp):
        return acc_ref[mp] / denominator(mp)

    if n_maps == 2:
        ratio = (lam_ref[0] * denominator(0) / denominator(1)).astype(BF16)
        group = math.gcd(VALUE_GROUP, seq)
        acc = jnp.zeros(acc_ref.shape[1:], F32)
        for k0 in range(0, seq, group):
            w = p_ref[0, k0:k0 + group, :] - ratio * p_ref[1, k0:k0 + group, :]
            acc = acc + jnp.dot(vt_ref[:, k0:k0 + group], w, preferred_element_type=F32)
        acc_ref[0] = acc / denominator(0)
    else:
        acc_ref[0] = jnp.dot(vt_ref[...], p_ref[0], preferred_element_type=F32) / denominator(0)

    def non_finite(x):
        return jnp.max(jnp.where(jnp.isfinite(x), 0.0, 1.0)) > 0.0

    @pl.when(non_finite(acc_ref[0]) | non_finite(l_ref[...]))
    def _():
        m_ref[...] = jnp.full(m_ref.shape, -jnp.inf, F32)
        l_ref[...] = jnp.zeros(l_ref.shape, F32)
        acc_ref[...] = jnp.zeros(acc_ref.shape, F32)

        def chunk(c, _):
            vt = vt_ref[:, pl.ds(pl.multiple_of(c * tkc, tkc), tkc)]
            k = key_rows(c, tkc)
            for mp in range(n_maps):
                s = scores(k, mp)
                m_old = m_ref[mp]
                m_new = jnp.maximum(m_old, jnp.max(s, axis=0, keepdims=True))
                alpha = jnp.exp2(m_old - m_new)
                p = jnp.exp2(s - m_new)
                l_ref[mp] = alpha * l_ref[mp] + sublane_sums(p)
                acc_ref[mp] = alpha * acc_ref[mp] + jnp.dot(vt, p.astype(BF16), preferred_element_type=F32)
                m_ref[mp] = m_new
            return 0

        lax.fori_loop(0, n_chunks, chunk, 0)
        if n_maps == 2:
            acc_ref[0] = normalized(0) - lam_ref[0] * normalized(1)
        else:
            acc_ref[0] = normalized(0)

    o = acc_ref[0]
    if n_maps == 2:
        ms = jnp.mean(o * o, axis=0, keepdims=True)
        o = o * lax.rsqrt(ms + SUBLN_EPS) * g_ref[...] * one_minus_lambda_init
    o_ref[...] = o.T.astype(o_ref.dtype)


def _attention(q, k, vt, batch, seq, n_maps, lam=None, subln=None, one_minus_lambda_init=1.0):
    t = q.shape[0]
    tq = min(TQ[n_maps], seq)
    nq = seq // tq
    heads = q.shape[1] // LANES
    q_spec = pl.BlockSpec((tq, LANES), lambda b, h, i: (b * nq + i, h))
    k_spec = pl.BlockSpec((seq, LANES), lambda b, h, i: (b, h))
    vt_spec = pl.BlockSpec((None, DA_V_DIM, seq), lambda b, h, i: (b, h, 0))
    in_specs = [q_spec, k_spec, vt_spec]
    args = [q, k, vt]
    if n_maps == 2:
        in_specs = [pl.BlockSpec(memory_space=pltpu.SMEM)] + in_specs + [
            pl.BlockSpec((DA_V_DIM, 1), lambda b, h, i: (0, 0))]
        args = [lam] + args + [subln]
    scratch = [pltpu.VMEM((n_maps, seq, tq), BF16),
               pltpu.VMEM((n_maps, DA_V_DIM, tq), F32),
               pltpu.VMEM((n_maps, SUBLANES, tq), F32),
               pltpu.VMEM((n_maps, 1, tq), F32),
               pltpu.VMEM((n_maps, tq, LANES), BF16)]
    return pl.pallas_call(
        functools.partial(_attn_kernel, n_maps=n_maps, one_minus_lambda_init=one_minus_lambda_init),
        grid=(batch, heads, nq),
        in_specs=in_specs,
        out_specs=q_spec,
        scratch_shapes=scratch,
        out_shape=jax.ShapeDtypeStruct((t, q.shape[1]), BF16),
        compiler_params=_cparams(("parallel", "parallel", "parallel")),
        name="diff_attn" if n_maps == 2 else "mla_attn",
    )(*args)


def _to_row_tiles(ref, val):
    groups = val.shape[0] // SUBLANES
    for j in range(D_MODEL // LANES):
        ref[:, j] = val[:, j * LANES:(j + 1) * LANES].reshape(groups, SUBLANES, LANES)


def _from_row_tiles(ref):
    rows = ref.shape[0] * SUBLANES
    return jnp.concatenate([ref[:, j].reshape(rows, LANES) for j in range(D_MODEL // LANES)], axis=1)


def _row_of(ref, r):
    return ref.at[lax.shift_right_logical(r, 3), :, pl.ds(jnp.bitwise_and(r, SUBLANES - 1), 1), :]


ROW_TILES = (D_MODEL // LANES, SUBLANES, LANES)


def _outproj_kernel(x_ref, oda_ref, oml_ref, wa_ref, wb_ref, g_ref, xo_ref, h_ref, *ht_ref):
    y = x_ref[...] + jnp.dot(oda_ref[...], wa_ref[...], preferred_element_type=F32)
    y = y + jnp.dot(oml_ref[...], wb_ref[...], preferred_element_type=F32)
    xo_ref[...] = y
    h = _rms(y, g_ref[...], NORM_EPS)
    h_ref[...] = h.astype(h_ref.dtype)
    if ht_ref:
        _to_row_tiles(ht_ref[0], h)


def _outproj(x2, o_da, o_ml, w_a, w_b, g_ffn, h_dtype, row_tiles):
    t = x2.shape[0]
    tm = min(TM_PREP, t)
    row = lambda width: pl.BlockSpec((tm, width), lambda i: (i, 0))
    full = lambda a: pl.BlockSpec(a.shape, lambda i: (0,) * a.ndim)
    out_specs = [row(D_MODEL), row(D_MODEL)]
    out_shape = [jax.ShapeDtypeStruct((t, D_MODEL), F32), jax.ShapeDtypeStruct((t, D_MODEL), h_dtype)]
    if row_tiles:
        out_specs.append(pl.BlockSpec((tm // SUBLANES,) + ROW_TILES, lambda i: (i, 0, 0, 0)))
        out_shape.append(jax.ShapeDtypeStruct((t // SUBLANES,) + ROW_TILES, F32))
    return pl.pallas_call(
        _outproj_kernel,
        grid=(t // tm,),
        in_specs=[row(D_MODEL), row(DA_WIDTH), row(MLA_WIDTH), full(w_a), full(w_b), full(g_ffn)],
        out_specs=out_specs,
        out_shape=out_shape,
        compiler_params=_cparams(("parallel",)),
        name="outproj",
    )(x2, o_da, o_ml, w_a, w_b, g_ffn)


def _ffn_kernel(x_ref, h_ref, wg_ref, wu_ref, wd_ref, o_ref):
    h = h_ref[...]
    g = jnp.dot(h, wg_ref[...], preferred_element_type=F32)
    u = jnp.dot(h, wu_ref[...], preferred_element_type=F32)
    a = (g * jax.nn.sigmoid(g) * u).astype(BF16)
    o_ref[...] = x_ref[...] + jnp.dot(a, wd_ref[...], preferred_element_type=F32)


def _dense_ffn(x2, h, wg, wu, wd):
    t = x2.shape[0]
    tm = min(TM_FFN, t)
    row = pl.BlockSpec((tm, D_MODEL), lambda i: (i, 0))
    res = lambda a: pl.BlockSpec(a.shape, lambda i: (0, 0), pipeline_mode=pl.Buffered(1))
    return pl.pallas_call(
        _ffn_kernel,
        grid=(t // tm,),
        in_specs=[row, row, res(wg), res(wu), res(wd)],
        out_specs=row,
        out_shape=jax.ShapeDtypeStruct((t, D_MODEL), F32),
        compiler_params=_cparams(("parallel",)),
        name="dense_ffn",
    )(x2, h, wg, wu, wd)


def _router_kernel(h_ref, wr_ref, gates_ref, rank_ref, cnt_ref):
    tm = h_ref.shape[0]

    @pl.when(pl.program_id(0) == 0)
    def _():
        cnt_ref[...] = jnp.zeros_like(cnt_ref)

    lane = lax.broadcasted_iota(jnp.int32, (tm, LANES), 1)
    logits = jnp.dot(h_ref[...], wr_ref[...], preferred_element_type=F32, precision=lax.Precision.HIGHEST)
    neg = jnp.float32(-jnp.inf)
    logits = jnp.where(lane < N_EXPERTS, logits, neg)
    v1 = jnp.max(logits, axis=1, keepdims=True)
    i1 = jnp.min(jnp.where(logits == v1, lane, LANES), axis=1, keepdims=True)
    sel1 = lane == i1
    rest = jnp.where(sel1, neg, logits)
    v2 = jnp.max(rest, axis=1, keepdims=True)
    i2 = jnp.min(jnp.where(rest == v2, lane, LANES), axis=1, keepdims=True)
    sel2 = lane == i2
    e = jnp.exp(v2 - v1)
    g1 = 1.0 / (1.0 + e)
    g2 = e / (1.0 + e)
    gates_ref[...] = jnp.where(sel1, g1, jnp.where(sel2, g2, 0.0))
    sel = jnp.where(sel1 | sel2, 1.0, 0.0)
    r_i = lax.broadcasted_iota(jnp.int32, (tm, tm), 0)
    c_i = lax.broadcasted_iota(jnp.int32, (tm, tm), 1)
    tri = jnp.where(c_i < r_i, 1.0, 0.0).astype(BF16)
    before = jnp.dot(tri, sel.astype(BF16), preferred_element_type=F32) + cnt_ref[0:1, :]
    rank_ref[...] = jnp.where(sel > 0, before, -1.0).astype(jnp.int32)
    cnt_ref[...] = cnt_ref[...] + jnp.sum(sel, axis=0, keepdims=True)


def _router(h, w_router_pad):
    t = h.shape[0]
    tm = min(TM_ROUTE, t)
    row = lambda width: pl.BlockSpec((tm, width), lambda i: (i, 0))
    return pl.pallas_call(
        _router_kernel,
        grid=(t // tm,),
        in_specs=[row(D_MODEL), pl.BlockSpec(w_router_pad.shape, lambda i: (0, 0))],
        out_specs=[row(LANES), row(LANES), pl.BlockSpec((8, LANES), lambda i: (0, 0))],
        out_shape=[jax.ShapeDtypeStruct((t, LANES), F32), jax.ShapeDtypeStruct((t, LANES), jnp.int32),
                   jax.ShapeDtypeStruct((8, LANES), F32)],
        compiler_params=_cparams(("arbitrary",)),
        name="router",
    )(h, w_router_pad)


def _scatter_kernel(ends_ref, dest_ref, h_ref, xs_ref, zero_ref, sem):
    tm = h_ref.shape[0] * SUBLANES
    tile_groups = TM_EXP // SUBLANES

    def row_copy(g, s, k):
        return pltpu.make_async_copy(h_ref.at[g, :, pl.ds(s, 1), :],
                                     _row_of(xs_ref, dest_ref[0, 2 * (SUBLANES * g + s) + k]), sem)

    @pl.when(pl.program_id(0) == 0)
    def _():
        zero_ref[...] = jnp.zeros_like(zero_ref)
        copies = []
        for e in range(N_EXPERTS):
            start = pl.multiple_of(jnp.maximum(ends_ref[e] - TM_EXP, 0) // SUBLANES, tile_groups)
            cp = pltpu.make_async_copy(zero_ref, xs_ref.at[pl.ds(start, tile_groups)], sem)
            cp.start()
            copies.append(cp)
        for cp in copies:
            cp.wait()

        def zero_tail(i, _):
            cp = pltpu.make_async_copy(
                zero_ref, xs_ref.at[pl.ds(pl.multiple_of(i * tile_groups, tile_groups), tile_groups)], sem)
            cp.start()
            cp.wait()
            return 0

        lax.fori_loop(ends_ref[N_EXPERTS - 1] // TM_EXP, xs_ref.shape[0] // tile_groups, zero_tail, 0)

    def issue(g, _):
        for s in range(SUBLANES):
            row_copy(g, s, 0).start(priority=0)
            row_copy(g, s, 1).start(priority=1)
        return 0

    lax.fori_loop(0, tm // SUBLANES, issue, 0)

    def drain(g, _):
        for s in range(SUBLANES):
            row_copy(g, s, 0).wait()
            row_copy(g, s, 1).wait()
        return 0

    lax.fori_loop(0, tm // SUBLANES, drain, 0)


def _scatter_rows(h, dest2, ends, n_rows):
    t = h.shape[0] * SUBLANES
    tm = min(TM_ROW, t)
    nt = t // tm
    dest3 = dest2.reshape(nt, 1, 2 * tm)
    grid_spec = pltpu.PrefetchScalarGridSpec(
        num_scalar_prefetch=1,
        grid=(nt,),
        in_specs=[pl.BlockSpec((None, 1, 2 * tm), lambda i, ends: (i, 0, 0), memory_space=pltpu.SMEM),
                  pl.BlockSpec((tm // SUBLANES,) + ROW_TILES, lambda i, ends: (i, 0, 0, 0))],
        out_specs=pl.BlockSpec(memory_space=pl.ANY),
        scratch_shapes=[pltpu.VMEM((TM_EXP // SUBLANES,) + ROW_TILES, F32), pltpu.SemaphoreType.DMA(())],
    )
    return pl.pallas_call(
        _scatter_kernel,
        grid_spec=grid_spec,
        out_shape=jax.ShapeDtypeStruct((n_rows // SUBLANES,) + ROW_TILES, F32),
        compiler_params=_cparams(("arbitrary",)),
        name="scatter_rows",
    )(ends, dest3, h)


def _expert_kernel(te_ref, na_ref, xs_ref, wg_ref, wu_ref, wd_ref, ys_ref, acc_ref):
    i = pl.program_id(0)
    j = pl.program_id(1)

    @pl.when(i < na_ref[0])
    def _():
        xb = _from_row_tiles(xs_ref).astype(BF16)
        g = jnp.dot(xb, wg_ref[...], preferred_element_type=F32)
        u = jnp.dot(xb, wu_ref[...], preferred_element_type=F32)
        a = (g * jax.nn.sigmoid(g) * u).astype(BF16)
        part = jnp.dot(a, wd_ref[...], preferred_element_type=F32)

        @pl.when(j == 0)
        def _():
            acc_ref[...] = part

        @pl.when(j > 0)
        def _():
            acc_ref[...] = acc_ref[...] + part

        @pl.when(j == pl.num_programs(1) - 1)
        def _():
            _to_row_tiles(ys_ref, acc_ref[...])

    @pl.when(i >= na_ref[0])
    def _():
        ys_ref[...] = jnp.zeros_like(ys_ref)


def _expert_ffn(xs, tile_expert, n_active, wg, wu, wd):
    n_rows = xs.shape[0] * SUBLANES
    ff = wg.shape[2]
    tf = ff // 2
    nj = ff // tf
    nt = n_rows // TM_EXP

    def wj(i, j, na):
        return jnp.where(i < na[0], j, nj - 1)

    row = pl.BlockSpec((TM_EXP // SUBLANES,) + ROW_TILES, lambda i, j, te, na: (i, 0, 0, 0))
    grid_spec = pltpu.PrefetchScalarGridSpec(
        num_scalar_prefetch=2,
        grid=(nt, nj),
        in_specs=[row,
                  pl.BlockSpec((None, D_MODEL, tf), lambda i, j, te, na: (te[i], 0, wj(i, j, na))),
                  pl.BlockSpec((None, D_MODEL, tf), lambda i, j, te, na: (te[i], 0, wj(i, j, na))),
                  pl.BlockSpec((None, tf, D_MODEL), lambda i, j, te, na: (te[i], wj(i, j, na), 0))],
        out_specs=row,
        scratch_shapes=[pltpu.VMEM((TM_EXP, D_MODEL), F32)],
    )
    return pl.pallas_call(
        _expert_kernel,
        grid_spec=grid_spec,
        out_shape=jax.ShapeDtypeStruct((n_rows // SUBLANES,) + ROW_TILES, F32),
        compiler_params=_cparams(("arbitrary", "arbitrary")),
        name="expert_ffn",
    )(tile_expert, n_active, xs, wg, wu, wd)


def _combine_kernel(dest_ref, x_ref, g12_ref, gfin_ref, ys_ref, o_ref, buf_ref, sem, *, final):
    tm = x_ref.shape[0]

    def row_copy(g, s, k):
        return pltpu.make_async_copy(_row_of(ys_ref, dest_ref[0, 2 * (SUBLANES * g + s) + k]),
                                     buf_ref.at[k, g, :, pl.ds(s, 1), :], sem)

    def issue(g, _):
        for s in range(SUBLANES):
            row_copy(g, s, 0).start(priority=0)
            row_copy(g, s, 1).start(priority=1)
        return 0

    lax.fori_loop(0, tm // SUBLANES, issue, 0)

    def drain(g, _):
        for s in range(SUBLANES):
            row_copy(g, s, 0).wait()
            row_copy(g, s, 1).wait()
        return 0

    lax.fori_loop(0, tm // SUBLANES, drain, 0)
    g12 = g12_ref[...]
    y = x_ref[...] + g12[:, 0:1] * _from_row_tiles(buf_ref.at[0]) + g12[:, 1:2] * _from_row_tiles(buf_ref.at[1])
    o_ref[...] = _rms(y, gfin_ref[...], NORM_EPS) if final else y


def _combine(x2, ys, dest2, g12, g_final, final):
    t = x2.shape[0]
    tm = min(TM_ROW, t)
    nt = t // tm
    dest3 = dest2.reshape(nt, 1, 2 * tm)
    row = pl.BlockSpec((tm, D_MODEL), lambda i: (i, 0))
    return pl.pallas_call(
        functools.partial(_combine_kernel, final=final),
        grid=(nt,),
        in_specs=[pl.BlockSpec((None, 1, 2 * tm), lambda i: (i, 0, 0), memory_space=pltpu.SMEM),
                  row, pl.BlockSpec((tm, 2), lambda i: (i, 0)),
                  pl.BlockSpec((1, D_MODEL), lambda i: (0, 0)),
                  pl.BlockSpec(memory_space=pl.ANY)],
        out_specs=row,
        out_shape=jax.ShapeDtypeStruct((t, D_MODEL), F32),
        scratch_shapes=[pltpu.VMEM((2, tm // SUBLANES) + ROW_TILES, F32), pltpu.SemaphoreType.DMA(())],
        compiler_params=_cparams(("arbitrary",)),
        name="combine",
    )(dest3, x2, g12, g_final, ys)


def _final_norm_kernel(x_ref, g_ref, o_ref):
    o_ref[...] = _rms(x_ref[...], g_ref[...], NORM_EPS)


def _final_norm(x2, g):
    t = x2.shape[0]
    tm = min(TM_PREP, t)
    row = pl.BlockSpec((tm, D_MODEL), lambda i: (i, 0))
    return pl.pallas_call(
        _final_norm_kernel, grid=(t // tm,),
        in_specs=[row, pl.BlockSpec((1, D_MODEL), lambda i: (0, 0))], out_specs=row,
        out_shape=jax.ShapeDtypeStruct((t, D_MODEL), F32),
        compiler_params=_cparams(("parallel",)), name="final_norm",
    )(x2, g)


def _moe_layer(x2, h, h_tiles, router, wg, wu, wd, g_final, final):
    t = x2.shape[0]
    w_router_pad = jnp.zeros((D_MODEL, LANES), F32).at[:, :N_EXPERTS].set(router.astype(F32))
    gates, rank, cnt = _router(h, w_router_pad)
    counts = cnt[0, :N_EXPERTS].astype(jnp.int32)
    padded = ((counts + TM_EXP - 1) // TM_EXP) * TM_EXP
    ends = jnp.cumsum(padded)
    starts = ends - padded
    n_rows = TOP_K * t + N_EXPERTS * TM_EXP
    nt = n_rows // TM_EXP
    n_active = (ends[-1] // TM_EXP).astype(jnp.int32)
    tile_start = jnp.minimum(jnp.arange(nt, dtype=jnp.int32), n_active - 1) * TM_EXP
    tile_expert = jnp.minimum(jnp.sum(ends[None, :] <= tile_start[:, None], axis=1), N_EXPERTS - 1).astype(jnp.int32)
    rank8 = rank[:, :N_EXPERTS]
    sel8 = rank8 >= 0
    dest8 = starts[None, :] + rank8
    e_idx = jnp.arange(N_EXPERTS, dtype=jnp.int32)[None, :]
    first = jnp.min(jnp.where(sel8, e_idx, N_EXPERTS), axis=1, keepdims=True)
    second = jnp.max(jnp.where(sel8, e_idx, -1), axis=1, keepdims=True)
    pick = lambda a, e: jnp.sum(jnp.where(e_idx == e, a, 0), axis=1, keepdims=True)
    dest2 = jnp.concatenate([pick(dest8, first), pick(dest8, second)], axis=1).astype(jnp.int32)
    gates8 = gates[:, :N_EXPERTS]
    g12 = jnp.concatenate([pick(gates8, first), pick(gates8, second)], axis=1)
    xs = _scatter_rows(h_tiles, dest2, ends.astype(jnp.int32), n_rows)
    ys = _expert_ffn(xs, tile_expert, n_active.reshape(1), wg, wu, wd)
    return _combine(x2, ys, dest2, g12, g_final, final)


def kernel(x, positions, w_in, w_out, norm_mix, norm_ffn, da_lambda_q1, da_lambda_k1, da_lambda_q2,
           da_lambda_k2, da_subln, mla_q_norm, mla_w_uq, mla_kv_norm, mla_w_ukv, ffn_w_gate, ffn_w_up,
           ffn_w_down, moe_router, moe_w_gate, moe_w_up, moe_w_down, norm_final):
    batch, seq, d = x.shape
    depth = w_in.shape[0]
    t = batch * seq
    x2 = x.reshape(t, d).astype(F32)
    tabs = _rope_tables(positions)

    n_qk = 2 * DA_HEADS * 2 * DA_HEAD_DIM
    n_v = DA_HEADS * DA_V_DIM
    c0 = n_qk + n_v
    kv_per_head = MLA_NOPE_DIM + MLA_V_DIM

    for l in range(depth):
        wl = w_in[l]
        w_kr = jnp.zeros((d, LANES), F32).at[:, MLA_NOPE_DIM:MLA_QK_DIM].set(
            wl[:, c0 + MLA_Q_RANK + MLA_KV_RANK:])
        wc = jnp.concatenate([wl[:, c0:c0 + MLA_Q_RANK + MLA_KV_RANK], w_kr], axis=1)
        wuq = jnp.pad(mla_w_uq[l].reshape(MLA_Q_RANK, MLA_HEADS, MLA_QK_DIM),
                      ((0, 0), (0, 0), (0, LANES - MLA_QK_DIM))).reshape(MLA_Q_RANK, MLA_HEADS * LANES)
        wukv = mla_w_ukv[l].reshape(MLA_KV_RANK, MLA_HEADS, kv_per_head)
        wuk = jnp.pad(wukv[:, :, :MLA_NOPE_DIM],
                      ((0, 0), (0, 0), (0, LANES - MLA_NOPE_DIM))).reshape(MLA_KV_RANK, MLA_HEADS * LANES)
        wuv = wukv[:, :, MLA_NOPE_DIM:].reshape(MLA_KV_RANK, MLA_HEADS * MLA_V_DIM)
        w = dict(wqk=wl[:, :n_qk].astype(BF16), wv=wl[:, n_qk:c0].T.astype(BF16), wc=wc.astype(BF16),
                 gq=mla_q_norm[l].reshape(1, -1).astype(F32), wuq=wuq.astype(BF16),
                 gkv=mla_kv_norm[l].reshape(1, -1).astype(F32), wuk=wuk.astype(BF16), wuv=wuv.T.astype(BF16))

        da_q, da_k, da_vt, ml_q, ml_k, ml_vt = _prep(
            x2, norm_mix[l].reshape(1, d).astype(F32), w, tabs, batch, seq)

        lambda_init = 0.8 - 0.6 * float(np.exp(-0.3 * l))
        lam = (jnp.exp(jnp.sum(da_lambda_q1[l].astype(F32) * da_lambda_k1[l].astype(F32)))
               - jnp.exp(jnp.sum(da_lambda_q2[l].astype(F32) * da_lambda_k2[l].astype(F32)))
               + lambda_init).reshape(1).astype(F32)
        o_da = _attention(da_q, da_k, da_vt, batch, seq, 2, lam=lam,
                          subln=da_subln[l].reshape(DA_V_DIM, 1).astype(F32),
                          one_minus_lambda_init=1.0 - lambda_init)
        o_ml = _attention(ml_q, ml_k, ml_vt, batch, seq, 1)

        is_moe = l % 2 == 1
        wo = w_out[l].astype(BF16)
        x2, h, *h_tiles = _outproj(x2, o_da, o_ml, wo[:DA_WIDTH], wo[DA_WIDTH:],
                                   norm_ffn[l].reshape(1, d).astype(F32), F32 if is_moe else BF16, is_moe)
        i = l // 2
        last = l == depth - 1
        if not is_moe:
            x2 = _dense_ffn(x2, h, ffn_w_gate[i].astype(BF16), ffn_w_up[i].astype(BF16),
                            ffn_w_down[i].astype(BF16))
            if last:
                x2 = _final_norm(x2, norm_final.reshape(1, d).astype(F32))
        else:
            x2 = _moe_layer(x2, h, h_tiles[0], moe_router[i], moe_w_gate[i].astype(BF16),
                            moe_w_up[i].astype(BF16), moe_w_down[i].astype(BF16),
                            norm_final.reshape(1, d).astype(F32), last)
    return x2.reshape(batch, seq, d).astype(x.dtype)
```

```python
import functools
import math

import numpy as np
import jax
import jax.numpy as jnp
from jax import lax
from jax.experimental import pallas as pl
from jax.experimental.pallas import tpu as pltpu

D_MODEL = 1024
ROPE_THETA = 10000.0
NORM_EPS = 1e-6
SUBLN_EPS = 1e-5
DA_HEAD_DIM = 64
DA_V_DIM = 128
DA_HEADS = 4
DA_WIDTH = 512
MLA_HEADS = 4
MLA_V_DIM = 128
MLA_WIDTH = 512
MLA_Q_RANK = 256
MLA_KV_RANK = 128
MLA_NOPE_DIM = 64
MLA_ROPE_DIM = 32
MLA_QK_DIM = 96
N_EXPERTS = 8
TOP_K = 2
LOG2E = 1.4426950408889634

LANES = 128
SUBLANES = 8
VMEM_LIMIT = 56 * 1024 * 1024

TM_PREP = 512
PREP_SPLIT = 1
TQ = {1: 1024, 2: 1024}
ATTN_CHUNK = 256
SHIFT_KEYS = 64
VALUE_GROUP = 2048
ATTN_TRIP = 16
TM_FFN = 512
TM_EXP = 512
TM_ROW = 256
TM_ROUTE = 512
ROW_DMA_UNROLL = 8

BF16 = jnp.bfloat16
F32 = jnp.float32


def _cparams(sem):
    return pltpu.CompilerParams(dimension_semantics=sem, vmem_limit_bytes=VMEM_LIMIT)


def _rms(x, g, eps):
    return x * lax.rsqrt(jnp.mean(x * x, axis=-1, keepdims=True) + eps) * g


def _rope_table_kernel(pos_ref, inv_da_ref, sgn_da_ref, inv_ml_ref, sgn_ml_ref,
                       cos_da_ref, sin_da_ref, cos_ml_ref, sin_ml_ref):
    pos = pos_ref[...]
    ang_da = pos * inv_da_ref[...]
    cos_da_ref[...] = jnp.cos(ang_da)
    sin_da_ref[...] = jnp.sin(ang_da) * sgn_da_ref[...]
    ang_ml = pos * inv_ml_ref[...]
    cos_ml_ref[...] = jnp.cos(ang_ml)
    sin_ml_ref[...] = jnp.sin(ang_ml) * sgn_ml_ref[...]


def _rope_tables(positions):
    t = positions.size
    tm = min(TM_PREP, t)
    pos = positions.reshape(t, 1).astype(F32)
    lane = np.arange(LANES)
    inv_da = 1.0 / (ROPE_THETA ** (jnp.arange(0, DA_HEAD_DIM, 2, dtype=F32) / DA_HEAD_DIM))
    inv_ml = 1.0 / (ROPE_THETA ** (jnp.arange(0, MLA_ROPE_DIM, 2, dtype=F32) / MLA_ROPE_DIM))
    half_da = DA_HEAD_DIM // 2
    half_ml = MLA_ROPE_DIM // 2
    inv_da_l = inv_da[lane % half_da][None, :]
    sgn_da = jnp.asarray(np.where(lane % DA_HEAD_DIM < half_da, -1.0, 1.0), F32)[None, :]
    in_rope = (lane >= MLA_NOPE_DIM) & (lane < MLA_QK_DIM)
    inv_ml_l = jnp.where(in_rope, inv_ml[(lane - MLA_NOPE_DIM) % half_ml], 0.0)[None, :]
    sgn_ml = jnp.asarray(
        np.where(in_rope, np.where(lane < MLA_NOPE_DIM + half_ml, -1.0, 1.0), 0.0), F32)[None, :]
    row = pl.BlockSpec((tm, LANES), lambda i: (i, 0))
    const = pl.BlockSpec((1, LANES), lambda i: (0, 0))
    out = jax.ShapeDtypeStruct((t, LANES), F32)
    return pl.pallas_call(
        _rope_table_kernel,
        grid=(t // tm,),
        in_specs=[pl.BlockSpec((tm, 1), lambda i: (i, 0)), const, const, const, const],
        out_specs=[row, row, row, row],
        out_shape=[out, out, out, out],
        compiler_params=_cparams(("parallel",)),
        name="rope_tables",
    )(pos, inv_da_l, sgn_da, inv_ml_l, sgn_ml)


def _rot_da(blk, lane):
    fwd = pltpu.roll(blk, LANES - DA_HEAD_DIM // 2, 1)
    bwd = pltpu.roll(blk, DA_HEAD_DIM // 2, 1)
    return jnp.where(lane % DA_HEAD_DIM < DA_HEAD_DIM // 2, fwd, bwd)


def _rot_ml(blk, lane):
    fwd = pltpu.roll(blk, LANES - MLA_ROPE_DIM // 2, 1)
    bwd = pltpu.roll(blk, MLA_ROPE_DIM // 2, 1)
    return jnp.where(lane < MLA_NOPE_DIM + MLA_ROPE_DIM // 2, fwd, bwd)


def _prep_kernel(x_ref, g_ref, wqk_ref, wv_ref, wc_ref, gq_ref, wuq_ref, gkv_ref, wuk_ref, wuv_ref,
                 cda_ref, sda_ref, cml_ref, sml_ref,
                 daq_ref, dak_ref, davt_ref, mlq_ref, mlk_ref, mlvt_ref):
    tm = x_ref.shape[0]
    sub = tm // PREP_SPLIT
    lane = lax.broadcasted_iota(jnp.int32, (sub, LANES), 1)
    q_scale = DA_HEAD_DIM ** -0.5 * LOG2E
    ml_scale = MLA_QK_DIM ** -0.5 * LOG2E

    for part in range(PREP_SPLIT):
        rows = slice(part * sub, (part + 1) * sub)
        hb = _rms(x_ref[rows, :], g_ref[...], NORM_EPS).astype(BF16)

        qk = jnp.dot(hb, wqk_ref[...], preferred_element_type=F32)
        cda, sda = cda_ref[rows, :], sda_ref[rows, :]
        for j in range(2 * DA_HEADS):
            blk = qk[:, j * LANES:(j + 1) * LANES]
            r = blk * cda + _rot_da(blk, lane) * sda
            if j < DA_HEADS:
                daq_ref[rows, j * LANES:(j + 1) * LANES] = (r * q_scale).astype(BF16)
            else:
                jj = j - DA_HEADS
                dak_ref[rows, jj * LANES:(jj + 1) * LANES] = r.astype(BF16)
        davt_ref[:, rows] = lax.dot_general(wv_ref[...], hb, (((1,), (1,)), ((), ())),
                                            preferred_element_type=F32).astype(BF16)

        c = jnp.dot(hb, wc_ref[...], preferred_element_type=F32)
        cml, sml = cml_ref[rows, :], sml_ref[rows, :]
        cq = _rms(c[:, :MLA_Q_RANK], gq_ref[...], NORM_EPS).astype(BF16)
        qm = jnp.dot(cq, wuq_ref[...], preferred_element_type=F32)
        for j in range(MLA_HEADS):
            blk = qm[:, j * LANES:(j + 1) * LANES]
            r = blk * cml + _rot_ml(blk, lane) * sml
            mlq_ref[rows, j * LANES:(j + 1) * LANES] = (r * ml_scale).astype(BF16)
        ckv = _rms(c[:, MLA_Q_RANK:MLA_Q_RANK + MLA_KV_RANK], gkv_ref[...], NORM_EPS).astype(BF16)
        kr = c[:, MLA_Q_RANK + MLA_KV_RANK:]
        kr = kr * cml + _rot_ml(kr, lane) * sml
        kn = jnp.dot(ckv, wuk_ref[...], preferred_element_type=F32)
        for j in range(MLA_HEADS):
            mlk_ref[rows, j * LANES:(j + 1) * LANES] = (kn[:, j * LANES:(j + 1) * LANES] + kr).astype(BF16)
        mlvt_ref[:, rows] = lax.dot_general(wuv_ref[...], ckv, (((1,), (1,)), ((), ())),
                                            preferred_element_type=F32).astype(BF16)


def _prep(x2, g_mix, w, tabs, batch, seq):
    t = x2.shape[0]
    tm = min(TM_PREP, seq)
    nps = seq // tm
    row = lambda width: pl.BlockSpec((tm, width), lambda i: (i, 0))
    full = lambda a: pl.BlockSpec(a.shape, lambda i: (0,) * a.ndim)
    vt_spec = pl.BlockSpec((None, DA_WIDTH, tm), lambda i: (i // nps, 0, i % nps))
    tok = jax.ShapeDtypeStruct((t, DA_WIDTH), BF16)
    vts = jax.ShapeDtypeStruct((batch, DA_WIDTH, seq), BF16)
    weights = [w["wqk"], w["wv"], w["wc"], w["gq"], w["wuq"], w["gkv"], w["wuk"], w["wuv"]]
    return pl.pallas_call(
        _prep_kernel,
        grid=(t // tm,),
        in_specs=[row(D_MODEL), full(g_mix)] + [full(a) for a in weights] + [row(LANES)] * 4,
        out_specs=[row(DA_WIDTH), row(DA_WIDTH), vt_spec, row(DA_WIDTH), row(DA_WIDTH), vt_spec],
        out_shape=[tok, tok, vts, tok, tok, vts],
        compiler_params=_cparams(("parallel",)),
        name="prep",
    )(x2, g_mix, *weights, *tabs)


def _attn_kernel(*refs, n_maps, one_minus_lambda_init):
    if n_maps == 2:
        lam_ref, q_ref, k_ref, vt_ref, g_ref, o_ref = refs[:6]
    else:
        q_ref, k_ref, vt_ref, o_ref = refs[:4]
    p_ref, acc_ref, l_ref, m_ref, qm_ref = refs[-5:]
    seq = k_ref.shape[0]
    tq = q_ref.shape[0]
    tkc = ATTN_CHUNK
    n_chunks = seq // tkc
    q = q_ref[...]
    if n_maps == 2:
        lane = lax.broadcasted_iota(jnp.int32, q.shape, 1)
        zero = jnp.zeros_like(q)
        qm_ref[0] = jnp.where(lane < DA_HEAD_DIM, q, zero)
        qm_ref[1] = jnp.where(lane >= DA_HEAD_DIM, q, zero)
    else:
        qm_ref[0] = q

    def key_rows(c, rows):
        return k_ref[pl.ds(pl.multiple_of(c * tkc, rows), rows), :]

    def scores(k, mp):
        return lax.dot_general(k, qm_ref[mp], (((1,), (1,)), ((), ())), preferred_element_type=F32)

    def sublane_sums(p):
        return jnp.sum(p.reshape(p.shape[0] // SUBLANES, SUBLANES, tq), axis=0)

    for mp in range(n_maps):
        m_ref[mp] = jnp.max(scores(key_rows(0, SHIFT_KEYS), mp), axis=0, keepdims=True)
    l_ref[...] = jnp.zeros(l_ref.shape, F32)

    per_trip = math.gcd(ATTN_TRIP, n_chunks)

    def trip(t, _):
        for off in range(per_trip):
            c = t * per_trip + off
            k = key_rows(c, tkc)
            for mp in range(n_maps):
                p = jnp.exp2(scores(k, mp) - m_ref[mp])
                l_ref[mp] = l_ref[mp] + sublane_sums(p)
                p_ref[mp, pl.ds(pl.multiple_of(c * tkc, tkc), tkc), :] = p.astype(BF16)
        return 0

    lax.fori_loop(0, n_chunks // per_trip, trip, 0)

    def denominator(mp):
        return jnp.sum(l_ref[mp], axis=0, keepdims=True)

    def normalized(mp):
        return acc_ref[mp] / denominator(mp)

    if n_maps == 2:
        ratio = (lam_ref[0] * denominator(0) / denominator(1)).astype(BF16)
        group = math.gcd(VALUE_GROUP, seq)
        acc = jnp.zeros(acc_ref.shape[1:], F32)
        for k0 in range(0, seq, group):
            w = p_ref[0, k0:k0 + group, :] - ratio * p_ref[1, k0:k0 + group, :]
            acc = acc + jnp.dot(vt_ref[:, k0:k0 + group], w, preferred_element_type=F32)
        acc_ref[0] = acc / denominator(0)
    else:
        acc_ref[0] = jnp.dot(vt_ref[...], p_ref[0], preferred_element_type=F32) / denominator(0)

    def non_finite(x):
        return jnp.max(jnp.where(jnp.isfinite(x), 0.0, 1.0)) > 0.0

    @pl.when(non_finite(acc_ref[0]) | non_finite(l_ref[...]))
    def _():
        m_ref[...] = jnp.full(m_ref.shape, -jnp.inf, F32)
        l_ref[...] = jnp.zeros(l_ref.shape, F32)
        acc_ref[...] = jnp.zeros(acc_ref.shape, F32)

        def chunk(c, _):
            vt = vt_ref[:, pl.ds(pl.multiple_of(c * tkc, tkc), tkc)]
            k = key_rows(c, tkc)
            for mp in range(n_maps):
                s = scores(k, mp)
                m_old = m_ref[mp]
                m_new = jnp.maximum(m_old, jnp.max(s, axis=0, keepdims=True))
                alpha = jnp.exp2(m_old - m_new)
                p = jnp.exp2(s - m_new)
                l_ref[mp] = alpha * l_ref[mp] + sublane_sums(p)
                acc_ref[mp] = alpha * acc_ref[mp] + jnp.dot(vt, p.astype(BF16), preferred_element_type=F32)
                m_ref[mp] = m_new
            return 0

        lax.fori_loop(0, n_chunks, chunk, 0)
        if n_maps == 2:
            acc_ref[0] = normalized(0) - lam_ref[0] * normalized(1)
        else:
            acc_ref[0] = normalized(0)

    o = acc_ref[0]
    if n_maps == 2:
        ms = jnp.mean(o * o, axis=0, keepdims=True)
        o = o * lax.rsqrt(ms + SUBLN_EPS) * g_ref[...] * one_minus_lambda_init
    o_ref[...] = o.T.astype(o_ref.dtype)


def _attention(q, k, vt, batch, seq, n_maps, lam=None, subln=None, one_minus_lambda_init=1.0):
    t = q.shape[0]
    tq = min(TQ[n_maps], seq)
    nq = seq // tq
    heads = q.shape[1] // LANES
    q_spec = pl.BlockSpec((tq, LANES), lambda b, h, i: (b * nq + i, h))
    k_spec = pl.BlockSpec((seq, LANES), lambda b, h, i: (b, h))
    vt_spec = pl.BlockSpec((None, DA_V_DIM, seq), lambda b, h, i: (b, h, 0))
    in_specs = [q_spec, k_spec, vt_spec]
    args = [q, k, vt]
    if n_maps == 2:
        in_specs = [pl.BlockSpec(memory_space=pltpu.SMEM)] + in_specs + [
            pl.BlockSpec((DA_V_DIM, 1), lambda b, h, i: (0, 0))]
        args = [lam] + args + [subln]
    scratch = [pltpu.VMEM((n_maps, seq, tq), BF16),
               pltpu.VMEM((n_maps, DA_V_DIM, tq), F32),
               pltpu.VMEM((n_maps, SUBLANES, tq), F32),
               pltpu.VMEM((n_maps, 1, tq), F32),
               pltpu.VMEM((n_maps, tq, LANES), BF16)]
    return pl.pallas_call(
        functools.partial(_attn_kernel, n_maps=n_maps, one_minus_lambda_init=one_minus_lambda_init),
        grid=(batch, heads, nq),
        in_specs=in_specs,
        out_specs=q_spec,
        scratch_shapes=scratch,
        out_shape=jax.ShapeDtypeStruct((t, q.shape[1]), BF16),
        compiler_params=_cparams(("parallel", "parallel", "parallel")),
        name="diff_attn" if n_maps == 2 else "mla_attn",
    )(*args)


def _to_row_tiles(ref, val):
    groups = val.shape[0] // SUBLANES
    for j in range(D_MODEL // LANES):
        ref[:, j] = val[:, j * LANES:(j + 1) * LANES].reshape(groups, SUBLANES, LANES)


def _from_row_tiles(ref):
    rows = ref.shape[0] * SUBLANES
    return jnp.concatenate([ref[:, j].reshape(rows, LANES) for j in range(D_MODEL // LANES)], axis=1)


def _row_of(ref, r):
    return ref.at[lax.shift_right_logical(r, 3), :, pl.ds(jnp.bitwise_and(r, SUBLANES - 1), 1), :]


ROW_TILES = (D_MODEL // LANES, SUBLANES, LANES)


def _outproj_kernel(x_ref, oda_ref, oml_ref, wa_ref, wb_ref, g_ref, xo_ref, h_ref, *ht_ref):
    y = x_ref[...] + jnp.dot(oda_ref[...], wa_ref[...], preferred_element_type=F32)
    y = y + jnp.dot(oml_ref[...], wb_ref[...], preferred_element_type=F32)
    xo_ref[...] = y
    h = _rms(y, g_ref[...], NORM_EPS)
    h_ref[...] = h.astype(h_ref.dtype)
    if ht_ref:
        _to_row_tiles(ht_ref[0], h)


def _outproj(x2, o_da, o_ml, w_a, w_b, g_ffn, h_dtype, row_tiles):
    t = x2.shape[0]
    tm = min(TM_PREP, t)
    row = lambda width: pl.BlockSpec((tm, width), lambda i: (i, 0))
    full = lambda a: pl.BlockSpec(a.shape, lambda i: (0,) * a.ndim)
    out_specs = [row(D_MODEL), row(D_MODEL)]
    out_shape = [jax.ShapeDtypeStruct((t, D_MODEL), F32), jax.ShapeDtypeStruct((t, D_MODEL), h_dtype)]
    if row_tiles:
        out_specs.append(pl.BlockSpec((tm // SUBLANES,) + ROW_TILES, lambda i: (i, 0, 0, 0)))
        out_shape.append(jax.ShapeDtypeStruct((t // SUBLANES,) + ROW_TILES, F32))
    return pl.pallas_call(
        _outproj_kernel,
        grid=(t // tm,),
        in_specs=[row(D_MODEL), row(DA_WIDTH), row(MLA_WIDTH), full(w_a), full(w_b), full(g_ffn)],
        out_specs=out_specs,
        out_shape=out_shape,
        compiler_params=_cparams(("parallel",)),
        name="outproj",
    )(x2, o_da, o_ml, w_a, w_b, g_ffn)


def _ffn_kernel(x_ref, h_ref, wg_ref, wu_ref, wd_ref, o_ref):
    h = h_ref[...]
    g = jnp.dot(h, wg_ref[...], preferred_element_type=F32)
    u = jnp.dot(h, wu_ref[...], preferred_element_type=F32)
    a = (g * jax.nn.sigmoid(g) * u).astype(BF16)
    o_ref[...] = x_ref[...] + jnp.dot(a, wd_ref[...], preferred_element_type=F32)


def _dense_ffn(x2, h, wg, wu, wd):
    t = x2.shape[0]
    tm = min(TM_FFN, t)
    row = pl.BlockSpec((tm, D_MODEL), lambda i: (i, 0))
    res = lambda a: pl.BlockSpec(a.shape, lambda i: (0, 0), pipeline_mode=pl.Buffered(1))
    return pl.pallas_call(
        _ffn_kernel,
        grid=(t // tm,),
        in_specs=[row, row, res(wg), res(wu), res(wd)],
        out_specs=row,
        out_shape=jax.ShapeDtypeStruct((t, D_MODEL), F32),
        compiler_params=_cparams(("parallel",)),
        name="dense_ffn",
    )(x2, h, wg, wu, wd)


def _router_kernel(h_ref, wr_ref, gates_ref, rank_ref, cnt_ref):
    tm = h_ref.shape[0]

    @pl.when(pl.program_id(0) == 0)
    def _():
        cnt_ref[...] = jnp.zeros_like(cnt_ref)

    lane = lax.broadcasted_iota(jnp.int32, (tm, LANES), 1)
    logits = jnp.dot(h_ref[...], wr_ref[...], preferred_element_type=F32, precision=lax.Precision.HIGHEST)
    neg = jnp.float32(-jnp.inf)
    logits = jnp.where(lane < N_EXPERTS, logits, neg)
    v1 = jnp.max(logits, axis=1, keepdims=True)
    i1 = jnp.min(jnp.where(logits == v1, lane, LANES), axis=1, keepdims=True)
    sel1 = lane == i1
    rest = jnp.where(sel1, neg, logits)
    v2 = jnp.max(rest, axis=1, keepdims=True)
    i2 = jnp.min(jnp.where(rest == v2, lane, LANES), axis=1, keepdims=True)
    sel2 = lane == i2
    e = jnp.exp(v2 - v1)
    g1 = 1.0 / (1.0 + e)
    g2 = e / (1.0 + e)
    gates_ref[...] = jnp.where(sel1, g1, jnp.where(sel2, g2, 0.0))
    sel = jnp.where(sel1 | sel2, 1.0, 0.0)
    r_i = lax.broadcasted_iota(jnp.int32, (tm, tm), 0)
    c_i = lax.broadcasted_iota(jnp.int32, (tm, tm), 1)
    tri = jnp.where(c_i < r_i, 1.0, 0.0).astype(BF16)
    before = jnp.dot(tri, sel.astype(BF16), preferred_element_type=F32) + cnt_ref[0:1, :]
    rank_ref[...] = jnp.where(sel > 0, before, -1.0).astype(jnp.int32)
    cnt_ref[...] = cnt_ref[...] + jnp.sum(sel, axis=0, keepdims=True)


def _router(h, w_router_pad):
    t = h.shape[0]
    tm = min(TM_ROUTE, t)
    row = lambda width: pl.BlockSpec((tm, width), lambda i: (i, 0))
    return pl.pallas_call(
        _router_kernel,
        grid=(t // tm,),
        in_specs=[row(D_MODEL), pl.BlockSpec(w_router_pad.shape, lambda i: (0, 0))],
        out_specs=[row(LANES), row(LANES), pl.BlockSpec((8, LANES), lambda i: (0, 0))],
        out_shape=[jax.ShapeDtypeStruct((t, LANES), F32), jax.ShapeDtypeStruct((t, LANES), jnp.int32),
                   jax.ShapeDtypeStruct((8, LANES), F32)],
        compiler_params=_cparams(("arbitrary",)),
        name="router",
    )(h, w_router_pad)


def _scatter_kernel(ends_ref, dest_ref, h_ref, xs_ref, zero_ref, sem):
    tm = h_ref.shape[0] * SUBLANES
    tile_groups = TM_EXP // SUBLANES

    def row_copy(g, s, k):
        return pltpu.make_async_copy(h_ref.at[g, :, pl.ds(s, 1), :],
                                     _row_of(xs_ref, dest_ref[0, 2 * (SUBLANES * g + s) + k]), sem)

    @pl.when(pl.program_id(0) == 0)
    def _():
        zero_ref[...] = jnp.zeros_like(zero_ref)
        copies = []
        for e in range(N_EXPERTS):
            start = pl.multiple_of(jnp.maximum(ends_ref[e] - TM_EXP, 0) // SUBLANES, tile_groups)
            cp = pltpu.make_async_copy(zero_ref, xs_ref.at[pl.ds(start, tile_groups)], sem)
            cp.start()
            copies.append(cp)
        for cp in copies:
            cp.wait()

        def zero_tail(i, _):
            cp = pltpu.make_async_copy(
                zero_ref, xs_ref.at[pl.ds(pl.multiple_of(i * tile_groups, tile_groups), tile_groups)], sem)
            cp.start()
            cp.wait()
            return 0

        lax.fori_loop(ends_ref[N_EXPERTS - 1] // TM_EXP, xs_ref.shape[0] // tile_groups, zero_tail, 0)

    def issue(g, _):
        for s in range(SUBLANES):
            row_copy(g, s, 0).start(priority=0)
            row_copy(g, s, 1).start(priority=1)
        return 0

    lax.fori_loop(0, tm // SUBLANES, issue, 0)

    def drain(g, _):
        for s in range(SUBLANES):
            row_copy(g, s, 0).wait()
            row_copy(g, s, 1).wait()
        return 0

    lax.fori_loop(0, tm // SUBLANES, drain, 0)


def _scatter_rows(h, dest2, ends, n_rows):
    t = h.shape[0] * SUBLANES
    tm = min(TM_ROW, t)
    nt = t // tm
    dest3 = dest2.reshape(nt, 1, 2 * tm)
    grid_spec = pltpu.PrefetchScalarGridSpec(
        num_scalar_prefetch=1,
        grid=(nt,),
        in_specs=[pl.BlockSpec((None, 1, 2 * tm), lambda i, ends: (i, 0, 0), memory_space=pltpu.SMEM),
                  pl.BlockSpec((tm // SUBLANES,) + ROW_TILES, lambda i, ends: (i, 0, 0, 0))],
        out_specs=pl.BlockSpec(memory_space=pl.ANY),
        scratch_shapes=[pltpu.VMEM((TM_EXP // SUBLANES,) + ROW_TILES, F32), pltpu.SemaphoreType.DMA(())],
    )
    return pl.pallas_call(
        _scatter_kernel,
        grid_spec=grid_spec,
        out_shape=jax.ShapeDtypeStruct((n_rows // SUBLANES,) + ROW_TILES, F32),
        compiler_params=_cparams(("arbitrary",)),
        name="scatter_rows",
    )(ends, dest3, h)


def _expert_kernel(te_ref, na_ref, xs_ref, wg_ref, wu_ref, wd_ref, ys_ref, acc_ref):
    i = pl.program_id(0)
    j = pl.program_id(1)

    @pl.when((i == 0) & (j == 0))
    def _():
        acc_ref[...] = jnp.zeros_like(acc_ref)

    @pl.when(i < na_ref[0])
    def _():
        xb = _from_row_tiles(xs_ref).astype(BF16)
        g = jnp.dot(xb, wg_ref[...], preferred_element_type=F32)
        u = jnp.dot(xb, wu_ref[...], preferred_element_type=F32)
        a = (g * jax.nn.sigmoid(g) * u).astype(BF16)
        acc = jnp.where(j > 0, acc_ref[...], 0.0) + jnp.dot(a, wd_ref[...], preferred_element_type=F32)
        acc_ref[...] = acc
        _to_row_tiles(ys_ref, acc)

    @pl.when(i >= na_ref[0])
    def _():
        ys_ref[...] = jnp.zeros_like(ys_ref)


def _expert_ffn(xs, tile_expert, n_active, wg, wu, wd):
    n_rows = xs.shape[0] * SUBLANES
    ff = wg.shape[2]
    tf = ff // 2
    nj = ff // tf
    nt = n_rows // TM_EXP

    def wj(i, j, na):
        return jnp.where(i < na[0], j, nj - 1)

    row = pl.BlockSpec((TM_EXP // SUBLANES,) + ROW_TILES, lambda i, j, te, na: (i, 0, 0, 0))
    grid_spec = pltpu.PrefetchScalarGridSpec(
        num_scalar_prefetch=2,
        grid=(nt, nj),
        in_specs=[row,
                  pl.BlockSpec((None, D_MODEL, tf), lambda i, j, te, na: (te[i], 0, wj(i, j, na))),
                  pl.BlockSpec((None, D_MODEL, tf), lambda i, j, te, na: (te[i], 0, wj(i, j, na))),
                  pl.BlockSpec((None, tf, D_MODEL), lambda i, j, te, na: (te[i], wj(i, j, na), 0))],
        out_specs=row,
        scratch_shapes=[pltpu.VMEM((TM_EXP, D_MODEL), F32)],
    )
    return pl.pallas_call(
        _expert_kernel,
        grid_spec=grid_spec,
        out_shape=jax.ShapeDtypeStruct((n_rows // SUBLANES,) + ROW_TILES, F32),
        compiler_params=_cparams(("arbitrary", "arbitrary")),
        name="expert_ffn",
    )(tile_expert, n_active, xs, wg, wu, wd)


def _combine_kernel(dest_ref, x_ref, g12_ref, gfin_ref, ys_ref, o_ref, buf_ref, sem, *, final):
    tm = x_ref.shape[0]

    def row_copy(g, s, k):
        return pltpu.make_async_copy(_row_of(ys_ref, dest_ref[0, 2 * (SUBLANES * g + s) + k]),
                                     buf_ref.at[k, g, :, pl.ds(s, 1), :], sem)

    def issue(g, _):
        for s in range(SUBLANES):
            row_copy(g, s, 0).start(priority=0)
            row_copy(g, s, 1).start(priority=1)
        return 0

    lax.fori_loop(0, tm // SUBLANES, issue, 0)

    def drain(g, _):
        for s in range(SUBLANES):
            row_copy(g, s, 0).wait()
            row_copy(g, s, 1).wait()
        return 0

    lax.fori_loop(0, tm // SUBLANES, drain, 0)
    g12 = g12_ref[...]
    y = x_ref[...] + g12[:, 0:1] * _from_row_tiles(buf_ref.at[0]) + g12[:, 1:2] * _from_row_tiles(buf_ref.at[1])
    o_ref[...] = _rms(y, gfin_ref[...], NORM_EPS) if final else y


def _combine(x2, ys, dest2, g12, g_final, final):
    t = x2.shape[0]
    tm = min(TM_ROW, t)
    nt = t // tm
    dest3 = dest2.reshape(nt, 1, 2 * tm)
    row = pl.BlockSpec((tm, D_MODEL), lambda i: (i, 0))
    return pl.pallas_call(
        functools.partial(_combine_kernel, final=final),
        grid=(nt,),
        in_specs=[pl.BlockSpec((None, 1, 2 * tm), lambda i: (i, 0, 0), memory_space=pltpu.SMEM),
                  row, pl.BlockSpec((tm, 2), lambda i: (i, 0)),
                  pl.BlockSpec((1, D_MODEL), lambda i: (0, 0)),
                  pl.BlockSpec(memory_space=pl.ANY)],
        out_specs=row,
        out_shape=jax.ShapeDtypeStruct((t, D_MODEL), F32),
        scratch_shapes=[pltpu.VMEM((2, tm // SUBLANES) + ROW_TILES, F32), pltpu.SemaphoreType.DMA(())],
        compiler_params=_cparams(("arbitrary",)),
        name="combine",
    )(dest3, x2, g12, g_final, ys)


def _final_norm_kernel(x_ref, g_ref, o_ref):
    o_ref[...] = _rms(x_ref[...], g_ref[...], NORM_EPS)


def _final_norm(x2, g):
    t = x2.shape[0]
    tm = min(TM_PREP, t)
    row = pl.BlockSpec((tm, D_MODEL), lambda i: (i, 0))
    return pl.pallas_call(
        _final_norm_kernel, grid=(t // tm,),
        in_specs=[row, pl.BlockSpec((1, D_MODEL), lambda i: (0, 0))], out_specs=row,
        out_shape=jax.ShapeDtypeStruct((t, D_MODEL), F32),
        compiler_params=_cparams(("parallel",)), name="final_norm",
    )(x2, g)


def _moe_layer(x2, h, h_tiles, router, wg, wu, wd, g_final, final):
    t = x2.shape[0]
    w_router_pad = jnp.zeros((D_MODEL, LANES), F32).at[:, :N_EXPERTS].set(router.astype(F32))
    gates, rank, cnt = _router(h, w_router_pad)
    counts = cnt[0, :N_EXPERTS].astype(jnp.int32)
    padded = ((counts + TM_EXP - 1) // TM_EXP) * TM_EXP
    ends = jnp.cumsum(padded)
    starts = ends - padded
    n_rows = TOP_K * t + N_EXPERTS * TM_EXP
    nt = n_rows // TM_EXP
    n_active = (ends[-1] // TM_EXP).astype(jnp.int32)
    tile_start = jnp.minimum(jnp.arange(nt, dtype=jnp.int32), n_active - 1) * TM_EXP
    tile_expert = jnp.minimum(jnp.sum(ends[None, :] <= tile_start[:, None], axis=1), N_EXPERTS - 1).astype(jnp.int32)
    rank8 = rank[:, :N_EXPERTS]
    sel8 = rank8 >= 0
    dest8 = starts[None, :] + rank8
    e_idx = jnp.arange(N_EXPERTS, dtype=jnp.int32)[None, :]
    first = jnp.min(jnp.where(sel8, e_idx, N_EXPERTS), axis=1, keepdims=True)
    second = jnp.max(jnp.where(sel8, e_idx, -1), axis=1, keepdims=True)
    pick = lambda a, e: jnp.sum(jnp.where(e_idx == e, a, 0), axis=1, keepdims=True)
    dest2 = jnp.concatenate([pick(dest8, first), pick(dest8, second)], axis=1).astype(jnp.int32)
    gates8 = gates[:, :N_EXPERTS]
    g12 = jnp.concatenate([pick(gates8, first), pick(gates8, second)], axis=1)
    xs = _scatter_rows(h_tiles, dest2, ends.astype(jnp.int32), n_rows)
    ys = _expert_ffn(xs, tile_expert, n_active.reshape(1), wg, wu, wd)
    return _combine(x2, ys, dest2, g12, g_final, final)


def kernel(x, positions, w_in, w_out, norm_mix, norm_ffn, da_lambda_q1, da_lambda_k1, da_lambda_q2,
           da_lambda_k2, da_subln, mla_q_norm, mla_w_uq, mla_kv_norm, mla_w_ukv, ffn_w_gate, ffn_w_up,
           ffn_w_down, moe_router, moe_w_gate, moe_w_up, moe_w_down, norm_final):
    batch, seq, d = x.shape
    depth = w_in.shape[0]
    t = batch * seq
    x2 = x.reshape(t, d).astype(F32)
    tabs = _rope_tables(positions)

    n_qk = 2 * DA_HEADS * 2 * DA_HEAD_DIM
    n_v = DA_HEADS * DA_V_DIM
    c0 = n_qk + n_v
    kv_per_head = MLA_NOPE_DIM + MLA_V_DIM

    for l in range(depth):
        wl = w_in[l]
        w_kr = jnp.zeros((d, LANES), F32).at[:, MLA_NOPE_DIM:MLA_QK_DIM].set(
            wl[:, c0 + MLA_Q_RANK + MLA_KV_RANK:])
        wc = jnp.concatenate([wl[:, c0:c0 + MLA_Q_RANK + MLA_KV_RANK], w_kr], axis=1)
        wuq = jnp.pad(mla_w_uq[l].reshape(MLA_Q_RANK, MLA_HEADS, MLA_QK_DIM),
                      ((0, 0), (0, 0), (0, LANES - MLA_QK_DIM))).reshape(MLA_Q_RANK, MLA_HEADS * LANES)
        wukv = mla_w_ukv[l].reshape(MLA_KV_RANK, MLA_HEADS, kv_per_head)
        wuk = jnp.pad(wukv[:, :, :MLA_NOPE_DIM],
                      ((0, 0), (0, 0), (0, LANES - MLA_NOPE_DIM))).reshape(MLA_KV_RANK, MLA_HEADS * LANES)
        wuv = wukv[:, :, MLA_NOPE_DIM:].reshape(MLA_KV_RANK, MLA_HEADS * MLA_V_DIM)
        w = dict(wqk=wl[:, :n_qk].astype(BF16), wv=wl[:, n_qk:c0].T.astype(BF16), wc=wc.astype(BF16),
                 gq=mla_q_norm[l].reshape(1, -1).astype(F32), wuq=wuq.astype(BF16),
                 gkv=mla_kv_norm[l].reshape(1, -1).astype(F32), wuk=wuk.astype(BF16), wuv=wuv.T.astype(BF16))

        da_q, da_k, da_vt, ml_q, ml_k, ml_vt = _prep(
            x2, norm_mix[l].reshape(1, d).astype(F32), w, tabs, batch, seq)

        lambda_init = 0.8 - 0.6 * float(np.exp(-0.3 * l))
        lam = (jnp.exp(jnp.sum(da_lambda_q1[l].astype(F32) * da_lambda_k1[l].astype(F32)))
               - jnp.exp(jnp.sum(da_lambda_q2[l].astype(F32) * da_lambda_k2[l].astype(F32)))
               + lambda_init).reshape(1).astype(F32)
        o_da = _attention(da_q, da_k, da_vt, batch, seq, 2, lam=lam,
                          subln=da_subln[l].reshape(DA_V_DIM, 1).astype(F32),
                          one_minus_lambda_init=1.0 - lambda_init)
        o_ml = _attention(ml_q, ml_k, ml_vt, batch, seq, 1)

        is_moe = l % 2 == 1
        wo = w_out[l].astype(BF16)
        x2, h, *h_tiles = _outproj(x2, o_da, o_ml, wo[:DA_WIDTH], wo[DA_WIDTH:],
                                   norm_ffn[l].reshape(1, d).astype(F32), F32 if is_moe else BF16, is_moe)
        i = l // 2
        last = l == depth - 1
        if not is_moe:
            x2 = _dense_ffn(x2, h, ffn_w_gate[i].astype(BF16), ffn_w_up[i].astype(BF16),
                            ffn_w_down[i].astype(BF16))
            if last:
                x2 = _final_norm(x2, norm_final.reshape(1, d).astype(F32))
        else:
            x2 = _moe_layer(x2, h, h_tiles[0], moe_router[i], moe_w_gate[i].astype(BF16),
                            moe_w_up[i].astype(BF16), moe_w_down[i].astype(BF16),
                            norm_final.reshape(1, d).astype(F32), last)
    return x2.reshape(batch, seq, d).astype(x.dtype)
```

```python
import functools
import math

import numpy as np
import jax
import jax.numpy as jnp
from jax import lax
from jax.experimental import pallas as pl
from jax.experimental.pallas import tpu as pltpu

D_MODEL = 1024
ROPE_THETA = 10000.0
NORM_EPS = 1e-6
SUBLN_EPS = 1e-5
DA_HEAD_DIM = 64
DA_V_DIM = 128
DA_HEADS = 4
DA_WIDTH = 512
MLA_HEADS = 4
MLA_V_DIM = 128
MLA_WIDTH = 512
MLA_Q_RANK = 256
MLA_KV_RANK = 128
MLA_NOPE_DIM = 64
MLA_ROPE_DIM = 32
MLA_QK_DIM = 96
N_EXPERTS = 8
TOP_K = 2
LOG2E = 1.4426950408889634

LANES = 128
SUBLANES = 8
VMEM_LIMIT = 56 * 1024 * 1024

TM_PREP = 512
PREP_SPLIT = 1
TQ = {1: 1024, 2: 1024}
ATTN_CHUNK = 256
SHIFT_KEYS = 64
VALUE_GROUP = 2048
ATTN_TRIP = 16
TM_FFN = 512
TM_EXP = 512
TM_ROW = 256
TM_ROUTE = 512
ROW_DMA_UNROLL = 8

BF16 = jnp.bfloat16
F32 = jnp.float32


def _cparams(sem):
    return pltpu.CompilerParams(dimension_semantics=sem, vmem_limit_bytes=VMEM_LIMIT)


def _rms(x, g, eps):
    return x * lax.rsqrt(jnp.mean(x * x, axis=-1, keepdims=True) + eps) * g


def _rope_table_kernel(pos_ref, inv_ref, cos_da_ref, sin_da_ref, cos_ml_ref, sin_ml_ref):
    ang = pos_ref[...] * inv_ref[...]
    c, s = jnp.cos(ang), jnp.sin(ang)
    h_da, h_ml = DA_HEAD_DIM // 2, MLA_ROPE_DIM // 2
    c_da, s_da = c[:, :h_da], s[:, :h_da]
    c_ml, s_ml = c[:, h_da:h_da + h_ml], s[:, h_da:h_da + h_ml]
    tm = ang.shape[0]
    cos_da_ref[...] = jnp.concatenate([c_da] * (LANES // h_da), axis=1)
    sin_da_ref[...] = jnp.concatenate([-s_da, s_da] * (LANES // DA_HEAD_DIM), axis=1)
    pad = LANES - MLA_QK_DIM
    cos_ml_ref[...] = jnp.concatenate(
        [jnp.ones((tm, MLA_NOPE_DIM), F32), c_ml, c_ml, jnp.ones((tm, pad), F32)], axis=1)
    sin_ml_ref[...] = jnp.concatenate(
        [jnp.zeros((tm, MLA_NOPE_DIM), F32), -s_ml, s_ml, jnp.zeros((tm, pad), F32)], axis=1)


def _rope_tables(positions):
    t = positions.size
    tm = min(TM_PREP, t)
    pos = positions.reshape(t, 1).astype(F32)
    inv_da = 1.0 / (ROPE_THETA ** (jnp.arange(0, DA_HEAD_DIM, 2, dtype=F32) / DA_HEAD_DIM))
    inv_ml = 1.0 / (ROPE_THETA ** (jnp.arange(0, MLA_ROPE_DIM, 2, dtype=F32) / MLA_ROPE_DIM))
    inv = jnp.concatenate([inv_da, inv_ml, jnp.zeros((LANES - inv_da.size - inv_ml.size,), F32)])[None, :]
    row = pl.BlockSpec((tm, LANES), lambda i: (i, 0))
    out = jax.ShapeDtypeStruct((t, LANES), F32)
    return pl.pallas_call(
        _rope_table_kernel,
        grid=(t // tm,),
        in_specs=[pl.BlockSpec((tm, 1), lambda i: (i, 0)), pl.BlockSpec((1, LANES), lambda i: (0, 0))],
        out_specs=[row, row, row, row],
        out_shape=[out, out, out, out],
        compiler_params=_cparams(("parallel",)),
        name="rope_tables",
    )(pos, inv)


def _rotate_half_matrices():
    lane = np.arange(LANES)
    half_da, half_ml = DA_HEAD_DIM // 2, MLA_ROPE_DIM // 2
    partner_da = np.where(lane % DA_HEAD_DIM < half_da, lane + half_da, lane - half_da)
    r_da = np.zeros((LANES, LANES), np.float32)
    r_da[partner_da, lane] = 1.0
    in_rope = (lane >= MLA_NOPE_DIM) & (lane < MLA_QK_DIM)
    partner_ml = np.where(lane < MLA_NOPE_DIM + half_ml, lane + half_ml, lane - half_ml)
    r_ml = np.zeros((LANES, LANES), np.float32)
    r_ml[partner_ml[in_rope], lane[in_rope]] = 1.0
    pair = lambda r: jnp.asarray(np.kron(np.eye(2, dtype=np.float32), r), BF16)
    return pair(r_da), pair(r_ml)


def _rotary_pair(pair, rot_ref, cos, sin):
    rot = jnp.dot(pair.astype(BF16), rot_ref[...], preferred_element_type=F32)
    return pair * cos + rot * sin


def _prep_kernel(x_ref, g_ref, wqk_ref, wv_ref, wc_ref, gq_ref, wuq_ref, gkv_ref, wuk_ref, wuv_ref,
                 rda_ref, rml_ref, cda_ref, sda_ref, cml_ref, sml_ref,
                 daq_ref, dak_ref, davt_ref, mlq_ref, mlk_ref, mlvt_ref):
    tm = x_ref.shape[0]
    sub = tm // PREP_SPLIT
    pair_w = 2 * LANES
    q_scale = DA_HEAD_DIM ** -0.5 * LOG2E
    ml_scale = MLA_QK_DIM ** -0.5 * LOG2E

    for part in range(PREP_SPLIT):
        rows = slice(part * sub, (part + 1) * sub)
        hb = _rms(x_ref[rows, :], g_ref[...], NORM_EPS).astype(BF16)

        qk = jnp.dot(hb, wqk_ref[...], preferred_element_type=F32)
        cda = jnp.concatenate([cda_ref[rows, :]] * 2, axis=1)
        sda = jnp.concatenate([sda_ref[rows, :]] * 2, axis=1)
        for jp in range(DA_HEADS):
            r = _rotary_pair(qk[:, jp * pair_w:(jp + 1) * pair_w], rda_ref, cda, sda)
            if jp < DA_HEADS // 2:
                daq_ref[rows, jp * pair_w:(jp + 1) * pair_w] = (r * q_scale).astype(BF16)
            else:
                jj = jp - DA_HEADS // 2
                dak_ref[rows, jj * pair_w:(jj + 1) * pair_w] = r.astype(BF16)
        davt_ref[:, rows] = lax.dot_general(wv_ref[...], hb, (((1,), (1,)), ((), ())),
                                            preferred_element_type=F32).astype(BF16)

        c = jnp.dot(hb, wc_ref[...], preferred_element_type=F32)
        cml1, sml1 = cml_ref[rows, :], sml_ref[rows, :]
        cml = jnp.concatenate([cml1] * 2, axis=1)
        sml = jnp.concatenate([sml1] * 2, axis=1)
        cq = _rms(c[:, :MLA_Q_RANK], gq_ref[...], NORM_EPS).astype(BF16)
        qm = jnp.dot(cq, wuq_ref[...], preferred_element_type=F32)
        for jp in range(MLA_HEADS // 2):
            r = _rotary_pair(qm[:, jp * pair_w:(jp + 1) * pair_w], rml_ref, cml, sml)
            mlq_ref[rows, jp * pair_w:(jp + 1) * pair_w] = (r * ml_scale).astype(BF16)
        ckv = _rms(c[:, MLA_Q_RANK:MLA_Q_RANK + MLA_KV_RANK], gkv_ref[...], NORM_EPS).astype(BF16)
        kr = c[:, MLA_Q_RANK + MLA_KV_RANK:]
        kr_rot = jnp.dot(kr.astype(BF16), rml_ref[:LANES, :LANES], preferred_element_type=F32)
        kr = kr * cml1 + kr_rot * sml1
        kn = jnp.dot(ckv, wuk_ref[...], preferred_element_type=F32)
        for j in range(MLA_HEADS):
            mlk_ref[rows, j * LANES:(j + 1) * LANES] = (kn[:, j * LANES:(j + 1) * LANES] + kr).astype(BF16)
        mlvt_ref[:, rows] = lax.dot_general(wuv_ref[...], ckv, (((1,), (1,)), ((), ())),
                                            preferred_element_type=F32).astype(BF16)


def _prep(x2, g_mix, w, tabs, batch, seq):
    t = x2.shape[0]
    tm = min(TM_PREP, seq)
    nps = seq // tm
    row = lambda width: pl.BlockSpec((tm, width), lambda i: (i, 0))
    full = lambda a: pl.BlockSpec(a.shape, lambda i: (0,) * a.ndim)
    vt_spec = pl.BlockSpec((None, DA_WIDTH, tm), lambda i: (i // nps, 0, i % nps))
    tok = jax.ShapeDtypeStruct((t, DA_WIDTH), BF16)
    vts = jax.ShapeDtypeStruct((batch, DA_WIDTH, seq), BF16)
    weights = [w["wqk"], w["wv"], w["wc"], w["gq"], w["wuq"], w["gkv"], w["wuk"], w["wuv"],
               *_rotate_half_matrices()]
    return pl.pallas_call(
        _prep_kernel,
        grid=(t // tm,),
        in_specs=[row(D_MODEL), full(g_mix)] + [full(a) for a in weights] + [row(LANES)] * 4,
        out_specs=[row(DA_WIDTH), row(DA_WIDTH), vt_spec, row(DA_WIDTH), row(DA_WIDTH), vt_spec],
        out_shape=[tok, tok, vts, tok, tok, vts],
        compiler_params=_cparams(("parallel",)),
        name="prep",
    )(x2, g_mix, *weights, *tabs)


def _attn_kernel(*refs, n_maps, one_minus_lambda_init):
    if n_maps == 2:
        lam_ref, q_ref, k_ref, vt_ref, g_ref, o_ref = refs[:6]
    else:
        q_ref, k_ref, vt_ref, o_ref = refs[:4]
    p_ref, acc_ref, l_ref, m_ref, qm_ref = refs[-5:]
    seq = k_ref.shape[0]
    tq = q_ref.shape[0]
    tkc = ATTN_CHUNK
    n_chunks = seq // tkc
    q = q_ref[...]
    if n_maps == 2:
        lane = lax.broadcasted_iota(jnp.int32, q.shape, 1)
        zero = jnp.zeros_like(q)
        qm_ref[0] = jnp.where(lane < DA_HEAD_DIM, q, zero)
        qm_ref[1] = jnp.where(lane >= DA_HEAD_DIM, q, zero)
    else:
        qm_ref[0] = q

    def key_rows(c, rows):
        return k_ref[pl.ds(pl.multiple_of(c * tkc, rows), rows), :]

    def scores(k, mp):
        return lax.dot_general(k, qm_ref[mp], (((1,), (1,)), ((), ())), preferred_element_type=F32)

    def sublane_sums(p):
        return jnp.sum(p.reshape(p.shape[0] // SUBLANES, SUBLANES, tq), axis=0)

    for mp in range(n_maps):
        m_ref[mp] = jnp.max(scores(key_rows(0, SHIFT_KEYS), mp), axis=0, keepdims=True)
    l_ref[...] = jnp.zeros(l_ref.shape, F32)

    per_trip = math.gcd(ATTN_TRIP, n_chunks)

    def trip(t, _):
        for off in range(per_trip):
            c = t * per_trip + off
            k = key_rows(c, tkc)
            for mp in range(n_maps):
                p = jnp.exp2(scores(k, mp) - m_ref[mp])
                l_ref[mp] = l_ref[mp] + sublane_sums(p)
                p_ref[mp, pl.ds(pl.multiple_of(c * tkc, tkc), tkc), :] = p.astype(BF16)
        return 0

    lax.fori_loop(0, n_chunks // per_trip, trip, 0)

    def denominator(mp):
        return jnp.sum(l_ref[mp], axis=0, keepdims=True)

    def normalized(mp):
        return acc_ref[mp] / denominator(mp)

    if n_maps == 2:
        ratio = (lam_ref[0] * denominator(0) / denominator(1)).astype(BF16)
        group = math.gcd(VALUE_GROUP, seq)
        acc = jnp.zeros(acc_ref.shape[1:], F32)
        for k0 in range(0, seq, group):
            w = p_ref[0, k0:k0 + group, :] - ratio * p_ref[1, k0:k0 + group, :]
            acc = acc + jnp.dot(vt_ref[:, k0:k0 + group], w, preferred_element_type=F32)
        acc_ref[0] = acc / denominator(0)
    else:
        acc_ref[0] = jnp.dot(vt_ref[...], p_ref[0], preferred_element_type=F32) / denominator(0)

    def non_finite(x):
        return jnp.max(jnp.where(jnp.isfinite(x), 0.0, 1.0)) > 0.0

    @pl.when(non_finite(acc_ref[0]) | non_finite(l_ref[...]))
    def _():
        m_ref[...] = jnp.full(m_ref.shape, -jnp.inf, F32)
        l_ref[...] = jnp.zeros(l_ref.shape, F32)
        acc_ref[...] = jnp.zeros(acc_ref.shape, F32)

        def chunk(c, _):
            vt = vt_ref[:, pl.ds(pl.multiple_of(c * tkc, tkc), tkc)]
            k = key_rows(c, tkc)
            for mp in range(n_maps):
                s = scores(k, mp)
                m_old = m_ref[mp]
                m_new = jnp.maximum(m_old, jnp.max(s, axis=0, keepdims=True))
                alpha = jnp.exp2(m_old - m_new)
                p = jnp.exp2(s - m_new)
                l_ref[mp] = alpha * l_ref[mp] + sublane_sums(p)
                acc_ref[mp] = alpha * acc_ref[mp] + jnp.dot(vt, p.astype(BF16), preferred_element_type=F32)
                m_ref[mp] = m_new
            return 0

        lax.fori_loop(0, n_chunks, chunk, 0)
        if n_maps == 2:
            acc_ref[0] = normalized(0) - lam_ref[0] * normalized(1)
        else:
            acc_ref[0] = normalized(0)

    o = acc_ref[0]
    if n_maps == 2:
        ms = jnp.mean(o * o, axis=0, keepdims=True)
        o = o * lax.rsqrt(ms + SUBLN_EPS) * g_ref[...] * one_minus_lambda_init
    o_ref[...] = o.T.astype(o_ref.dtype)


def _attention(q, k, vt, batch, seq, n_maps, lam=None, subln=None, one_minus_lambda_init=1.0):
    t = q.shape[0]
    tq = min(TQ[n_maps], seq)
    nq = seq // tq
    heads = q.shape[1] // LANES
    q_spec = pl.BlockSpec((tq, LANES), lambda b, h, i: (b * nq + i, h))
    k_spec = pl.BlockSpec((seq, LANES), lambda b, h, i: (b, h))
    vt_spec = pl.BlockSpec((None, DA_V_DIM, seq), lambda b, h, i: (b, h, 0))
    in_specs = [q_spec, k_spec, vt_spec]
    args = [q, k, vt]
    if n_maps == 2:
        in_specs = [pl.BlockSpec(memory_space=pltpu.SMEM)] + in_specs + [
            pl.BlockSpec((DA_V_DIM, 1), lambda b, h, i: (0, 0))]
        args = [lam] + args + [subln]
    scratch = [pltpu.VMEM((n_maps, seq, tq), BF16),
               pltpu.VMEM((n_maps, DA_V_DIM, tq), F32),
               pltpu.VMEM((n_maps, SUBLANES, tq), F32),
               pltpu.VMEM((n_maps, 1, tq), F32),
               pltpu.VMEM((n_maps, tq, LANES), BF16)]
    return pl.pallas_call(
        functools.partial(_attn_kernel, n_maps=n_maps, one_minus_lambda_init=one_minus_lambda_init),
        grid=(batch, heads, nq),
        in_specs=in_specs,
        out_specs=q_spec,
        scratch_shapes=scratch,
        out_shape=jax.ShapeDtypeStruct((t, q.shape[1]), BF16),
        compiler_params=_cparams(("parallel", "parallel", "parallel")),
        name="diff_attn" if n_maps == 2 else "mla_attn",
    )(*args)


def _to_row_tiles(ref, val):
    groups = val.shape[0] // SUBLANES
    for j in range(D_MODEL // LANES):
        ref[:, j] = val[:, j * LANES:(j + 1) * LANES].reshape(groups, SUBLANES, LANES)


def _from_row_tiles(ref):
    rows = ref.shape[0] * SUBLANES
    return jnp.concatenate([ref[:, j].reshape(rows, LANES) for j in range(D_MODEL // LANES)], axis=1)


def _row_of(ref, r):
    return ref.at[lax.shift_right_logical(r, 3), :, pl.ds(jnp.bitwise_and(r, SUBLANES - 1), 1), :]


ROW_TILES = (D_MODEL // LANES, SUBLANES, LANES)


def _outproj_kernel(x_ref, oda_ref, oml_ref, wa_ref, wb_ref, g_ref, xo_ref, h_ref, *ht_ref):
    y = x_ref[...] + jnp.dot(oda_ref[...], wa_ref[...], preferred_element_type=F32)
    y = y + jnp.dot(oml_ref[...], wb_ref[...], preferred_element_type=F32)
    xo_ref[...] = y
    h = _rms(y, g_ref[...], NORM_EPS)
    h_ref[...] = h.astype(h_ref.dtype)
    if ht_ref:
        _to_row_tiles(ht_ref[0], h)


def _outproj(x2, o_da, o_ml, w_a, w_b, g_ffn, h_dtype, row_tiles):
    t = x2.shape[0]
    tm = min(TM_PREP, t)
    row = lambda width: pl.BlockSpec((tm, width), lambda i: (i, 0))
    full = lambda a: pl.BlockSpec(a.shape, lambda i: (0,) * a.ndim)
    out_specs = [row(D_MODEL), row(D_MODEL)]
    out_shape = [jax.ShapeDtypeStruct((t, D_MODEL), F32), jax.ShapeDtypeStruct((t, D_MODEL), h_dtype)]
    if row_tiles:
        out_specs.append(pl.BlockSpec((tm // SUBLANES,) + ROW_TILES, lambda i: (i, 0, 0, 0)))
        out_shape.append(jax.ShapeDtypeStruct((t // SUBLANES,) + ROW_TILES, F32))
    return pl.pallas_call(
        _outproj_kernel,
        grid=(t // tm,),
        in_specs=[row(D_MODEL), row(DA_WIDTH), row(MLA_WIDTH), full(w_a), full(w_b), full(g_ffn)],
        out_specs=out_specs,
        out_shape=out_shape,
        compiler_params=_cparams(("parallel",)),
        name="outproj",
    )(x2, o_da, o_ml, w_a, w_b, g_ffn)


def _ffn_kernel(x_ref, h_ref, wg_ref, wu_ref, wd_ref, o_ref):
    h = h_ref[...]
    g = jnp.dot(h, wg_ref[...], preferred_element_type=F32)
    u = jnp.dot(h, wu_ref[...], preferred_element_type=F32)
    a = (g * jax.nn.sigmoid(g) * u).astype(BF16)
    o_ref[...] = x_ref[...] + jnp.dot(a, wd_ref[...], preferred_element_type=F32)


def _dense_ffn(x2, h, wg, wu, wd):
    t = x2.shape[0]
    tm = min(TM_FFN, t)
    row = pl.BlockSpec((tm, D_MODEL), lambda i: (i, 0))
    res = lambda a: pl.BlockSpec(a.shape, lambda i: (0, 0), pipeline_mode=pl.Buffered(1))
    return pl.pallas_call(
        _ffn_kernel,
        grid=(t // tm,),
        in_specs=[row, row, res(wg), res(wu), res(wd)],
        out_specs=row,
        out_shape=jax.ShapeDtypeStruct((t, D_MODEL), F32),
        compiler_params=_cparams(("parallel",)),
        name="dense_ffn",
    )(x2, h, wg, wu, wd)


def _router_kernel(h_ref, wr_ref, gates_ref, rank_ref, cnt_ref):
    tm = h_ref.shape[0]

    @pl.when(pl.program_id(0) == 0)
    def _():
        cnt_ref[...] = jnp.zeros_like(cnt_ref)

    lane = lax.broadcasted_iota(jnp.int32, (tm, LANES), 1)
    logits = jnp.dot(h_ref[...], wr_ref[...], preferred_element_type=F32, precision=lax.Precision.HIGHEST)
    neg = jnp.float32(-jnp.inf)
    logits = jnp.where(lane < N_EXPERTS, logits, neg)
    v1 = jnp.max(logits, axis=1, keepdims=True)
    i1 = jnp.min(jnp.where(logits == v1, lane, LANES), axis=1, keepdims=True)
    sel1 = lane == i1
    rest = jnp.where(sel1, neg, logits)
    v2 = jnp.max(rest, axis=1, keepdims=True)
    i2 = jnp.min(jnp.where(rest == v2, lane, LANES), axis=1, keepdims=True)
    sel2 = lane == i2
    e = jnp.exp(v2 - v1)
    g1 = 1.0 / (1.0 + e)
    g2 = e / (1.0 + e)
    gates_ref[...] = jnp.where(sel1, g1, jnp.where(sel2, g2, 0.0))
    sel = jnp.where(sel1 | sel2, 1.0, 0.0)
    r_i = lax.broadcasted_iota(jnp.int32, (tm, tm), 0)
    c_i = lax.broadcasted_iota(jnp.int32, (tm, tm), 1)
    tri = jnp.where(c_i < r_i, 1.0, 0.0).astype(BF16)
    before = jnp.dot(tri, sel.astype(BF16), preferred_element_type=F32) + cnt_ref[0:1, :]
    rank_ref[...] = jnp.where(sel > 0, before, -1.0).astype(jnp.int32)
    cnt_ref[...] = cnt_ref[...] + jnp.sum(sel, axis=0, keepdims=True)


def _router(h, w_router_pad):
    t = h.shape[0]
    tm = min(TM_ROUTE, t)
    row = lambda width: pl.BlockSpec((tm, width), lambda i: (i, 0))
    return pl.pallas_call(
        _router_kernel,
        grid=(t // tm,),
        in_specs=[row(D_MODEL), pl.BlockSpec(w_router_pad.shape, lambda i: (0, 0))],
        out_specs=[row(LANES), row(LANES), pl.BlockSpec((8, LANES), lambda i: (0, 0))],
        out_shape=[jax.ShapeDtypeStruct((t, LANES), F32), jax.ShapeDtypeStruct((t, LANES), jnp.int32),
                   jax.ShapeDtypeStruct((8, LANES), F32)],
        compiler_params=_cparams(("arbitrary",)),
        name="router",
    )(h, w_router_pad)


def _scatter_kernel(ends_ref, dest_ref, h_ref, xs_ref, zero_ref, sem):
    tm = h_ref.shape[0] * SUBLANES
    tile_groups = TM_EXP // SUBLANES

    def row_copy(g, s, k):
        return pltpu.make_async_copy(h_ref.at[g, :, pl.ds(s, 1), :],
                                     _row_of(xs_ref, dest_ref[0, 2 * (SUBLANES * g + s) + k]), sem)

    @pl.when(pl.program_id(0) == 0)
    def _():
        zero_ref[...] = jnp.zeros_like(zero_ref)
        copies = []
        for e in range(N_EXPERTS):
            start = pl.multiple_of(jnp.maximum(ends_ref[e] - TM_EXP, 0) // SUBLANES, tile_groups)
            cp = pltpu.make_async_copy(zero_ref, xs_ref.at[pl.ds(start, tile_groups)], sem)
            cp.start()
            copies.append(cp)
        for cp in copies:
            cp.wait()

        def zero_tail(i, _):
            cp = pltpu.make_async_copy(
                zero_ref, xs_ref.at[pl.ds(pl.multiple_of(i * tile_groups, tile_groups), tile_groups)], sem)
            cp.start()
            cp.wait()
            return 0

        lax.fori_loop(ends_ref[N_EXPERTS - 1] // TM_EXP, xs_ref.shape[0] // tile_groups, zero_tail, 0)

    def issue(g, _):
        for s in range(SUBLANES):
            row_copy(g, s, 0).start(priority=0)
            row_copy(g, s, 1).start(priority=1)
        return 0

    lax.fori_loop(0, tm // SUBLANES, issue, 0)

    def drain(g, _):
        for s in range(SUBLANES):
            row_copy(g, s, 0).wait()
            row_copy(g, s, 1).wait()
        return 0

    lax.fori_loop(0, tm // SUBLANES, drain, 0)


def _scatter_rows(h, dest2, ends, n_rows):
    t = h.shape[0] * SUBLANES
    tm = min(TM_ROW, t)
    nt = t // tm
    dest3 = dest2.reshape(nt, 1, 2 * tm)
    grid_spec = pltpu.PrefetchScalarGridSpec(
        num_scalar_prefetch=1,
        grid=(nt,),
        in_specs=[pl.BlockSpec((None, 1, 2 * tm), lambda i, ends: (i, 0, 0), memory_space=pltpu.SMEM),
                  pl.BlockSpec((tm // SUBLANES,) + ROW_TILES, lambda i, ends: (i, 0, 0, 0))],
        out_specs=pl.BlockSpec(memory_space=pl.ANY),
        scratch_shapes=[pltpu.VMEM((TM_EXP // SUBLANES,) + ROW_TILES, F32), pltpu.SemaphoreType.DMA(())],
    )
    return pl.pallas_call(
        _scatter_kernel,
        grid_spec=grid_spec,
        out_shape=jax.ShapeDtypeStruct((n_rows // SUBLANES,) + ROW_TILES, F32),
        compiler_params=_cparams(("arbitrary",)),
        name="scatter_rows",
    )(ends, dest3, h)


def _expert_kernel(te_ref, na_ref, xs_ref, wg_ref, wu_ref, wd_ref, ys_ref, acc_ref):
    i = pl.program_id(0)
    j = pl.program_id(1)

    @pl.when((i == 0) & (j == 0))
    def _():
        acc_ref[...] = jnp.zeros_like(acc_ref)

    @pl.when(i < na_ref[0])
    def _():
        xb = _from_row_tiles(xs_ref).astype(BF16)
        g = jnp.dot(xb, wg_ref[...], preferred_element_type=F32)
        u = jnp.dot(xb, wu_ref[...], preferred_element_type=F32)
        a = (g * jax.nn.sigmoid(g) * u).astype(BF16)
        acc = jnp.where(j > 0, acc_ref[...], 0.0) + jnp.dot(a, wd_ref[...], preferred_element_type=F32)
        acc_ref[...] = acc
        _to_row_tiles(ys_ref, acc)

    @pl.when(i >= na_ref[0])
    def _():
        ys_ref[...] = jnp.zeros_like(ys_ref)


def _expert_ffn(xs, tile_expert, n_active, wg, wu, wd):
    n_rows = xs.shape[0] * SUBLANES
    ff = wg.shape[2]
    tf = ff // 2
    nj = ff // tf
    nt = n_rows // TM_EXP

    def wj(i, j, na):
        return jnp.where(i < na[0], j, nj - 1)

    row = pl.BlockSpec((TM_EXP // SUBLANES,) + ROW_TILES, lambda i, j, te, na: (i, 0, 0, 0))
    grid_spec = pltpu.PrefetchScalarGridSpec(
        num_scalar_prefetch=2,
        grid=(nt, nj),
        in_specs=[row,
                  pl.BlockSpec((None, D_MODEL, tf), lambda i, j, te, na: (te[i], 0, wj(i, j, na))),
                  pl.BlockSpec((None, D_MODEL, tf), lambda i, j, te, na: (te[i], 0, wj(i, j, na))),
                  pl.BlockSpec((None, tf, D_MODEL), lambda i, j, te, na: (te[i], wj(i, j, na), 0))],
        out_specs=row,
        scratch_shapes=[pltpu.VMEM((TM_EXP, D_MODEL), F32)],
    )
    return pl.pallas_call(
        _expert_kernel,
        grid_spec=grid_spec,
        out_shape=jax.ShapeDtypeStruct((n_rows // SUBLANES,) + ROW_TILES, F32),
        compiler_params=_cparams(("arbitrary", "arbitrary")),
        name="expert_ffn",
    )(tile_expert, n_active, xs, wg, wu, wd)


def _combine_kernel(dest_ref, x_ref, g12_ref, gfin_ref, ys_ref, o_ref, buf_ref, sem, *, final):
    tm = x_ref.shape[0]

    def row_copy(g, s, k):
        return pltpu.make_async_copy(_row_of(ys_ref, dest_ref[0, 2 * (SUBLANES * g + s) + k]),
                                     buf_ref.at[k, g, :, pl.ds(s, 1), :], sem)

    def issue(g, _):
        for s in range(SUBLANES):
            row_copy(g, s, 0).start(priority=0)
            row_copy(g, s, 1).start(priority=1)
        return 0

    lax.fori_loop(0, tm // SUBLANES, issue, 0)

    def drain(g, _):
        for s in range(SUBLANES):
            row_copy(g, s, 0).wait()
            row_copy(g, s, 1).wait()
        return 0

    lax.fori_loop(0, tm // SUBLANES, drain, 0)
    g12 = g12_ref[...]
    y = x_ref[...] + g12[:, 0:1] * _from_row_tiles(buf_ref.at[0]) + g12[:, 1:2] * _from_row_tiles(buf_ref.at[1])
    o_ref[...] = _rms(y, gfin_ref[...], NORM_EPS) if final else y


def _combine(x2, ys, dest2, g12, g_final, final):
    t = x2.shape[0]
    tm = min(TM_ROW, t)
    nt = t // tm
    dest3 = dest2.reshape(nt, 1, 2 * tm)
    row = pl.BlockSpec((tm, D_MODEL), lambda i: (i, 0))
    return pl.pallas_call(
        functools.partial(_combine_kernel, final=final),
        grid=(nt,),
        in_specs=[pl.BlockSpec((None, 1, 2 * tm), lambda i: (i, 0, 0), memory_space=pltpu.SMEM),
                  row, pl.BlockSpec((tm, 2), lambda i: (i, 0)),
                  pl.BlockSpec((1, D_MODEL), lambda i: (0, 0)),
                  pl.BlockSpec(memory_space=pl.ANY)],
        out_specs=row,
        out_shape=jax.ShapeDtypeStruct((t, D_MODEL), F32),
        scratch_shapes=[pltpu.VMEM((2, tm // SUBLANES) + ROW_TILES, F32), pltpu.SemaphoreType.DMA(())],
        compiler_params=_cparams(("arbitrary",)),
        name="combine",
    )(dest3, x2, g12, g_final, ys)


def _final_norm_kernel(x_ref, g_ref, o_ref):
    o_ref[...] = _rms(x_ref[...], g_ref[...], NORM_EPS)


def _final_norm(x2, g):
    t = x2.shape[0]
    tm = min(TM_PREP, t)
    row = pl.BlockSpec((tm, D_MODEL), lambda i: (i, 0))
    return pl.pallas_call(
        _final_norm_kernel, grid=(t // tm,),
        in_specs=[row, pl.BlockSpec((1, D_MODEL), lambda i: (0, 0))], out_specs=row,
        out_shape=jax.ShapeDtypeStruct((t, D_MODEL), F32),
        compiler_params=_cparams(("parallel",)), name="final_norm",
    )(x2, g)


def _moe_layer(x2, h, h_tiles, router, wg, wu, wd, g_final, final):
    t = x2.shape[0]
    w_router_pad = jnp.zeros((D_MODEL, LANES), F32).at[:, :N_EXPERTS].set(router.astype(F32))
    gates, rank, cnt = _router(h, w_router_pad)
    counts = cnt[0, :N_EXPERTS].astype(jnp.int32)
    padded = ((counts + TM_EXP - 1) // TM_EXP) * TM_EXP
    ends = jnp.cumsum(padded)
    starts = ends - padded
    n_rows = TOP_K * t + N_EXPERTS * TM_EXP
    nt = n_rows // TM_EXP
    n_active = (ends[-1] // TM_EXP).astype(jnp.int32)
    tile_start = jnp.minimum(jnp.arange(nt, dtype=jnp.int32), n_active - 1) * TM_EXP
    tile_expert = jnp.minimum(jnp.sum(ends[None, :] <= tile_start[:, None], axis=1), N_EXPERTS - 1).astype(jnp.int32)
    rank8 = rank[:, :N_EXPERTS]
    sel8 = rank8 >= 0
    dest8 = starts[None, :] + rank8
    e_idx = jnp.arange(N_EXPERTS, dtype=jnp.int32)[None, :]
    first = jnp.min(jnp.where(sel8, e_idx, N_EXPERTS), axis=1, keepdims=True)
    second = jnp.max(jnp.where(sel8, e_idx, -1), axis=1, keepdims=True)
    pick = lambda a, e: jnp.sum(jnp.where(e_idx == e, a, 0), axis=1, keepdims=True)
    dest2 = jnp.concatenate([pick(dest8, first), pick(dest8, second)], axis=1).astype(jnp.int32)
    gates8 = gates[:, :N_EXPERTS]
    g12 = jnp.concatenate([pick(gates8, first), pick(gates8, second)], axis=1)
    xs = _scatter_rows(h_tiles, dest2, ends.astype(jnp.int32), n_rows)
    ys = _expert_ffn(xs, tile_expert, n_active.reshape(1), wg, wu, wd)
    return _combine(x2, ys, dest2, g12, g_final, final)


def kernel(x, positions, w_in, w_out, norm_mix, norm_ffn, da_lambda_q1, da_lambda_k1, da_lambda_q2,
           da_lambda_k2, da_subln, mla_q_norm, mla_w_uq, mla_kv_norm, mla_w_ukv, ffn_w_gate, ffn_w_up,
           ffn_w_down, moe_router, moe_w_gate, moe_w_up, moe_w_down, norm_final):
    batch, seq, d = x.shape
    depth = w_in.shape[0]
    t = batch * seq
    x2 = x.reshape(t, d).astype(F32)
    tabs = _rope_tables(positions)

    n_qk = 2 * DA_HEADS * 2 * DA_HEAD_DIM
    n_v = DA_HEADS * DA_V_DIM
    c0 = n_qk + n_v
    kv_per_head = MLA_NOPE_DIM + MLA_V_DIM

    for l in range(depth):
        wl = w_in[l]
        w_kr = jnp.zeros((d, LANES), F32).at[:, MLA_NOPE_DIM:MLA_QK_DIM].set(
            wl[:, c0 + MLA_Q_RANK + MLA_KV_RANK:])
        wc = jnp.concatenate([wl[:, c0:c0 + MLA_Q_RANK + MLA_KV_RANK], w_kr], axis=1)
        wuq = jnp.pad(mla_w_uq[l].reshape(MLA_Q_RANK, MLA_HEADS, MLA_QK_DIM),
                      ((0, 0), (0, 0), (0, LANES - MLA_QK_DIM))).reshape(MLA_Q_RANK, MLA_HEADS * LANES)
        wukv = mla_w_ukv[l].reshape(MLA_KV_RANK, MLA_HEADS, kv_per_head)
        wuk = jnp.pad(wukv[:, :, :MLA_NOPE_DIM],
                      ((0, 0), (0, 0), (0, LANES - MLA_NOPE_DIM))).reshape(MLA_KV_RANK, MLA_HEADS * LANES)
        wuv = wukv[:, :, MLA_NOPE_DIM:].reshape(MLA_KV_RANK, MLA_HEADS * MLA_V_DIM)
        w = dict(wqk=wl[:, :n_qk].astype(BF16), wv=wl[:, n_qk:c0].T.astype(BF16), wc=wc.astype(BF16),
                 gq=mla_q_norm[l].reshape(1, -1).astype(F32), wuq=wuq.astype(BF16),
                 gkv=mla_kv_norm[l].reshape(1, -1).astype(F32), wuk=wuk.astype(BF16), wuv=wuv.T.astype(BF16))

        da_q, da_k, da_vt, ml_q, ml_k, ml_vt = _prep(
            x2, norm_mix[l].reshape(1, d).astype(F32), w, tabs, batch, seq)

        lambda_init = 0.8 - 0.6 * float(np.exp(-0.3 * l))
        lam = (jnp.exp(jnp.sum(da_lambda_q1[l].astype(F32) * da_lambda_k1[l].astype(F32)))
               - jnp.exp(jnp.sum(da_lambda_q2[l].astype(F32) * da_lambda_k2[l].astype(F32)))
               + lambda_init).reshape(1).astype(F32)
        o_da = _attention(da_q, da_k, da_vt, batch, seq, 2, lam=lam,
                          subln=da_subln[l].reshape(DA_V_DIM, 1).astype(F32),
                          one_minus_lambda_init=1.0 - lambda_init)
        o_ml = _attention(ml_q, ml_k, ml_vt, batch, seq, 1)

        is_moe = l % 2 == 1
        wo = w_out[l].astype(BF16)
        x2, h, *h_tiles = _outproj(x2, o_da, o_ml, wo[:DA_WIDTH], wo[DA_WIDTH:],
                                   norm_ffn[l].reshape(1, d).astype(F32), F32 if is_moe else BF16, is_moe)
        i = l // 2
        last = l == depth - 1
        if not is_moe:
            x2 = _dense_ffn(x2, h, ffn_w_gate[i].astype(BF16), ffn_w_up[i].astype(BF16),
                            ffn_w_down[i].astype(BF16))
            if last:
                x2 = _final_norm(x2, norm_final.reshape(1, d).astype(F32))
        else:
            x2 = _moe_layer(x2, h, h_tiles[0], moe_router[i], moe_w_gate[i].astype(BF16),
                            moe_w_up[i].astype(BF16), moe_w_down[i].astype(BF16),
                            norm_final.reshape(1, d).astype(F32), last)
    return x2.reshape(batch, seq, d).astype(x.dtype)
```

```python
import functools
import math

import numpy as np
import jax
import jax.numpy as jnp
from jax import lax
from jax.experimental import pallas as pl
from jax.experimental.pallas import tpu as pltpu

D_MODEL = 1024
ROPE_THETA = 10000.0
NORM_EPS = 1e-6
SUBLN_EPS = 1e-5
DA_HEAD_DIM = 64
DA_V_DIM = 128
DA_HEADS = 4
DA_WIDTH = 512
MLA_HEADS = 4
MLA_V_DIM = 128
MLA_WIDTH = 512
MLA_Q_RANK = 256
MLA_KV_RANK = 128
MLA_NOPE_DIM = 64
MLA_ROPE_DIM = 32
MLA_QK_DIM = 96
N_EXPERTS = 8
TOP_K = 2
LOG2E = 1.4426950408889634

LANES = 128
SUBLANES = 8
VMEM_LIMIT = 56 * 1024 * 1024

TM_PREP = 512
PREP_SPLIT = 1
TQ = {1: 1024, 2: 1024}
ATTN_CHUNK = 256
SHIFT_KEYS = 64
VALUE_GROUP = 2048
ATTN_TRIP = 16
TM_FFN = 512
TM_EXP = 512
TM_ROW = 256
TM_ROUTE = 512
ROW_DMA_UNROLL = 8

BF16 = jnp.bfloat16
F32 = jnp.float32


def _cparams(sem):
    return pltpu.CompilerParams(dimension_semantics=sem, vmem_limit_bytes=VMEM_LIMIT)


def _rms(x, g, eps):
    return x * lax.rsqrt(jnp.mean(x * x, axis=-1, keepdims=True) + eps) * g


def _rope_table_kernel(pos_ref, inv_ref, cos_da_ref, sin_da_ref, cos_ml_ref, sin_ml_ref):
    ang = pos_ref[...] * inv_ref[...]
    c, s = jnp.cos(ang), jnp.sin(ang)
    h_da, h_ml = DA_HEAD_DIM // 2, MLA_ROPE_DIM // 2
    c_da, s_da = c[:, :h_da], s[:, :h_da]
    c_ml, s_ml = c[:, h_da:h_da + h_ml], s[:, h_da:h_da + h_ml]
    tm = ang.shape[0]
    cos_da_ref[...] = jnp.concatenate([c_da] * (LANES // h_da), axis=1)
    sin_da_ref[...] = jnp.concatenate([-s_da, s_da] * (LANES // DA_HEAD_DIM), axis=1)
    pad = LANES - MLA_QK_DIM
    cos_ml_ref[...] = jnp.concatenate(
        [jnp.ones((tm, MLA_NOPE_DIM), F32), c_ml, c_ml, jnp.ones((tm, pad), F32)], axis=1)
    sin_ml_ref[...] = jnp.concatenate(
        [jnp.zeros((tm, MLA_NOPE_DIM), F32), -s_ml, s_ml, jnp.zeros((tm, pad), F32)], axis=1)


def _rope_tables(positions):
    t = positions.size
    tm = min(TM_PREP, t)
    pos = positions.reshape(t, 1).astype(F32)
    inv_da = 1.0 / (ROPE_THETA ** (jnp.arange(0, DA_HEAD_DIM, 2, dtype=F32) / DA_HEAD_DIM))
    inv_ml = 1.0 / (ROPE_THETA ** (jnp.arange(0, MLA_ROPE_DIM, 2, dtype=F32) / MLA_ROPE_DIM))
    inv = jnp.concatenate([inv_da, inv_ml, jnp.zeros((LANES - inv_da.size - inv_ml.size,), F32)])[None, :]
    row = pl.BlockSpec((tm, LANES), lambda i: (i, 0))
    out = jax.ShapeDtypeStruct((t, LANES), F32)
    return pl.pallas_call(
        _rope_table_kernel,
        grid=(t // tm,),
        in_specs=[pl.BlockSpec((tm, 1), lambda i: (i, 0)), pl.BlockSpec((1, LANES), lambda i: (0, 0))],
        out_specs=[row, row, row, row],
        out_shape=[out, out, out, out],
        compiler_params=_cparams(("parallel",)),
        name="rope_tables",
    )(pos, inv)


def _rotate_half_matrices():
    lane = np.arange(LANES)
    half_da, half_ml = DA_HEAD_DIM // 2, MLA_ROPE_DIM // 2
    partner_da = np.where(lane % DA_HEAD_DIM < half_da, lane + half_da, lane - half_da)
    r_da = np.zeros((LANES, LANES), np.float32)
    r_da[partner_da, lane] = 1.0
    in_rope = (lane >= MLA_NOPE_DIM) & (lane < MLA_QK_DIM)
    partner_ml = np.where(lane < MLA_NOPE_DIM + half_ml, lane + half_ml, lane - half_ml)
    r_ml = np.zeros((LANES, LANES), np.float32)
    r_ml[partner_ml[in_rope], lane[in_rope]] = 1.0
    pair = lambda r: jnp.asarray(np.kron(np.eye(2, dtype=np.float32), r), BF16)
    return pair(r_da), pair(r_ml)


def _rotary_pair(pair, rot_ref, cos, sin):
    rot = jnp.dot(pair.astype(BF16), rot_ref[...], preferred_element_type=F32)
    return pair * cos + rot * sin


def _prep_kernel(x_ref, g_ref, wqk_ref, wv_ref, wc_ref, gq_ref, wuq_ref, gkv_ref, wuk_ref, wuv_ref,
                 rda_ref, rml_ref, cda_ref, sda_ref, cml_ref, sml_ref,
                 daq_ref, dak_ref, davt_ref, mlq_ref, mlk_ref, mlvt_ref):
    tm = x_ref.shape[0]
    sub = tm // PREP_SPLIT
    pair_w = 2 * LANES
    q_scale = DA_HEAD_DIM ** -0.5 * LOG2E
    ml_scale = MLA_QK_DIM ** -0.5 * LOG2E

    for part in range(PREP_SPLIT):
        rows = slice(part * sub, (part + 1) * sub)
        hb = _rms(x_ref[rows, :], g_ref[...], NORM_EPS).astype(BF16)

        qk = jnp.dot(hb, wqk_ref[...], preferred_element_type=F32)
        cda = jnp.concatenate([cda_ref[rows, :]] * 2, axis=1)
        sda = jnp.concatenate([sda_ref[rows, :]] * 2, axis=1)
        for jp in range(DA_HEADS):
            r = _rotary_pair(qk[:, jp * pair_w:(jp + 1) * pair_w], rda_ref, cda, sda)
            if jp < DA_HEADS // 2:
                daq_ref[rows, jp * pair_w:(jp + 1) * pair_w] = (r * q_scale).astype(BF16)
            else:
                jj = jp - DA_HEADS // 2
                dak_ref[rows, jj * pair_w:(jj + 1) * pair_w] = r.astype(BF16)
        davt_ref[:, rows] = lax.dot_general(wv_ref[...], hb, (((1,), (1,)), ((), ())),
                                            preferred_element_type=F32).astype(BF16)

        c = jnp.dot(hb, wc_ref[...], preferred_element_type=F32)
        cml1, sml1 = cml_ref[rows, :], sml_ref[rows, :]
        cml = jnp.concatenate([cml1] * 2, axis=1)
        sml = jnp.concatenate([sml1] * 2, axis=1)
        cq = _rms(c[:, :MLA_Q_RANK], gq_ref[...], NORM_EPS).astype(BF16)
        qm = jnp.dot(cq, wuq_ref[...], preferred_element_type=F32)
        for jp in range(MLA_HEADS // 2):
            r = _rotary_pair(qm[:, jp * pair_w:(jp + 1) * pair_w], rml_ref, cml, sml)
            mlq_ref[rows, jp * pair_w:(jp + 1) * pair_w] = (r * ml_scale).astype(BF16)
        ckv = _rms(c[:, MLA_Q_RANK:MLA_Q_RANK + MLA_KV_RANK], gkv_ref[...], NORM_EPS).astype(BF16)
        kr = c[:, MLA_Q_RANK + MLA_KV_RANK:]
        kr_rot = jnp.dot(kr.astype(BF16), rml_ref[:LANES, :LANES], preferred_element_type=F32)
        kr = kr * cml1 + kr_rot * sml1
        kn = jnp.dot(ckv, wuk_ref[...], preferred_element_type=F32)
        for j in range(MLA_HEADS):
            mlk_ref[rows, j * LANES:(j + 1) * LANES] = (kn[:, j * LANES:(j + 1) * LANES] + kr).astype(BF16)
        mlvt_ref[:, rows] = lax.dot_general(wuv_ref[...], ckv, (((1,), (1,)), ((), ())),
                                            preferred_element_type=F32).astype(BF16)


def _prep(x2, g_mix, w, tabs, batch, seq):
    t = x2.shape[0]
    tm = min(TM_PREP, seq)
    nps = seq // tm
    row = lambda width: pl.BlockSpec((tm, width), lambda i: (i, 0))
    full = lambda a: pl.BlockSpec(a.shape, lambda i: (0,) * a.ndim)
    vt_spec = pl.BlockSpec((None, DA_WIDTH, tm), lambda i: (i // nps, 0, i % nps))
    tok = jax.ShapeDtypeStruct((t, DA_WIDTH), BF16)
    vts = jax.ShapeDtypeStruct((batch, DA_WIDTH, seq), BF16)
    weights = [w["wqk"], w["wv"], w["wc"], w["gq"], w["wuq"], w["gkv"], w["wuk"], w["wuv"],
               *_rotate_half_matrices()]
    return pl.pallas_call(
        _prep_kernel,
        grid=(t // tm,),
        in_specs=[row(D_MODEL), full(g_mix)] + [full(a) for a in weights] + [row(LANES)] * 4,
        out_specs=[row(DA_WIDTH), row(DA_WIDTH), vt_spec, row(DA_WIDTH), row(DA_WIDTH), vt_spec],
        out_shape=[tok, tok, vts, tok, tok, vts],
        compiler_params=_cparams(("parallel",)),
        name="prep",
    )(x2, g_mix, *weights, *tabs)


def _attn_kernel(*refs, n_maps, one_minus_lambda_init):
    if n_maps == 2:
        lam_ref, q_ref, k_ref, vt_ref, g_ref, o_ref = refs[:6]
    else:
        q_ref, k_ref, vt_ref, o_ref = refs[:4]
    p_ref, acc_ref, l_ref, m_ref, qm_ref = refs[-5:]
    seq = k_ref.shape[0]
    tq = q_ref.shape[0]
    tkc = ATTN_CHUNK
    n_chunks = seq // tkc
    q = q_ref[...]
    if n_maps == 2:
        lane = lax.broadcasted_iota(jnp.int32, q.shape, 1)
        zero = jnp.zeros_like(q)
        qm_ref[0] = jnp.where(lane < DA_HEAD_DIM, q, zero)
        qm_ref[1] = jnp.where(lane >= DA_HEAD_DIM, q, zero)
    else:
        qm_ref[0] = q

    def key_rows(c, rows):
        return k_ref[pl.ds(pl.multiple_of(c * tkc, rows), rows), :]

    def scores(k, mp):
        return lax.dot_general(k, qm_ref[mp], (((1,), (1,)), ((), ())), preferred_element_type=F32)

    def sublane_sums(p):
        return jnp.sum(p.reshape(p.shape[0] // SUBLANES, SUBLANES, tq), axis=0)

    for mp in range(n_maps):
        m_ref[mp] = jnp.max(scores(key_rows(0, SHIFT_KEYS), mp), axis=0, keepdims=True)
    l_ref[...] = jnp.zeros(l_ref.shape, F32)

    per_trip = math.gcd(ATTN_TRIP, n_chunks)

    def trip(t, _):
        for off in range(per_trip):
            c = t * per_trip + off
            k = key_rows(c, tkc)
            for mp in range(n_maps):
                p = jnp.exp2(scores(k, mp) - m_ref[mp])
                l_ref[mp] = l_ref[mp] + sublane_sums(p)
                p_ref[mp, pl.ds(pl.multiple_of(c * tkc, tkc), tkc), :] = p.astype(BF16)
        return 0

    lax.fori_loop(0, n_chunks // per_trip, trip, 0)

    def denominator(mp):
        return jnp.sum(l_ref[mp], axis=0, keepdims=True)

    def normalized(mp):
        return acc_ref[mp] / denominator(mp)

    if n_maps == 2:
        ratio = (lam_ref[0] * denominator(0) / denominator(1)).astype(BF16)
        group = math.gcd(VALUE_GROUP, seq)
        acc = jnp.zeros(acc_ref.shape[1:], F32)
        for k0 in range(0, seq, group):
            w = p_ref[0, k0:k0 + group, :] - ratio * p_ref[1, k0:k0 + group, :]
            acc = acc + jnp.dot(vt_ref[:, k0:k0 + group], w, preferred_element_type=F32)
        acc_ref[0] = acc / denominator(0)
    else:
        acc_ref[0] = jnp.dot(vt_ref[...], p_ref[0], preferred_element_type=F32) / denominator(0)

    def non_finite(x):
        return jnp.max(jnp.where(jnp.isfinite(x), 0.0, 1.0)) > 0.0

    @pl.when(non_finite(acc_ref[0]) | non_finite(l_ref[...]))
    def _():
        m_ref[...] = jnp.full(m_ref.shape, -jnp.inf, F32)
        l_ref[...] = jnp.zeros(l_ref.shape, F32)
        acc_ref[...] = jnp.zeros(acc_ref.shape, F32)

        def chunk(c, _):
            vt = vt_ref[:, pl.ds(pl.multiple_of(c * tkc, tkc), tkc)]
            k = key_rows(c, tkc)
            for mp in range(n_maps):
                s = scores(k, mp)
                m_old = m_ref[mp]
                m_new = jnp.maximum(m_old, jnp.max(s, axis=0, keepdims=True))
                alpha = jnp.exp2(m_old - m_new)
                p = jnp.exp2(s - m_new)
                l_ref[mp] = alpha * l_ref[mp] + sublane_sums(p)
                acc_ref[mp] = alpha * acc_ref[mp] + jnp.dot(vt, p.astype(BF16), preferred_element_type=F32)
                m_ref[mp] = m_new
            return 0

        lax.fori_loop(0, n_chunks, chunk, 0)
        if n_maps == 2:
            acc_ref[0] = normalized(0) - lam_ref[0] * normalized(1)
        else:
            acc_ref[0] = normalized(0)

    o = acc_ref[0]
    if n_maps == 2:
        ms = jnp.mean(o * o, axis=0, keepdims=True)
        o = o * lax.rsqrt(ms + SUBLN_EPS) * g_ref[...] * one_minus_lambda_init
    o_ref[...] = o.T.astype(o_ref.dtype)


def _attention(q, k, vt, batch, seq, n_maps, lam=None, subln=None, one_minus_lambda_init=1.0):
    t = q.shape[0]
    tq = min(TQ[n_maps], seq)
    nq = seq // tq
    heads = q.shape[1] // LANES
    q_spec = pl.BlockSpec((tq, LANES), lambda b, h, i: (b * nq + i, h))
    k_spec = pl.BlockSpec((seq, LANES), lambda b, h, i: (b, h))
    vt_spec = pl.BlockSpec((None, DA_V_DIM, seq), lambda b, h, i: (b, h, 0))
    in_specs = [q_spec, k_spec, vt_spec]
    args = [q, k, vt]
    if n_maps == 2:
        in_specs = [pl.BlockSpec(memory_space=pltpu.SMEM)] + in_specs + [
            pl.BlockSpec((DA_V_DIM, 1), lambda b, h, i: (0, 0))]
        args = [lam] + args + [subln]
    scratch = [pltpu.VMEM((n_maps, seq, tq), BF16),
               pltpu.VMEM((n_maps, DA_V_DIM, tq), F32),
               pltpu.VMEM((n_maps, SUBLANES, tq), F32),
               pltpu.VMEM((n_maps, 1, tq), F32),
               pltpu.VMEM((n_maps, tq, LANES), BF16)]
    return pl.pallas_call(
        functools.partial(_attn_kernel, n_maps=n_maps, one_minus_lambda_init=one_minus_lambda_init),
        grid=(batch, heads, nq),
        in_specs=in_specs,
        out_specs=q_spec,
        scratch_shapes=scratch,
        out_shape=jax.ShapeDtypeStruct((t, q.shape[1]), BF16),
        compiler_params=_cparams(("parallel", "parallel", "parallel")),
        name="diff_attn" if n_maps == 2 else "mla_attn",
    )(*args)


def _to_row_tiles(ref, val):
    groups = val.shape[0] // SUBLANES
    for j in range(D_MODEL // LANES):
        ref[:, j] = val[:, j * LANES:(j + 1) * LANES].reshape(groups, SUBLANES, LANES)


def _from_row_tiles(ref):
    rows = ref.shape[0] * SUBLANES
    return jnp.concatenate([ref[:, j].reshape(rows, LANES) for j in range(D_MODEL // LANES)], axis=1)


def _row_of(ref, r):
    return ref.at[lax.shift_right_logical(r, 3), :, pl.ds(jnp.bitwise_and(r, SUBLANES - 1), 1), :]


ROW_TILES = (D_MODEL // LANES, SUBLANES, LANES)


def _outproj_kernel(x_ref, oda_ref, oml_ref, wa_ref, wb_ref, g_ref, xo_ref, h_ref, *ht_ref):
    y = x_ref[...] + jnp.dot(oda_ref[...], wa_ref[...], preferred_element_type=F32)
    y = y + jnp.dot(oml_ref[...], wb_ref[...], preferred_element_type=F32)
    xo_ref[...] = y
    h = _rms(y, g_ref[...], NORM_EPS)
    h_ref[...] = h.astype(h_ref.dtype)
    if ht_ref:
        _to_row_tiles(ht_ref[0], h)


def _outproj(x2, o_da, o_ml, w_a, w_b, g_ffn, h_dtype, row_tiles):
    t = x2.shape[0]
    tm = min(TM_PREP, t)
    row = lambda width: pl.BlockSpec((tm, width), lambda i: (i, 0))
    full = lambda a: pl.BlockSpec(a.shape, lambda i: (0,) * a.ndim)
    out_specs = [row(D_MODEL), row(D_MODEL)]
    out_shape = [jax.ShapeDtypeStruct((t, D_MODEL), F32), jax.ShapeDtypeStruct((t, D_MODEL), h_dtype)]
    if row_tiles:
        out_specs.append(pl.BlockSpec((tm // SUBLANES,) + ROW_TILES, lambda i: (i, 0, 0, 0)))
        out_shape.append(jax.ShapeDtypeStruct((t // SUBLANES,) + ROW_TILES, F32))
    return pl.pallas_call(
        _outproj_kernel,
        grid=(t // tm,),
        in_specs=[row(D_MODEL), row(DA_WIDTH), row(MLA_WIDTH), full(w_a), full(w_b), full(g_ffn)],
        out_specs=out_specs,
        out_shape=out_shape,
        compiler_params=_cparams(("parallel",)),
        name="outproj",
    )(x2, o_da, o_ml, w_a, w_b, g_ffn)


def _ffn_kernel(x_ref, h_ref, wg_ref, wu_ref, wd_ref, o_ref):
    h = h_ref[...]
    g = jnp.dot(h, wg_ref[...], preferred_element_type=F32)
    u = jnp.dot(h, wu_ref[...], preferred_element_type=F32)
    a = (g * jax.nn.sigmoid(g) * u).astype(BF16)
    o_ref[...] = x_ref[...] + jnp.dot(a, wd_ref[...], preferred_element_type=F32)


def _dense_ffn(x2, h, wg, wu, wd):
    t = x2.shape[0]
    tm = min(TM_FFN, t)
    row = pl.BlockSpec((tm, D_MODEL), lambda i: (i, 0))
    res = lambda a: pl.BlockSpec(a.shape, lambda i: (0, 0), pipeline_mode=pl.Buffered(1))
    return pl.pallas_call(
        _ffn_kernel,
        grid=(t // tm,),
        in_specs=[row, row, res(wg), res(wu), res(wd)],
        out_specs=row,
        out_shape=jax.ShapeDtypeStruct((t, D_MODEL), F32),
        compiler_params=_cparams(("parallel",)),
        name="dense_ffn",
    )(x2, h, wg, wu, wd)


def _router_kernel(h_ref, wr_ref, gates_ref, rank_ref, cnt_ref):
    tm = h_ref.shape[0]

    @pl.when(pl.program_id(0) == 0)
    def _():
        cnt_ref[...] = jnp.zeros_like(cnt_ref)

    lane = lax.broadcasted_iota(jnp.int32, (tm, LANES), 1)
    h = h_ref[...]
    h_hi = h.astype(BF16)
    h_lo = (h - h_hi.astype(F32)).astype(BF16)
    hi = jnp.dot(h_hi, wr_ref[...], preferred_element_type=F32)
    lo = jnp.dot(h_lo, wr_ref[...], preferred_element_type=F32)
    logits = hi[:, :LANES] + hi[:, LANES:] + lo[:, :LANES]
    neg = jnp.float32(-jnp.inf)
    logits = jnp.where(lane < N_EXPERTS, logits, neg)
    v1 = jnp.max(logits, axis=1, keepdims=True)
    i1 = jnp.min(jnp.where(logits == v1, lane, LANES), axis=1, keepdims=True)
    sel1 = lane == i1
    rest = jnp.where(sel1, neg, logits)
    v2 = jnp.max(rest, axis=1, keepdims=True)
    i2 = jnp.min(jnp.where(rest == v2, lane, LANES), axis=1, keepdims=True)
    sel2 = lane == i2
    e = jnp.exp(v2 - v1)
    g1 = 1.0 / (1.0 + e)
    g2 = e / (1.0 + e)
    gates_ref[...] = jnp.where(sel1, g1, jnp.where(sel2, g2, 0.0))
    sel = jnp.where(sel1 | sel2, 1.0, 0.0)
    r_i = lax.broadcasted_iota(jnp.int32, (tm, tm), 0)
    c_i = lax.broadcasted_iota(jnp.int32, (tm, tm), 1)
    tri = jnp.where(c_i < r_i, 1.0, 0.0).astype(BF16)
    before = jnp.dot(tri, sel.astype(BF16), preferred_element_type=F32) + cnt_ref[0:1, :]
    rank_ref[...] = jnp.where(sel > 0, before, -1.0).astype(jnp.int32)
    cnt_ref[...] = cnt_ref[...] + jnp.sum(sel, axis=0, keepdims=True)


def _router(h, w_router_pad):
    t = h.shape[0]
    tm = min(TM_ROUTE, t)
    row = lambda width: pl.BlockSpec((tm, width), lambda i: (i, 0))
    return pl.pallas_call(
        _router_kernel,
        grid=(t // tm,),
        in_specs=[row(D_MODEL), pl.BlockSpec(w_router_pad.shape, lambda i: (0, 0))],
        out_specs=[row(LANES), row(LANES), pl.BlockSpec((8, LANES), lambda i: (0, 0))],
        out_shape=[jax.ShapeDtypeStruct((t, LANES), F32), jax.ShapeDtypeStruct((t, LANES), jnp.int32),
                   jax.ShapeDtypeStruct((8, LANES), F32)],
        compiler_params=_cparams(("arbitrary",)),
        name="router",
    )(h, w_router_pad)


def _scatter_kernel(ends_ref, dest_ref, h_ref, xs_ref, zero_ref, sem):
    tm = h_ref.shape[0] * SUBLANES
    tile_groups = TM_EXP // SUBLANES

    def row_copy(g, s, k):
        return pltpu.make_async_copy(h_ref.at[g, :, pl.ds(s, 1), :],
                                     _row_of(xs_ref, dest_ref[0, 2 * (SUBLANES * g + s) + k]), sem)

    @pl.when(pl.program_id(0) == 0)
    def _():
        zero_ref[...] = jnp.zeros_like(zero_ref)
        copies = []
        for e in range(N_EXPERTS):
            start = pl.multiple_of(jnp.maximum(ends_ref[e] - TM_EXP, 0) // SUBLANES, tile_groups)
            cp = pltpu.make_async_copy(zero_ref, xs_ref.at[pl.ds(start, tile_groups)], sem)
            cp.start()
            copies.append(cp)
        for cp in copies:
            cp.wait()

        def zero_tail(i, _):
            cp = pltpu.make_async_copy(
                zero_ref, xs_ref.at[pl.ds(pl.multiple_of(i * tile_groups, tile_groups), tile_groups)], sem)
            cp.start()
            cp.wait()
            return 0

        lax.fori_loop(ends_ref[N_EXPERTS - 1] // TM_EXP, xs_ref.shape[0] // tile_groups, zero_tail, 0)

    def issue(g, _):
        for s in range(SUBLANES):
            row_copy(g, s, 0).start(priority=0)
            row_copy(g, s, 1).start(priority=1)
        return 0

    lax.fori_loop(0, tm // SUBLANES, issue, 0)

    def drain(g, _):
        for s in range(SUBLANES):
            row_copy(g, s, 0).wait()
            row_copy(g, s, 1).wait()
        return 0

    lax.fori_loop(0, tm // SUBLANES, drain, 0)


def _scatter_rows(h, dest2, ends, n_rows):
    t = h.shape[0] * SUBLANES
    tm = min(TM_ROW, t)
    nt = t // tm
    dest3 = dest2.reshape(nt, 1, 2 * tm)
    grid_spec = pltpu.PrefetchScalarGridSpec(
        num_scalar_prefetch=1,
        grid=(nt,),
        in_specs=[pl.BlockSpec((None, 1, 2 * tm), lambda i, ends: (i, 0, 0), memory_space=pltpu.SMEM),
                  pl.BlockSpec((tm // SUBLANES,) + ROW_TILES, lambda i, ends: (i, 0, 0, 0))],
        out_specs=pl.BlockSpec(memory_space=pl.ANY),
        scratch_shapes=[pltpu.VMEM((TM_EXP // SUBLANES,) + ROW_TILES, F32), pltpu.SemaphoreType.DMA(())],
    )
    return pl.pallas_call(
        _scatter_kernel,
        grid_spec=grid_spec,
        out_shape=jax.ShapeDtypeStruct((n_rows // SUBLANES,) + ROW_TILES, F32),
        compiler_params=_cparams(("arbitrary",)),
        name="scatter_rows",
    )(ends, dest3, h)


def _expert_kernel(te_ref, na_ref, xs_ref, wg_ref, wu_ref, wd_ref, ys_ref, acc_ref):
    i = pl.program_id(0)
    j = pl.program_id(1)

    @pl.when((i == 0) & (j == 0))
    def _():
        acc_ref[...] = jnp.zeros_like(acc_ref)

    @pl.when(i < na_ref[0])
    def _():
        xb = _from_row_tiles(xs_ref).astype(BF16)
        g = jnp.dot(xb, wg_ref[...], preferred_element_type=F32)
        u = jnp.dot(xb, wu_ref[...], preferred_element_type=F32)
        a = (g * jax.nn.sigmoid(g) * u).astype(BF16)
        acc = jnp.where(j > 0, acc_ref[...], 0.0) + jnp.dot(a, wd_ref[...], preferred_element_type=F32)
        acc_ref[...] = acc
        _to_row_tiles(ys_ref, acc)

    @pl.when(i >= na_ref[0])
    def _():
        ys_ref[...] = jnp.zeros_like(ys_ref)


def _expert_ffn(xs, tile_expert, n_active, wg, wu, wd):
    n_rows = xs.shape[0] * SUBLANES
    ff = wg.shape[2]
    tf = ff // 2
    nj = ff // tf
    nt = n_rows // TM_EXP

    def wj(i, j, na):
        return jnp.where(i < na[0], j, nj - 1)

    row = pl.BlockSpec((TM_EXP // SUBLANES,) + ROW_TILES, lambda i, j, te, na: (i, 0, 0, 0))
    grid_spec = pltpu.PrefetchScalarGridSpec(
        num_scalar_prefetch=2,
        grid=(nt, nj),
        in_specs=[row,
                  pl.BlockSpec((None, D_MODEL, tf), lambda i, j, te, na: (te[i], 0, wj(i, j, na))),
                  pl.BlockSpec((None, D_MODEL, tf), lambda i, j, te, na: (te[i], 0, wj(i, j, na))),
                  pl.BlockSpec((None, tf, D_MODEL), lambda i, j, te, na: (te[i], wj(i, j, na), 0))],
        out_specs=row,
        scratch_shapes=[pltpu.VMEM((TM_EXP, D_MODEL), F32)],
    )
    return pl.pallas_call(
        _expert_kernel,
        grid_spec=grid_spec,
        out_shape=jax.ShapeDtypeStruct((n_rows // SUBLANES,) + ROW_TILES, F32),
        compiler_params=_cparams(("arbitrary", "arbitrary")),
        name="expert_ffn",
    )(tile_expert, n_active, xs, wg, wu, wd)


def _combine_kernel(dest_ref, x_ref, g12_ref, gfin_ref, ys_ref, o_ref, buf_ref, sem, *, final):
    tm = x_ref.shape[0]

    def row_copy(g, s, k):
        return pltpu.make_async_copy(_row_of(ys_ref, dest_ref[0, 2 * (SUBLANES * g + s) + k]),
                                     buf_ref.at[k, g, :, pl.ds(s, 1), :], sem)

    def issue(g, _):
        for s in range(SUBLANES):
            row_copy(g, s, 0).start(priority=0)
            row_copy(g, s, 1).start(priority=1)
        return 0

    lax.fori_loop(0, tm // SUBLANES, issue, 0)

    def drain(g, _):
        for s in range(SUBLANES):
            row_copy(g, s, 0).wait()
            row_copy(g, s, 1).wait()
        return 0

    lax.fori_loop(0, tm // SUBLANES, drain, 0)
    g12 = g12_ref[...]
    y = x_ref[...] + g12[:, 0:1] * _from_row_tiles(buf_ref.at[0]) + g12[:, 1:2] * _from_row_tiles(buf_ref.at[1])
    o_ref[...] = _rms(y, gfin_ref[...], NORM_EPS) if final else y


def _combine(x2, ys, dest2, g12, g_final, final):
    t = x2.shape[0]
    tm = min(TM_ROW, t)
    nt = t // tm
    dest3 = dest2.reshape(nt, 1, 2 * tm)
    row = pl.BlockSpec((tm, D_MODEL), lambda i: (i, 0))
    return pl.pallas_call(
        functools.partial(_combine_kernel, final=final),
        grid=(nt,),
        in_specs=[pl.BlockSpec((None, 1, 2 * tm), lambda i: (i, 0, 0), memory_space=pltpu.SMEM),
                  row, pl.BlockSpec((tm, 2), lambda i: (i, 0)),
                  pl.BlockSpec((1, D_MODEL), lambda i: (0, 0)),
                  pl.BlockSpec(memory_space=pl.ANY)],
        out_specs=row,
        out_shape=jax.ShapeDtypeStruct((t, D_MODEL), F32),
        scratch_shapes=[pltpu.VMEM((2, tm // SUBLANES) + ROW_TILES, F32), pltpu.SemaphoreType.DMA(())],
        compiler_params=_cparams(("arbitrary",)),
        name="combine",
    )(dest3, x2, g12, g_final, ys)


def _final_norm_kernel(x_ref, g_ref, o_ref):
    o_ref[...] = _rms(x_ref[...], g_ref[...], NORM_EPS)


def _final_norm(x2, g):
    t = x2.shape[0]
    tm = min(TM_PREP, t)
    row = pl.BlockSpec((tm, D_MODEL), lambda i: (i, 0))
    return pl.pallas_call(
        _final_norm_kernel, grid=(t // tm,),
        in_specs=[row, pl.BlockSpec((1, D_MODEL), lambda i: (0, 0))], out_specs=row,
        out_shape=jax.ShapeDtypeStruct((t, D_MODEL), F32),
        compiler_params=_cparams(("parallel",)), name="final_norm",
    )(x2, g)


def _moe_layer(x2, h, h_tiles, router, wg, wu, wd, g_final, final):
    t = x2.shape[0]
    w_router_pad = jnp.zeros((D_MODEL, LANES), F32).at[:, :N_EXPERTS].set(router.astype(F32))
    w_hi = w_router_pad.astype(BF16)
    w_lo = (w_router_pad - w_hi.astype(F32)).astype(BF16)
    gates, rank, cnt = _router(h, jnp.concatenate([w_hi, w_lo], axis=1))
    counts = cnt[0, :N_EXPERTS].astype(jnp.int32)
    padded = ((counts + TM_EXP - 1) // TM_EXP) * TM_EXP
    ends = jnp.cumsum(padded)
    starts = ends - padded
    n_rows = TOP_K * t + N_EXPERTS * TM_EXP
    nt = n_rows // TM_EXP
    n_active = (ends[-1] // TM_EXP).astype(jnp.int32)
    tile_start = jnp.minimum(jnp.arange(nt, dtype=jnp.int32), n_active - 1) * TM_EXP
    tile_expert = jnp.minimum(jnp.sum(ends[None, :] <= tile_start[:, None], axis=1), N_EXPERTS - 1).astype(jnp.int32)
    rank8 = rank[:, :N_EXPERTS]
    sel8 = rank8 >= 0
    dest8 = starts[None, :] + rank8
    e_idx = jnp.arange(N_EXPERTS, dtype=jnp.int32)[None, :]
    first = jnp.min(jnp.where(sel8, e_idx, N_EXPERTS), axis=1, keepdims=True)
    second = jnp.max(jnp.where(sel8, e_idx, -1), axis=1, keepdims=True)
    pick = lambda a, e: jnp.sum(jnp.where(e_idx == e, a, 0), axis=1, keepdims=True)
    dest2 = jnp.concatenate([pick(dest8, first), pick(dest8, second)], axis=1).astype(jnp.int32)
    gates8 = gates[:, :N_EXPERTS]
    g12 = jnp.concatenate([pick(gates8, first), pick(gates8, second)], axis=1)
    xs = _scatter_rows(h_tiles, dest2, ends.astype(jnp.int32), n_rows)
    ys = _expert_ffn(xs, tile_expert, n_active.reshape(1), wg, wu, wd)
    return _combine(x2, ys, dest2, g12, g_final, final)


def kernel(x, positions, w_in, w_out, norm_mix, norm_ffn, da_lambda_q1, da_lambda_k1, da_lambda_q2,
           da_lambda_k2, da_subln, mla_q_norm, mla_w_uq, mla_kv_norm, mla_w_ukv, ffn_w_gate, ffn_w_up,
           ffn_w_down, moe_router, moe_w_gate, moe_w_up, moe_w_down, norm_final):
    batch, seq, d = x.shape
    depth = w_in.shape[0]
    t = batch * seq
    x2 = x.reshape(t, d).astype(F32)
    tabs = _rope_tables(positions)

    n_qk = 2 * DA_HEADS * 2 * DA_HEAD_DIM
    n_v = DA_HEADS * DA_V_DIM
    c0 = n_qk + n_v
    kv_per_head = MLA_NOPE_DIM + MLA_V_DIM

    for l in range(depth):
        wl = w_in[l]
        w_kr = jnp.zeros((d, LANES), F32).at[:, MLA_NOPE_DIM:MLA_QK_DIM].set(
            wl[:, c0 + MLA_Q_RANK + MLA_KV_RANK:])
        wc = jnp.concatenate([wl[:, c0:c0 + MLA_Q_RANK + MLA_KV_RANK], w_kr], axis=1)
        wuq = jnp.pad(mla_w_uq[l].reshape(MLA_Q_RANK, MLA_HEADS, MLA_QK_DIM),
                      ((0, 0), (0, 0), (0, LANES - MLA_QK_DIM))).reshape(MLA_Q_RANK, MLA_HEADS * LANES)
        wukv = mla_w_ukv[l].reshape(MLA_KV_RANK, MLA_HEADS, kv_per_head)
        wuk = jnp.pad(wukv[:, :, :MLA_NOPE_DIM],
                      ((0, 0), (0, 0), (0, LANES - MLA_NOPE_DIM))).reshape(MLA_KV_RANK, MLA_HEADS * LANES)
        wuv = wukv[:, :, MLA_NOPE_DIM:].reshape(MLA_KV_RANK, MLA_HEADS * MLA_V_DIM)
        w = dict(wqk=wl[:, :n_qk].astype(BF16), wv=wl[:, n_qk:c0].T.astype(BF16), wc=wc.astype(BF16),
                 gq=mla_q_norm[l].reshape(1, -1).astype(F32), wuq=wuq.astype(BF16),
                 gkv=mla_kv_norm[l].reshape(1, -1).astype(F32), wuk=wuk.astype(BF16), wuv=wuv.T.astype(BF16))

        da_q, da_k, da_vt, ml_q, ml_k, ml_vt = _prep(
            x2, norm_mix[l].reshape(1, d).astype(F32), w, tabs, batch, seq)

        lambda_init = 0.8 - 0.6 * float(np.exp(-0.3 * l))
        lam = (jnp.exp(jnp.sum(da_lambda_q1[l].astype(F32) * da_lambda_k1[l].astype(F32)))
               - jnp.exp(jnp.sum(da_lambda_q2[l].astype(F32) * da_lambda_k2[l].astype(F32)))
               + lambda_init).reshape(1).astype(F32)
        o_da = _attention(da_q, da_k, da_vt, batch, seq, 2, lam=lam,
                          subln=da_subln[l].reshape(DA_V_DIM, 1).astype(F32),
                          one_minus_lambda_init=1.0 - lambda_init)
        o_ml = _attention(ml_q, ml_k, ml_vt, batch, seq, 1)

        is_moe = l % 2 == 1
        wo = w_out[l].astype(BF16)
        x2, h, *h_tiles = _outproj(x2, o_da, o_ml, wo[:DA_WIDTH], wo[DA_WIDTH:],
                                   norm_ffn[l].reshape(1, d).astype(F32), F32 if is_moe else BF16, is_moe)
        i = l // 2
        last = l == depth - 1
        if not is_moe:
            x2 = _dense_ffn(x2, h, ffn_w_gate[i].astype(BF16), ffn_w_up[i].astype(BF16),
                            ffn_w_down[i].astype(BF16))
            if last:
                x2 = _final_norm(x2, norm_final.reshape(1, d).astype(F32))
        else:
            x2 = _moe_layer(x2, h, h_tiles[0], moe_router[i], moe_w_gate[i].astype(BF16),
                            moe_w_up[i].astype(BF16), moe_w_down[i].astype(BF16),
                            norm_final.reshape(1, d).astype(F32), last)
    return x2.reshape(batch, seq, d).astype(x.dtype)
```

```python
import functools
import math

import numpy as np
import jax
import jax.numpy as jnp
from jax import lax
from jax.experimental import pallas as pl
from jax.experimental.pallas import tpu as pltpu

D_MODEL = 1024
ROPE_THETA = 10000.0
NORM_EPS = 1e-6
SUBLN_EPS = 1e-5
DA_HEAD_DIM = 64
DA_V_DIM = 128
DA_HEADS = 4
DA_WIDTH = 512
MLA_HEADS = 4
MLA_V_DIM = 128
MLA_WIDTH = 512
MLA_Q_RANK = 256
MLA_KV_RANK = 128
MLA_NOPE_DIM = 64
MLA_ROPE_DIM = 32
MLA_QK_DIM = 96
N_EXPERTS = 8
TOP_K = 2
LOG2E = 1.4426950408889634

LANES = 128
SUBLANES = 8
VMEM_LIMIT = 56 * 1024 * 1024

TM_PREP = 512
PREP_SPLIT = 1
TQ = {1: 1024, 2: 1024}
ATTN_CHUNK = 256
SHIFT_KEYS = 64
VALUE_GROUP = 2048
ATTN_TRIP = 16
TM_FFN = 512
TM_EXP = 512
TM_ROW = 512
TM_ROUTE = 512
ROW_DMA_UNROLL = 8

BF16 = jnp.bfloat16
F32 = jnp.float32


def _cparams(sem):
    return pltpu.CompilerParams(dimension_semantics=sem, vmem_limit_bytes=VMEM_LIMIT)


def _rms(x, g, eps):
    return x * lax.rsqrt(jnp.mean(x * x, axis=-1, keepdims=True) + eps) * g


def _rope_table_kernel(pos_ref, inv_ref, cos_da_ref, sin_da_ref, cos_ml_ref, sin_ml_ref):
    ang = pos_ref[...] * inv_ref[...]
    c, s = jnp.cos(ang), jnp.sin(ang)
    h_da, h_ml = DA_HEAD_DIM // 2, MLA_ROPE_DIM // 2
    c_da, s_da = c[:, :h_da], s[:, :h_da]
    c_ml, s_ml = c[:, h_da:h_da + h_ml], s[:, h_da:h_da + h_ml]
    tm = ang.shape[0]
    cos_da_ref[...] = jnp.concatenate([c_da] * (LANES // h_da), axis=1)
    sin_da_ref[...] = jnp.concatenate([-s_da, s_da] * (LANES // DA_HEAD_DIM), axis=1)
    pad = LANES - MLA_QK_DIM
    cos_ml_ref[...] = jnp.concatenate(
        [jnp.ones((tm, MLA_NOPE_DIM), F32), c_ml, c_ml, jnp.ones((tm, pad), F32)], axis=1)
    sin_ml_ref[...] = jnp.concatenate(
        [jnp.zeros((tm, MLA_NOPE_DIM), F32), -s_ml, s_ml, jnp.zeros((tm, pad), F32)], axis=1)


def _rope_tables(positions):
    t = positions.size
    tm = min(TM_PREP, t)
    pos = positions.reshape(t, 1).astype(F32)
    inv_da = 1.0 / (ROPE_THETA ** (jnp.arange(0, DA_HEAD_DIM, 2, dtype=F32) / DA_HEAD_DIM))
    inv_ml = 1.0 / (ROPE_THETA ** (jnp.arange(0, MLA_ROPE_DIM, 2, dtype=F32) / MLA_ROPE_DIM))
    inv = jnp.concatenate([inv_da, inv_ml, jnp.zeros((LANES - inv_da.size - inv_ml.size,), F32)])[None, :]
    row = pl.BlockSpec((tm, LANES), lambda i: (i, 0))
    out = jax.ShapeDtypeStruct((t, LANES), F32)
    return pl.pallas_call(
        _rope_table_kernel,
        grid=(t // tm,),
        in_specs=[pl.BlockSpec((tm, 1), lambda i: (i, 0)), pl.BlockSpec((1, LANES), lambda i: (0, 0))],
        out_specs=[row, row, row, row],
        out_shape=[out, out, out, out],
        compiler_params=_cparams(("parallel",)),
        name="rope_tables",
    )(pos, inv)


def _rotate_half_matrices():
    lane = np.arange(LANES)
    half_da, half_ml = DA_HEAD_DIM // 2, MLA_ROPE_DIM // 2
    partner_da = np.where(lane % DA_HEAD_DIM < half_da, lane + half_da, lane - half_da)
    r_da = np.zeros((LANES, LANES), np.float32)
    r_da[partner_da, lane] = 1.0
    in_rope = (lane >= MLA_NOPE_DIM) & (lane < MLA_QK_DIM)
    partner_ml = np.where(lane < MLA_NOPE_DIM + half_ml, lane + half_ml, lane - half_ml)
    r_ml = np.zeros((LANES, LANES), np.float32)
    r_ml[partner_ml[in_rope], lane[in_rope]] = 1.0
    pair = lambda r: jnp.asarray(np.kron(np.eye(2, dtype=np.float32), r), BF16)
    return pair(r_da), pair(r_ml)


def _rotary_pair(pair, rot_ref, cos, sin):
    rot = jnp.dot(pair.astype(BF16), rot_ref[...], preferred_element_type=F32)
    return pair * cos + rot * sin


def _prep_kernel(x_ref, g_ref, wqk_ref, wv_ref, wc_ref, gq_ref, wuq_ref, gkv_ref, wuk_ref, wuv_ref,
                 rda_ref, rml_ref, cda_ref, sda_ref, cml_ref, sml_ref,
                 daq_ref, dak_ref, davt_ref, mlq_ref, mlk_ref, mlvt_ref):
    tm = x_ref.shape[0]
    sub = tm // PREP_SPLIT
    pair_w = 2 * LANES
    q_scale = DA_HEAD_DIM ** -0.5 * LOG2E
    ml_scale = MLA_QK_DIM ** -0.5 * LOG2E

    for part in range(PREP_SPLIT):
        rows = slice(part * sub, (part + 1) * sub)
        hb = _rms(x_ref[rows, :], g_ref[...], NORM_EPS).astype(BF16)

        qk = jnp.dot(hb, wqk_ref[...], preferred_element_type=F32)
        cda = jnp.concatenate([cda_ref[rows, :]] * 2, axis=1)
        sda = jnp.concatenate([sda_ref[rows, :]] * 2, axis=1)
        for jp in range(DA_HEADS):
            r = _rotary_pair(qk[:, jp * pair_w:(jp + 1) * pair_w], rda_ref, cda, sda)
            if jp < DA_HEADS // 2:
                daq_ref[rows, jp * pair_w:(jp + 1) * pair_w] = (r * q_scale).astype(BF16)
            else:
                jj = jp - DA_HEADS // 2
                dak_ref[rows, jj * pair_w:(jj + 1) * pair_w] = r.astype(BF16)
        davt_ref[:, rows] = lax.dot_general(wv_ref[...], hb, (((1,), (1,)), ((), ())),
                                            preferred_element_type=F32).astype(BF16)

        c = jnp.dot(hb, wc_ref[...], preferred_element_type=F32)
        cml1, sml1 = cml_ref[rows, :], sml_ref[rows, :]
        cml = jnp.concatenate([cml1] * 2, axis=1)
        sml = jnp.concatenate([sml1] * 2, axis=1)
        cq = _rms(c[:, :MLA_Q_RANK], gq_ref[...], NORM_EPS).astype(BF16)
        qm = jnp.dot(cq, wuq_ref[...], preferred_element_type=F32)
        for jp in range(MLA_HEADS // 2):
            r = _rotary_pair(qm[:, jp * pair_w:(jp + 1) * pair_w], rml_ref, cml, sml)
            mlq_ref[rows, jp * pair_w:(jp + 1) * pair_w] = (r * ml_scale).astype(BF16)
        ckv = _rms(c[:, MLA_Q_RANK:MLA_Q_RANK + MLA_KV_RANK], gkv_ref[...], NORM_EPS).astype(BF16)
        kr = c[:, MLA_Q_RANK + MLA_KV_RANK:]
        kr_rot = jnp.dot(kr.astype(BF16), rml_ref[:LANES, :LANES], preferred_element_type=F32)
        kr = kr * cml1 + kr_rot * sml1
        kn = jnp.dot(ckv, wuk_ref[...], preferred_element_type=F32)
        for j in range(MLA_HEADS):
            mlk_ref[rows, j * LANES:(j + 1) * LANES] = (kn[:, j * LANES:(j + 1) * LANES] + kr).astype(BF16)
        mlvt_ref[:, rows] = lax.dot_general(wuv_ref[...], ckv, (((1,), (1,)), ((), ())),
                                            preferred_element_type=F32).astype(BF16)


def _prep(x2, g_mix, w, tabs, batch, seq):
    t = x2.shape[0]
    tm = min(TM_PREP, seq)
    nps = seq // tm
    row = lambda width: pl.BlockSpec((tm, width), lambda i: (i, 0))
    full = lambda a: pl.BlockSpec(a.shape, lambda i: (0,) * a.ndim)
    vt_spec = pl.BlockSpec((None, DA_WIDTH, tm), lambda i: (i // nps, 0, i % nps))
    tok = jax.ShapeDtypeStruct((t, DA_WIDTH), BF16)
    vts = jax.ShapeDtypeStruct((batch, DA_WIDTH, seq), BF16)
    weights = [w["wqk"], w["wv"], w["wc"], w["gq"], w["wuq"], w["gkv"], w["wuk"], w["wuv"],
               *_rotate_half_matrices()]
    return pl.pallas_call(
        _prep_kernel,
        grid=(t // tm,),
        in_specs=[row(D_MODEL), full(g_mix)] + [full(a) for a in weights] + [row(LANES)] * 4,
        out_specs=[row(DA_WIDTH), row(DA_WIDTH), vt_spec, row(DA_WIDTH), row(DA_WIDTH), vt_spec],
        out_shape=[tok, tok, vts, tok, tok, vts],
        compiler_params=_cparams(("parallel",)),
        name="prep",
    )(x2, g_mix, *weights, *tabs)


def _attn_kernel(*refs, n_maps, one_minus_lambda_init):
    if n_maps == 2:
        lam_ref, q_ref, k_ref, vt_ref, g_ref, o_ref = refs[:6]
    else:
        q_ref, k_ref, vt_ref, o_ref = refs[:4]
    p_ref, acc_ref, l_ref, m_ref, qm_ref = refs[-5:]
    seq = k_ref.shape[0]
    tq = q_ref.shape[0]
    tkc = ATTN_CHUNK
    n_chunks = seq // tkc
    q = q_ref[...]
    if n_maps == 2:
        lane = lax.broadcasted_iota(jnp.int32, q.shape, 1)
        zero = jnp.zeros_like(q)
        qm_ref[0] = jnp.where(lane < DA_HEAD_DIM, q, zero)
        qm_ref[1] = jnp.where(lane >= DA_HEAD_DIM, q, zero)
    else:
        qm_ref[0] = q

    def key_rows(c, rows):
        return k_ref[pl.ds(pl.multiple_of(c * tkc, rows), rows), :]

    def scores(k, mp):
        return lax.dot_general(k, qm_ref[mp], (((1,), (1,)), ((), ())), preferred_element_type=F32)

    def sublane_sums(p):
        return jnp.sum(p.reshape(p.shape[0] // SUBLANES, SUBLANES, tq), axis=0)

    for mp in range(n_maps):
        m_ref[mp] = jnp.max(scores(key_rows(0, SHIFT_KEYS), mp), axis=0, keepdims=True)
    l_ref[...] = jnp.zeros(l_ref.shape, F32)

    per_trip = math.gcd(ATTN_TRIP, n_chunks)

    def trip(t, _):
        for off in range(per_trip):
            c = t * per_trip + off
            k = key_rows(c, tkc)
            for mp in range(n_maps):
                p = jnp.exp2(scores(k, mp) - m_ref[mp])
                l_ref[mp] = l_ref[mp] + sublane_sums(p)
                p_ref[mp, pl.ds(pl.multiple_of(c * tkc, tkc), tkc), :] = p.astype(BF16)
        return 0

    lax.fori_loop(0, n_chunks // per_trip, trip, 0)

    def denominator(mp):
        return jnp.sum(l_ref[mp], axis=0, keepdims=True)

    def normalized(mp):
        return acc_ref[mp] / denominator(mp)

    if n_maps == 2:
        ratio = (lam_ref[0] * denominator(0) / denominator(1)).astype(BF16)
        group = math.gcd(VALUE_GROUP, seq)
        acc = jnp.zeros(acc_ref.shape[1:], F32)
        for k0 in range(0, seq, group):
            w = p_ref[0, k0:k0 + group, :] - ratio * p_ref[1, k0:k0 + group, :]
            acc = acc + jnp.dot(vt_ref[:, k0:k0 + group], w, preferred_element_type=F32)
        acc_ref[0] = acc / denominator(0)
    else:
        acc_ref[0] = jnp.dot(vt_ref[...], p_ref[0], preferred_element_type=F32) / denominator(0)

    def non_finite(x):
        return jnp.max(jnp.where(jnp.isfinite(x), 0.0, 1.0)) > 0.0

    @pl.when(non_finite(acc_ref[0]) | non_finite(l_ref[...]))
    def _():
        m_ref[...] = jnp.full(m_ref.shape, -jnp.inf, F32)
        l_ref[...] = jnp.zeros(l_ref.shape, F32)
        acc_ref[...] = jnp.zeros(acc_ref.shape, F32)

        def chunk(c, _):
            vt = vt_ref[:, pl.ds(pl.multiple_of(c * tkc, tkc), tkc)]
            k = key_rows(c, tkc)
            for mp in range(n_maps):
                s = scores(k, mp)
                m_old = m_ref[mp]
                m_new = jnp.maximum(m_old, jnp.max(s, axis=0, keepdims=True))
                alpha = jnp.exp2(m_old - m_new)
                p = jnp.exp2(s - m_new)
                l_ref[mp] = alpha * l_ref[mp] + sublane_sums(p)
                acc_ref[mp] = alpha * acc_ref[mp] + jnp.dot(vt, p.astype(BF16), preferred_element_type=F32)
                m_ref[mp] = m_new
            return 0

        lax.fori_loop(0, n_chunks, chunk, 0)
        if n_maps == 2:
            acc_ref[0] = normalized(0) - lam_ref[0] * normalized(1)
        else:
            acc_ref[0] = normalized(0)

    o = acc_ref[0]
    if n_maps == 2:
        ms = jnp.mean(o * o, axis=0, keepdims=True)
        o = o * lax.rsqrt(ms + SUBLN_EPS) * g_ref[...] * one_minus_lambda_init
    o_ref[...] = o.T.astype(o_ref.dtype)


def _attention(q, k, vt, batch, seq, n_maps, lam=None, subln=None, one_minus_lambda_init=1.0):
    t = q.shape[0]
    tq = min(TQ[n_maps], seq)
    nq = seq // tq
    heads = q.shape[1] // LANES
    q_spec = pl.BlockSpec((tq, LANES), lambda b, h, i: (b * nq + i, h))
    k_spec = pl.BlockSpec((seq, LANES), lambda b, h, i: (b, h))
    vt_spec = pl.BlockSpec((None, DA_V_DIM, seq), lambda b, h, i: (b, h, 0))
    in_specs = [q_spec, k_spec, vt_spec]
    args = [q, k, vt]
    if n_maps == 2:
        in_specs = [pl.BlockSpec(memory_space=pltpu.SMEM)] + in_specs + [
            pl.BlockSpec((DA_V_DIM, 1), lambda b, h, i: (0, 0))]
        args = [lam] + args + [subln]
    scratch = [pltpu.VMEM((n_maps, seq, tq), BF16),
               pltpu.VMEM((n_maps, DA_V_DIM, tq), F32),
               pltpu.VMEM((n_maps, SUBLANES, tq), F32),
               pltpu.VMEM((n_maps, 1, tq), F32),
               pltpu.VMEM((n_maps, tq, LANES), BF16)]
    return pl.pallas_call(
        functools.partial(_attn_kernel, n_maps=n_maps, one_minus_lambda_init=one_minus_lambda_init),
        grid=(batch, heads, nq),
        in_specs=in_specs,
        out_specs=q_spec,
        scratch_shapes=scratch,
        out_shape=jax.ShapeDtypeStruct((t, q.shape[1]), BF16),
        compiler_params=_cparams(("parallel", "parallel", "parallel")),
        name="diff_attn" if n_maps == 2 else "mla_attn",
    )(*args)


def _to_row_tiles(ref, val):
    groups = val.shape[0] // SUBLANES
    for j in range(D_MODEL // LANES):
        ref[:, j] = val[:, j * LANES:(j + 1) * LANES].reshape(groups, SUBLANES, LANES)


def _from_row_tiles(ref):
    rows = ref.shape[0] * SUBLANES
    return jnp.concatenate([ref[:, j].reshape(rows, LANES) for j in range(D_MODEL // LANES)], axis=1)


def _row_of(ref, r):
    return ref.at[lax.shift_right_logical(r, 3), :, pl.ds(jnp.bitwise_and(r, SUBLANES - 1), 1), :]


ROW_TILES = (D_MODEL // LANES, SUBLANES, LANES)


def _outproj_kernel(x_ref, oda_ref, oml_ref, wa_ref, wb_ref, g_ref, xo_ref, h_ref, *ht_ref):
    y = x_ref[...] + jnp.dot(oda_ref[...], wa_ref[...], preferred_element_type=F32)
    y = y + jnp.dot(oml_ref[...], wb_ref[...], preferred_element_type=F32)
    xo_ref[...] = y
    h = _rms(y, g_ref[...], NORM_EPS)
    h_ref[...] = h.astype(h_ref.dtype)
    if ht_ref:
        _to_row_tiles(ht_ref[0], h)


def _outproj(x2, o_da, o_ml, w_a, w_b, g_ffn, h_dtype, row_tiles):
    t = x2.shape[0]
    tm = min(TM_PREP, t)
    row = lambda width: pl.BlockSpec((tm, width), lambda i: (i, 0))
    full = lambda a: pl.BlockSpec(a.shape, lambda i: (0,) * a.ndim)
    out_specs = [row(D_MODEL), row(D_MODEL)]
    out_shape = [jax.ShapeDtypeStruct((t, D_MODEL), F32), jax.ShapeDtypeStruct((t, D_MODEL), h_dtype)]
    if row_tiles:
        out_specs.append(pl.BlockSpec((tm // SUBLANES,) + ROW_TILES, lambda i: (i, 0, 0, 0)))
        out_shape.append(jax.ShapeDtypeStruct((t // SUBLANES,) + ROW_TILES, F32))
    return pl.pallas_call(
        _outproj_kernel,
        grid=(t // tm,),
        in_specs=[row(D_MODEL), row(DA_WIDTH), row(MLA_WIDTH), full(w_a), full(w_b), full(g_ffn)],
        out_specs=out_specs,
        out_shape=out_shape,
        compiler_params=_cparams(("parallel",)),
        name="outproj",
    )(x2, o_da, o_ml, w_a, w_b, g_ffn)


def _ffn_kernel(x_ref, h_ref, wg_ref, wu_ref, wd_ref, o_ref):
    h = h_ref[...]
    g = jnp.dot(h, wg_ref[...], preferred_element_type=F32)
    u = jnp.dot(h, wu_ref[...], preferred_element_type=F32)
    a = (g * jax.nn.sigmoid(g) * u).astype(BF16)
    o_ref[...] = x_ref[...] + jnp.dot(a, wd_ref[...], preferred_element_type=F32)


def _dense_ffn(x2, h, wg, wu, wd):
    t = x2.shape[0]
    tm = min(TM_FFN, t)
    row = pl.BlockSpec((tm, D_MODEL), lambda i: (i, 0))
    res = lambda a: pl.BlockSpec(a.shape, lambda i: (0, 0), pipeline_mode=pl.Buffered(1))
    return pl.pallas_call(
        _ffn_kernel,
        grid=(t // tm,),
        in_specs=[row, row, res(wg), res(wu), res(wd)],
        out_specs=row,
        out_shape=jax.ShapeDtypeStruct((t, D_MODEL), F32),
        compiler_params=_cparams(("parallel",)),
        name="dense_ffn",
    )(x2, h, wg, wu, wd)


def _router_kernel(h_ref, wr_ref, gates_ref, rank_ref, cnt_ref):
    tm = h_ref.shape[0]

    @pl.when(pl.program_id(0) == 0)
    def _():
        cnt_ref[...] = jnp.zeros_like(cnt_ref)

    lane = lax.broadcasted_iota(jnp.int32, (tm, LANES), 1)
    h = h_ref[...]
    h_hi = h.astype(BF16)
    h_lo = (h - h_hi.astype(F32)).astype(BF16)
    hi = jnp.dot(h_hi, wr_ref[...], preferred_element_type=F32)
    lo = jnp.dot(h_lo, wr_ref[...], preferred_element_type=F32)
    logits = hi[:, :LANES] + hi[:, LANES:] + lo[:, :LANES]
    neg = jnp.float32(-jnp.inf)
    logits = jnp.where(lane < N_EXPERTS, logits, neg)
    v1 = jnp.max(logits, axis=1, keepdims=True)
    i1 = jnp.min(jnp.where(logits == v1, lane, LANES), axis=1, keepdims=True)
    sel1 = lane == i1
    rest = jnp.where(sel1, neg, logits)
    v2 = jnp.max(rest, axis=1, keepdims=True)
    i2 = jnp.min(jnp.where(rest == v2, lane, LANES), axis=1, keepdims=True)
    sel2 = lane == i2
    e = jnp.exp(v2 - v1)
    g1 = 1.0 / (1.0 + e)
    g2 = e / (1.0 + e)
    gates_ref[...] = jnp.where(sel1, g1, jnp.where(sel2, g2, 0.0))
    sel = jnp.where(sel1 | sel2, 1.0, 0.0)
    r_i = lax.broadcasted_iota(jnp.int32, (tm, tm), 0)
    c_i = lax.broadcasted_iota(jnp.int32, (tm, tm), 1)
    tri = jnp.where(c_i < r_i, 1.0, 0.0).astype(BF16)
    before = jnp.dot(tri, sel.astype(BF16), preferred_element_type=F32) + cnt_ref[0:1, :]
    rank_ref[...] = jnp.where(sel > 0, before, -1.0).astype(jnp.int32)
    cnt_ref[...] = cnt_ref[...] + jnp.sum(sel, axis=0, keepdims=True)


def _router(h, w_router_pad):
    t = h.shape[0]
    tm = min(TM_ROUTE, t)
    row = lambda width: pl.BlockSpec((tm, width), lambda i: (i, 0))
    return pl.pallas_call(
        _router_kernel,
        grid=(t // tm,),
        in_specs=[row(D_MODEL), pl.BlockSpec(w_router_pad.shape, lambda i: (0, 0))],
        out_specs=[row(LANES), row(LANES), pl.BlockSpec((8, LANES), lambda i: (0, 0))],
        out_shape=[jax.ShapeDtypeStruct((t, LANES), F32), jax.ShapeDtypeStruct((t, LANES), jnp.int32),
                   jax.ShapeDtypeStruct((8, LANES), F32)],
        compiler_params=_cparams(("arbitrary",)),
        name="router",
    )(h, w_router_pad)


def _route_rows_kernel(starts_ref, gates_ref, rank_ref, dest_ref, g12_ref):
    tm = rank_ref.shape[0]
    lane = lax.broadcasted_iota(jnp.int32, (tm, LANES), 1)
    rank = rank_ref[...]
    sel = rank >= 0
    dest = starts_ref[...] + rank.astype(F32)
    first = jnp.min(jnp.where(sel, lane, LANES), axis=1, keepdims=True)
    second = jnp.max(jnp.where(sel, lane, -1), axis=1, keepdims=True)
    pick = lambda a, e: jnp.sum(jnp.where(lane == e, a, 0.0), axis=1, keepdims=True)
    col = lax.broadcasted_iota(jnp.int32, (tm, TOP_K), 1)
    dest_ref[...] = jnp.where(col == 0, pick(dest, first), pick(dest, second)).astype(jnp.int32)
    g12_ref[...] = jnp.where(col == 0, pick(gates_ref[...], first), pick(gates_ref[...], second))


def _route_rows(starts_pad, gates, rank):
    t = gates.shape[0]
    tm = min(TM_ROUTE, t)
    row = lambda width: pl.BlockSpec((tm, width), lambda i: (i, 0))
    return pl.pallas_call(
        _route_rows_kernel,
        grid=(t // tm,),
        in_specs=[pl.BlockSpec((1, LANES), lambda i: (0, 0)), row(LANES), row(LANES)],
        out_specs=[row(TOP_K), row(TOP_K)],
        out_shape=[jax.ShapeDtypeStruct((t, TOP_K), jnp.int32), jax.ShapeDtypeStruct((t, TOP_K), F32)],
        compiler_params=_cparams(("parallel",)),
        name="route_rows",
    )(starts_pad, gates, rank)


def _scatter_kernel(ends_ref, dest_ref, h_ref, xs_ref, zero_ref, sem):
    tm = h_ref.shape[0] * SUBLANES
    tile_groups = TM_EXP // SUBLANES

    def row_copy(g, s, k):
        return pltpu.make_async_copy(h_ref.at[g, :, pl.ds(s, 1), :],
                                     _row_of(xs_ref, dest_ref[0, 2 * (SUBLANES * g + s) + k]), sem)

    @pl.when(pl.program_id(0) == 0)
    def _():
        zero_ref[...] = jnp.zeros_like(zero_ref)
        copies = []
        for e in range(N_EXPERTS):
            start = pl.multiple_of(jnp.maximum(ends_ref[e] - TM_EXP, 0) // SUBLANES, tile_groups)
            cp = pltpu.make_async_copy(zero_ref, xs_ref.at[pl.ds(start, tile_groups)], sem)
            cp.start()
            copies.append(cp)
        for cp in copies:
            cp.wait()

        def zero_tail(i, _):
            cp = pltpu.make_async_copy(
                zero_ref, xs_ref.at[pl.ds(pl.multiple_of(i * tile_groups, tile_groups), tile_groups)], sem)
            cp.start()
            cp.wait()
            return 0

        lax.fori_loop(ends_ref[N_EXPERTS - 1] // TM_EXP, xs_ref.shape[0] // tile_groups, zero_tail, 0)

    def issue(g, _):
        for s in range(SUBLANES):
            row_copy(g, s, 0).start(priority=0)
            row_copy(g, s, 1).start(priority=1)
        return 0

    lax.fori_loop(0, tm // SUBLANES, issue, 0)

    def drain(g, _):
        for s in range(SUBLANES):
            row_copy(g, s, 0).wait()
            row_copy(g, s, 1).wait()
        return 0

    lax.fori_loop(0, tm // SUBLANES, drain, 0)


def _scatter_rows(h, dest2, ends, n_rows):
    t = h.shape[0] * SUBLANES
    tm = min(TM_ROW, t)
    nt = t // tm
    dest3 = dest2.reshape(nt, 1, 2 * tm)
    grid_spec = pltpu.PrefetchScalarGridSpec(
        num_scalar_prefetch=1,
        grid=(nt,),
        in_specs=[pl.BlockSpec((None, 1, 2 * tm), lambda i, ends: (i, 0, 0), memory_space=pltpu.SMEM),
                  pl.BlockSpec((tm // SUBLANES,) + ROW_TILES, lambda i, ends: (i, 0, 0, 0))],
        out_specs=pl.BlockSpec(memory_space=pl.ANY),
        scratch_shapes=[pltpu.VMEM((TM_EXP // SUBLANES,) + ROW_TILES, F32), pltpu.SemaphoreType.DMA(())],
    )
    return pl.pallas_call(
        _scatter_kernel,
        grid_spec=grid_spec,
        out_shape=jax.ShapeDtypeStruct((n_rows // SUBLANES,) + ROW_TILES, F32),
        compiler_params=_cparams(("arbitrary",)),
        name="scatter_rows",
    )(ends, dest3, h)


def _expert_kernel(te_ref, na_ref, xs_ref, wg_ref, wu_ref, wd_ref, ys_ref, acc_ref):
    i = pl.program_id(0)
    j = pl.program_id(1)

    @pl.when((i == 0) & (j == 0))
    def _():
        acc_ref[...] = jnp.zeros_like(acc_ref)

    @pl.when(i < na_ref[0])
    def _():
        xb = _from_row_tiles(xs_ref).astype(BF16)
        g = jnp.dot(xb, wg_ref[...], preferred_element_type=F32)
        u = jnp.dot(xb, wu_ref[...], preferred_element_type=F32)
        a = (g * jax.nn.sigmoid(g) * u).astype(BF16)
        acc = jnp.where(j > 0, acc_ref[...], 0.0) + jnp.dot(a, wd_ref[...], preferred_element_type=F32)
        acc_ref[...] = acc
        _to_row_tiles(ys_ref, acc)

    @pl.when(i >= na_ref[0])
    def _():
        ys_ref[...] = jnp.zeros_like(ys_ref)


def _expert_ffn(xs, tile_expert, n_active, wg, wu, wd):
    n_rows = xs.shape[0] * SUBLANES
    ff = wg.shape[2]
    tf = ff // 2
    nj = ff // tf
    nt = n_rows // TM_EXP

    def wj(i, j, na):
        return jnp.where(i < na[0], j, nj - 1)

    row = pl.BlockSpec((TM_EXP // SUBLANES,) + ROW_TILES, lambda i, j, te, na: (i, 0, 0, 0))
    grid_spec = pltpu.PrefetchScalarGridSpec(
        num_scalar_prefetch=2,
        grid=(nt, nj),
        in_specs=[row,
                  pl.BlockSpec((None, D_MODEL, tf), lambda i, j, te, na: (te[i], 0, wj(i, j, na))),
                  pl.BlockSpec((None, D_MODEL, tf), lambda i, j, te, na: (te[i], 0, wj(i, j, na))),
                  pl.BlockSpec((None, tf, D_MODEL), lambda i, j, te, na: (te[i], wj(i, j, na), 0))],
        out_specs=row,
        scratch_shapes=[pltpu.VMEM((TM_EXP, D_MODEL), F32)],
    )
    return pl.pallas_call(
        _expert_kernel,
        grid_spec=grid_spec,
        out_shape=jax.ShapeDtypeStruct((n_rows // SUBLANES,) + ROW_TILES, F32),
        compiler_params=_cparams(("arbitrary", "arbitrary")),
        name="expert_ffn",
    )(tile_expert, n_active, xs, wg, wu, wd)


def _combine_kernel(dest_ref, x_ref, g12_ref, gfin_ref, ys_ref, o_ref, buf_ref, sem, *, final):
    tm = x_ref.shape[0]

    def row_copy(g, s, k):
        return pltpu.make_async_copy(_row_of(ys_ref, dest_ref[0, 2 * (SUBLANES * g + s) + k]),
                                     buf_ref.at[k, g, :, pl.ds(s, 1), :], sem)

    def issue(g, _):
        for s in range(SUBLANES):
            row_copy(g, s, 0).start(priority=0)
            row_copy(g, s, 1).start(priority=1)
        return 0

    lax.fori_loop(0, tm // SUBLANES, issue, 0)

    def drain(g, _):
        for s in range(SUBLANES):
            row_copy(g, s, 0).wait()
            row_copy(g, s, 1).wait()
        return 0

    lax.fori_loop(0, tm // SUBLANES, drain, 0)
    g12 = g12_ref[...]
    y = x_ref[...] + g12[:, 0:1] * _from_row_tiles(buf_ref.at[0]) + g12[:, 1:2] * _from_row_tiles(buf_ref.at[1])
    o_ref[...] = _rms(y, gfin_ref[...], NORM_EPS) if final else y


def _combine(x2, ys, dest2, g12, g_final, final):
    t = x2.shape[0]
    tm = min(TM_ROW, t)
    nt = t // tm
    dest3 = dest2.reshape(nt, 1, 2 * tm)
    row = pl.BlockSpec((tm, D_MODEL), lambda i: (i, 0))
    return pl.pallas_call(
        functools.partial(_combine_kernel, final=final),
        grid=(nt,),
        in_specs=[pl.BlockSpec((None, 1, 2 * tm), lambda i: (i, 0, 0), memory_space=pltpu.SMEM),
                  row, pl.BlockSpec((tm, 2), lambda i: (i, 0)),
                  pl.BlockSpec((1, D_MODEL), lambda i: (0, 0)),
                  pl.BlockSpec(memory_space=pl.ANY)],
        out_specs=row,
        out_shape=jax.ShapeDtypeStruct((t, D_MODEL), F32),
        scratch_shapes=[pltpu.VMEM((2, tm // SUBLANES) + ROW_TILES, F32), pltpu.SemaphoreType.DMA(())],
        compiler_params=_cparams(("arbitrary",)),
        name="combine",
    )(dest3, x2, g12, g_final, ys)


def _final_norm_kernel(x_ref, g_ref, o_ref):
    o_ref[...] = _rms(x_ref[...], g_ref[...], NORM_EPS)


def _final_norm(x2, g):
    t = x2.shape[0]
    tm = min(TM_PREP, t)
    row = pl.BlockSpec((tm, D_MODEL), lambda i: (i, 0))
    return pl.pallas_call(
        _final_norm_kernel, grid=(t // tm,),
        in_specs=[row, pl.BlockSpec((1, D_MODEL), lambda i: (0, 0))], out_specs=row,
        out_shape=jax.ShapeDtypeStruct((t, D_MODEL), F32),
        compiler_params=_cparams(("parallel",)), name="final_norm",
    )(x2, g)


def _moe_layer(x2, h, h_tiles, router, wg, wu, wd, g_final, final):
    t = x2.shape[0]
    w_router_pad = jnp.zeros((D_MODEL, LANES), F32).at[:, :N_EXPERTS].set(router.astype(F32))
    w_hi = w_router_pad.astype(BF16)
    w_lo = (w_router_pad - w_hi.astype(F32)).astype(BF16)
    gates, rank, cnt = _router(h, jnp.concatenate([w_hi, w_lo], axis=1))
    counts = cnt[0, :N_EXPERTS].astype(jnp.int32)
    padded = ((counts + TM_EXP - 1) // TM_EXP) * TM_EXP
    ends = jnp.cumsum(padded)
    starts = ends - padded
    n_rows = TOP_K * t + N_EXPERTS * TM_EXP
    nt = n_rows // TM_EXP
    n_active = (ends[-1] // TM_EXP).astype(jnp.int32)
    tile_start = jnp.minimum(jnp.arange(nt, dtype=jnp.int32), n_active - 1) * TM_EXP
    tile_expert = jnp.minimum(jnp.sum(ends[None, :] <= tile_start[:, None], axis=1), N_EXPERTS - 1).astype(jnp.int32)
    starts_pad = jnp.zeros((1, LANES), F32).at[0, :N_EXPERTS].set(starts.astype(F32))
    dest2, g12 = _route_rows(starts_pad, gates, rank)
    xs = _scatter_rows(h_tiles, dest2, ends.astype(jnp.int32), n_rows)
    ys = _expert_ffn(xs, tile_expert, n_active.reshape(1), wg, wu, wd)
    return _combine(x2, ys, dest2, g12, g_final, final)


def kernel(x, positions, w_in, w_out, norm_mix, norm_ffn, da_lambda_q1, da_lambda_k1, da_lambda_q2,
           da_lambda_k2, da_subln, mla_q_norm, mla_w_uq, mla_kv_norm, mla_w_ukv, ffn_w_gate, ffn_w_up,
           ffn_w_down, moe_router, moe_w_gate, moe_w_up, moe_w_down, norm_final):
    batch, seq, d = x.shape
    depth = w_in.shape[0]
    t = batch * seq
    x2 = x.reshape(t, d).astype(F32)
    tabs = _rope_tables(positions)

    n_qk = 2 * DA_HEADS * 2 * DA_HEAD_DIM
    n_v = DA_HEADS * DA_V_DIM
    c0 = n_qk + n_v
    kv_per_head = MLA_NOPE_DIM + MLA_V_DIM

    for l in range(depth):
        wl = w_in[l]
        w_kr = jnp.zeros((d, LANES), F32).at[:, MLA_NOPE_DIM:MLA_QK_DIM].set(
            wl[:, c0 + MLA_Q_RANK + MLA_KV_RANK:])
        wc = jnp.concatenate([wl[:, c0:c0 + MLA_Q_RANK + MLA_KV_RANK], w_kr], axis=1)
        wuq = jnp.pad(mla_w_uq[l].reshape(MLA_Q_RANK, MLA_HEADS, MLA_QK_DIM),
                      ((0, 0), (0, 0), (0, LANES - MLA_QK_DIM))).reshape(MLA_Q_RANK, MLA_HEADS * LANES)
        wukv = mla_w_ukv[l].reshape(MLA_KV_RANK, MLA_HEADS, kv_per_head)
        wuk = jnp.pad(wukv[:, :, :MLA_NOPE_DIM],
                      ((0, 0), (0, 0), (0, LANES - MLA_NOPE_DIM))).reshape(MLA_KV_RANK, MLA_HEADS * LANES)
        wuv = wukv[:, :, MLA_NOPE_DIM:].reshape(MLA_KV_RANK, MLA_HEADS * MLA_V_DIM)
        w = dict(wqk=wl[:, :n_qk].astype(BF16), wv=wl[:, n_qk:c0].T.astype(BF16), wc=wc.astype(BF16),
                 gq=mla_q_norm[l].reshape(1, -1).astype(F32), wuq=wuq.astype(BF16),
                 gkv=mla_kv_norm[l].reshape(1, -1).astype(F32), wuk=wuk.astype(BF16), wuv=wuv.T.astype(BF16))

        da_q, da_k, da_vt, ml_q, ml_k, ml_vt = _prep(
            x2, norm_mix[l].reshape(1, d).astype(F32), w, tabs, batch, seq)

        lambda_init = 0.8 - 0.6 * float(np.exp(-0.3 * l))
        lam = (jnp.exp(jnp.sum(da_lambda_q1[l].astype(F32) * da_lambda_k1[l].astype(F32)))
               - jnp.exp(jnp.sum(da_lambda_q2[l].astype(F32) * da_lambda_k2[l].astype(F32)))
               + lambda_init).reshape(1).astype(F32)
        o_da = _attention(da_q, da_k, da_vt, batch, seq, 2, lam=lam,
                          subln=da_subln[l].reshape(DA_V_DIM, 1).astype(F32),
                          one_minus_lambda_init=1.0 - lambda_init)
        o_ml = _attention(ml_q, ml_k, ml_vt, batch, seq, 1)

        is_moe = l % 2 == 1
        wo = w_out[l].astype(BF16)
        x2, h, *h_tiles = _outproj(x2, o_da, o_ml, wo[:DA_WIDTH], wo[DA_WIDTH:],
                                   norm_ffn[l].reshape(1, d).astype(F32), F32 if is_moe else BF16, is_moe)
        i = l // 2
        last = l == depth - 1
        if not is_moe:
            x2 = _dense_ffn(x2, h, ffn_w_gate[i].astype(BF16), ffn_w_up[i].astype(BF16),
                            ffn_w_down[i].astype(BF16))
            if last:
                x2 = _final_norm(x2, norm_final.reshape(1, d).astype(F32))
        else:
            x2 = _moe_layer(x2, h, h_tiles[0], moe_router[i], moe_w_gate[i].astype(BF16),
                            moe_w_up[i].astype(BF16), moe_w_down[i].astype(BF16),
                            norm_final.reshape(1, d).astype(F32), last)
    return x2.reshape(batch, seq, d).astype(x.dtype)
```

```python
import functools
import math

import numpy as np
import jax
import jax.numpy as jnp
from jax import lax
from jax.experimental import pallas as pl
from jax.experimental.pallas import tpu as pltpu

D_MODEL = 1024
ROPE_THETA = 10000.0
NORM_EPS = 1e-6
SUBLN_EPS = 1e-5
DA_HEAD_DIM = 64
DA_V_DIM = 128
DA_HEADS = 4
DA_WIDTH = 512
MLA_HEADS = 4
MLA_V_DIM = 128
MLA_WIDTH = 512
MLA_Q_RANK = 256
MLA_KV_RANK = 128
MLA_NOPE_DIM = 64
MLA_ROPE_DIM = 32
MLA_QK_DIM = 96
N_EXPERTS = 8
TOP_K = 2
LOG2E = 1.4426950408889634

LANES = 128
SUBLANES = 8
VMEM_LIMIT = 56 * 1024 * 1024

TM_PREP = 512
PREP_SPLIT = 1
TQ = {1: 1024, 2: 1024}
ATTN_CHUNK = 256
SHIFT_KEYS = 64
VALUE_GROUP = 2048
ATTN_TRIP = 16
TM_FFN = 512
TM_EXP = 512
TM_ROW = 1024
TM_ROUTE = 512
ROW_DMA_UNROLL = 8

BF16 = jnp.bfloat16
F32 = jnp.float32


def _cparams(sem):
    return pltpu.CompilerParams(dimension_semantics=sem, vmem_limit_bytes=VMEM_LIMIT)


def _rms(x, g, eps):
    return x * lax.rsqrt(jnp.mean(x * x, axis=-1, keepdims=True) + eps) * g


def _rope_table_kernel(pos_ref, inv_ref, cos_da_ref, sin_da_ref, cos_ml_ref, sin_ml_ref):
    ang = pos_ref[...] * inv_ref[...]
    c, s = jnp.cos(ang), jnp.sin(ang)
    h_da, h_ml = DA_HEAD_DIM // 2, MLA_ROPE_DIM // 2
    c_da, s_da = c[:, :h_da], s[:, :h_da]
    c_ml, s_ml = c[:, h_da:h_da + h_ml], s[:, h_da:h_da + h_ml]
    tm = ang.shape[0]
    cos_da_ref[...] = jnp.concatenate([c_da] * (LANES // h_da), axis=1)
    sin_da_ref[...] = jnp.concatenate([-s_da, s_da] * (LANES // DA_HEAD_DIM), axis=1)
    pad = LANES - MLA_QK_DIM
    cos_ml_ref[...] = jnp.concatenate(
        [jnp.ones((tm, MLA_NOPE_DIM), F32), c_ml, c_ml, jnp.ones((tm, pad), F32)], axis=1)
    sin_ml_ref[...] = jnp.concatenate(
        [jnp.zeros((tm, MLA_NOPE_DIM), F32), -s_ml, s_ml, jnp.zeros((tm, pad), F32)], axis=1)


def _rope_tables(positions):
    t = positions.size
    tm = min(TM_PREP, t)
    pos = positions.reshape(t, 1).astype(F32)
    inv_da = 1.0 / (ROPE_THETA ** (jnp.arange(0, DA_HEAD_DIM, 2, dtype=F32) / DA_HEAD_DIM))
    inv_ml = 1.0 / (ROPE_THETA ** (jnp.arange(0, MLA_ROPE_DIM, 2, dtype=F32) / MLA_ROPE_DIM))
    inv = jnp.concatenate([inv_da, inv_ml, jnp.zeros((LANES - inv_da.size - inv_ml.size,), F32)])[None, :]
    row = pl.BlockSpec((tm, LANES), lambda i: (i, 0))
    out = jax.ShapeDtypeStruct((t, LANES), F32)
    return pl.pallas_call(
        _rope_table_kernel,
        grid=(t // tm,),
        in_specs=[pl.BlockSpec((tm, 1), lambda i: (i, 0)), pl.BlockSpec((1, LANES), lambda i: (0, 0))],
        out_specs=[row, row, row, row],
        out_shape=[out, out, out, out],
        compiler_params=_cparams(("parallel",)),
        name="rope_tables",
    )(pos, inv)


def _rotate_half_matrices():
    lane = np.arange(LANES)
    half_da, half_ml = DA_HEAD_DIM // 2, MLA_ROPE_DIM // 2
    partner_da = np.where(lane % DA_HEAD_DIM < half_da, lane + half_da, lane - half_da)
    r_da = np.zeros((LANES, LANES), np.float32)
    r_da[partner_da, lane] = 1.0
    in_rope = (lane >= MLA_NOPE_DIM) & (lane < MLA_QK_DIM)
    partner_ml = np.where(lane < MLA_NOPE_DIM + half_ml, lane + half_ml, lane - half_ml)
    r_ml = np.zeros((LANES, LANES), np.float32)
    r_ml[partner_ml[in_rope], lane[in_rope]] = 1.0
    pair = lambda r: jnp.asarray(np.kron(np.eye(2, dtype=np.float32), r), BF16)
    return pair(r_da), pair(r_ml)


def _rotary_pair(pair, rot_ref, cos, sin):
    rot = jnp.dot(pair.astype(BF16), rot_ref[...], preferred_element_type=F32)
    return pair * cos + rot * sin


def _prep_kernel(x_ref, g_ref, wqk_ref, wv_ref, wc_ref, gq_ref, wuq_ref, gkv_ref, wuk_ref, wuv_ref,
                 rda_ref, rml_ref, cda_ref, sda_ref, cml_ref, sml_ref,
                 daq_ref, dak_ref, davt_ref, mlq_ref, mlk_ref, mlvt_ref):
    tm = x_ref.shape[0]
    sub = tm // PREP_SPLIT
    pair_w = 2 * LANES
    q_scale = DA_HEAD_DIM ** -0.5 * LOG2E
    ml_scale = MLA_QK_DIM ** -0.5 * LOG2E

    for part in range(PREP_SPLIT):
        rows = slice(part * sub, (part + 1) * sub)
        hb = _rms(x_ref[rows, :], g_ref[...], NORM_EPS).astype(BF16)

        qk = jnp.dot(hb, wqk_ref[...], preferred_element_type=F32)
        cda = jnp.concatenate([cda_ref[rows, :]] * 2, axis=1)
        sda = jnp.concatenate([sda_ref[rows, :]] * 2, axis=1)
        for jp in range(DA_HEADS):
            r = _rotary_pair(qk[:, jp * pair_w:(jp + 1) * pair_w], rda_ref, cda, sda)
            if jp < DA_HEADS // 2:
                daq_ref[rows, jp * pair_w:(jp + 1) * pair_w] = (r * q_scale).astype(BF16)
            else:
                jj = jp - DA_HEADS // 2
                dak_ref[rows, jj * pair_w:(jj + 1) * pair_w] = r.astype(BF16)
        davt_ref[:, rows] = lax.dot_general(wv_ref[...], hb, (((1,), (1,)), ((), ())),
                                            preferred_element_type=F32).astype(BF16)

        c = jnp.dot(hb, wc_ref[...], preferred_element_type=F32)
        cml1, sml1 = cml_ref[rows, :], sml_ref[rows, :]
        cml = jnp.concatenate([cml1] * 2, axis=1)
        sml = jnp.concatenate([sml1] * 2, axis=1)
        cq = _rms(c[:, :MLA_Q_RANK], gq_ref[...], NORM_EPS).astype(BF16)
        qm = jnp.dot(cq, wuq_ref[...], preferred_element_type=F32)
        for jp in range(MLA_HEADS // 2):
            r = _rotary_pair(qm[:, jp * pair_w:(jp + 1) * pair_w], rml_ref, cml, sml)
            mlq_ref[rows, jp * pair_w:(jp + 1) * pair_w] = (r * ml_scale).astype(BF16)
        ckv = _rms(c[:, MLA_Q_RANK:MLA_Q_RANK + MLA_KV_RANK], gkv_ref[...], NORM_EPS).astype(BF16)
        kr = c[:, MLA_Q_RANK + MLA_KV_RANK:]
        kr_rot = jnp.dot(kr.astype(BF16), rml_ref[:LANES, :LANES], preferred_element_type=F32)
        kr = kr * cml1 + kr_rot * sml1
        kn = jnp.dot(ckv, wuk_ref[...], preferred_element_type=F32)
        for j in range(MLA_HEADS):
            mlk_ref[rows, j * LANES:(j + 1) * LANES] = (kn[:, j * LANES:(j + 1) * LANES] + kr).astype(BF16)
        mlvt_ref[:, rows] = lax.dot_general(wuv_ref[...], ckv, (((1,), (1,)), ((), ())),
                                            preferred_element_type=F32).astype(BF16)


def _prep(x2, g_mix, w, tabs, batch, seq):
    t = x2.shape[0]
    tm = min(TM_PREP, seq)
    nps = seq // tm
    row = lambda width: pl.BlockSpec((tm, width), lambda i: (i, 0))
    full = lambda a: pl.BlockSpec(a.shape, lambda i: (0,) * a.ndim)
    vt_spec = pl.BlockSpec((None, DA_WIDTH, tm), lambda i: (i // nps, 0, i % nps))
    tok = jax.ShapeDtypeStruct((t, DA_WIDTH), BF16)
    vts = jax.ShapeDtypeStruct((batch, DA_WIDTH, seq), BF16)
    weights = [w["wqk"], w["wv"], w["wc"], w["gq"], w["wuq"], w["gkv"], w["wuk"], w["wuv"],
               *_rotate_half_matrices()]
    return pl.pallas_call(
        _prep_kernel,
        grid=(t // tm,),
        in_specs=[row(D_MODEL), full(g_mix)] + [full(a) for a in weights] + [row(LANES)] * 4,
        out_specs=[row(DA_WIDTH), row(DA_WIDTH), vt_spec, row(DA_WIDTH), row(DA_WIDTH), vt_spec],
        out_shape=[tok, tok, vts, tok, tok, vts],
        compiler_params=_cparams(("parallel",)),
        name="prep",
    )(x2, g_mix, *weights, *tabs)


def _attn_kernel(*refs, n_maps, one_minus_lambda_init):
    if n_maps == 2:
        lam_ref, q_ref, k_ref, vt_ref, g_ref, o_ref = refs[:6]
    else:
        q_ref, k_ref, vt_ref, o_ref = refs[:4]
    p_ref, acc_ref, l_ref, m_ref, qm_ref = refs[-5:]
    seq = k_ref.shape[0]
    tq = q_ref.shape[0]
    tkc = ATTN_CHUNK
    n_chunks = seq // tkc
    q = q_ref[...]
    if n_maps == 2:
        lane = lax.broadcasted_iota(jnp.int32, q.shape, 1)
        zero = jnp.zeros_like(q)
        qm_ref[0] = jnp.where(lane < DA_HEAD_DIM, q, zero)
        qm_ref[1] = jnp.where(lane >= DA_HEAD_DIM, q, zero)
    else:
        qm_ref[0] = q

    def key_rows(c, rows):
        return k_ref[pl.ds(pl.multiple_of(c * tkc, rows), rows), :]

    def scores(k, mp):
        return lax.dot_general(k, qm_ref[mp], (((1,), (1,)), ((), ())), preferred_element_type=F32)

    def sublane_sums(p):
        return jnp.sum(p.reshape(p.shape[0] // SUBLANES, SUBLANES, tq), axis=0)

    for mp in range(n_maps):
        m_ref[mp] = jnp.max(scores(key_rows(0, SHIFT_KEYS), mp), axis=0, keepdims=True)
    l_ref[...] = jnp.zeros(l_ref.shape, F32)

    per_trip = math.gcd(ATTN_TRIP, n_chunks)

    def trip(t, _):
        for off in range(per_trip):
            c = t * per_trip + off
            k = key_rows(c, tkc)
            for mp in range(n_maps):
                p = jnp.exp2(scores(k, mp) - m_ref[mp])
                l_ref[mp] = l_ref[mp] + sublane_sums(p)
                p_ref[mp, pl.ds(pl.multiple_of(c * tkc, tkc), tkc), :] = p.astype(BF16)
        return 0

    lax.fori_loop(0, n_chunks // per_trip, trip, 0)

    def denominator(mp):
        return jnp.sum(l_ref[mp], axis=0, keepdims=True)

    def normalized(mp):
        return acc_ref[mp] / denominator(mp)

    if n_maps == 2:
        ratio = (lam_ref[0] * denominator(0) / denominator(1)).astype(BF16)
        group = math.gcd(VALUE_GROUP, seq)
        acc = jnp.zeros(acc_ref.shape[1:], F32)
        for k0 in range(0, seq, group):
            w = p_ref[0, k0:k0 + group, :] - ratio * p_ref[1, k0:k0 + group, :]
            acc = acc + jnp.dot(vt_ref[:, k0:k0 + group], w, preferred_element_type=F32)
        acc_ref[0] = acc / denominator(0)
    else:
        acc_ref[0] = jnp.dot(vt_ref[...], p_ref[0], preferred_element_type=F32) / denominator(0)

    def non_finite(x):
        return jnp.max(jnp.where(jnp.isfinite(x), 0.0, 1.0)) > 0.0

    @pl.when(non_finite(acc_ref[0]) | non_finite(l_ref[...]))
    def _():
        m_ref[...] = jnp.full(m_ref.shape, -jnp.inf, F32)
        l_ref[...] = jnp.zeros(l_ref.shape, F32)
        acc_ref[...] = jnp.zeros(acc_ref.shape, F32)

        def chunk(c, _):
            vt = vt_ref[:, pl.ds(pl.multiple_of(c * tkc, tkc), tkc)]
            k = key_rows(c, tkc)
            for mp in range(n_maps):
                s = scores(k, mp)
                m_old = m_ref[mp]
                m_new = jnp.maximum(m_old, jnp.max(s, axis=0, keepdims=True))
                alpha = jnp.exp2(m_old - m_new)
                p = jnp.exp2(s - m_new)
                l_ref[mp] = alpha * l_ref[mp] + sublane_sums(p)
                acc_ref[mp] = alpha * acc_ref[mp] + jnp.dot(vt, p.astype(BF16), preferred_element_type=F32)
                m_ref[mp] = m_new
            return 0

        lax.fori_loop(0, n_chunks, chunk, 0)
        if n_maps == 2:
            acc_ref[0] = normalized(0) - lam_ref[0] * normalized(1)
        else:
            acc_ref[0] = normalized(0)

    o = acc_ref[0]
    if n_maps == 2:
        ms = jnp.mean(o * o, axis=0, keepdims=True)
        o = o * lax.rsqrt(ms + SUBLN_EPS) * g_ref[...] * one_minus_lambda_init
    o_ref[...] = o.T.astype(o_ref.dtype)


def _attention(q, k, vt, batch, seq, n_maps, lam=None, subln=None, one_minus_lambda_init=1.0):
    t = q.shape[0]
    tq = min(TQ[n_maps], seq)
    nq = seq // tq
    heads = q.shape[1] // LANES
    q_spec = pl.BlockSpec((tq, LANES), lambda b, h, i: (b * nq + i, h))
    k_spec = pl.BlockSpec((seq, LANES), lambda b, h, i: (b, h))
    vt_spec = pl.BlockSpec((None, DA_V_DIM, seq), lambda b, h, i: (b, h, 0))
    in_specs = [q_spec, k_spec, vt_spec]
    args = [q, k, vt]
    if n_maps == 2:
        in_specs = [pl.BlockSpec(memory_space=pltpu.SMEM)] + in_specs + [
            pl.BlockSpec((DA_V_DIM, 1), lambda b, h, i: (0, 0))]
        args = [lam] + args + [subln]
    scratch = [pltpu.VMEM((n_maps, seq, tq), BF16),
               pltpu.VMEM((n_maps, DA_V_DIM, tq), F32),
               pltpu.VMEM((n_maps, SUBLANES, tq), F32),
               pltpu.VMEM((n_maps, 1, tq), F32),
               pltpu.VMEM((n_maps, tq, LANES), BF16)]
    return pl.pallas_call(
        functools.partial(_attn_kernel, n_maps=n_maps, one_minus_lambda_init=one_minus_lambda_init),
        grid=(batch, heads, nq),
        in_specs=in_specs,
        out_specs=q_spec,
        scratch_shapes=scratch,
        out_shape=jax.ShapeDtypeStruct((t, q.shape[1]), BF16),
        compiler_params=_cparams(("parallel", "parallel", "parallel")),
        name="diff_attn" if n_maps == 2 else "mla_attn",
    )(*args)


def _to_row_tiles(ref, val):
    groups = val.shape[0] // SUBLANES
    for j in range(D_MODEL // LANES):
        ref[:, j] = val[:, j * LANES:(j + 1) * LANES].reshape(groups, SUBLANES, LANES)


def _from_row_tiles(ref):
    rows = ref.shape[0] * SUBLANES
    return jnp.concatenate([ref[:, j].reshape(rows, LANES) for j in range(D_MODEL // LANES)], axis=1)


def _row_of(ref, r):
    return ref.at[lax.shift_right_logical(r, 3), :, pl.ds(jnp.bitwise_and(r, SUBLANES - 1), 1), :]


ROW_TILES = (D_MODEL // LANES, SUBLANES, LANES)


def _outproj_kernel(x_ref, oda_ref, oml_ref, wa_ref, wb_ref, g_ref, xo_ref, h_ref, *ht_ref):
    y = x_ref[...] + jnp.dot(oda_ref[...], wa_ref[...], preferred_element_type=F32)
    y = y + jnp.dot(oml_ref[...], wb_ref[...], preferred_element_type=F32)
    xo_ref[...] = y
    h = _rms(y, g_ref[...], NORM_EPS)
    h_ref[...] = h.astype(h_ref.dtype)
    if ht_ref:
        _to_row_tiles(ht_ref[0], h)


def _outproj(x2, o_da, o_ml, w_a, w_b, g_ffn, h_dtype, row_tiles):
    t = x2.shape[0]
    tm = min(TM_PREP, t)
    row = lambda width: pl.BlockSpec((tm, width), lambda i: (i, 0))
    full = lambda a: pl.BlockSpec(a.shape, lambda i: (0,) * a.ndim)
    out_specs = [row(D_MODEL), row(D_MODEL)]
    out_shape = [jax.ShapeDtypeStruct((t, D_MODEL), F32), jax.ShapeDtypeStruct((t, D_MODEL), h_dtype)]
    if row_tiles:
        out_specs.append(pl.BlockSpec((tm // SUBLANES,) + ROW_TILES, lambda i: (i, 0, 0, 0)))
        out_shape.append(jax.ShapeDtypeStruct((t // SUBLANES,) + ROW_TILES, F32))
    return pl.pallas_call(
        _outproj_kernel,
        grid=(t // tm,),
        in_specs=[row(D_MODEL), row(DA_WIDTH), row(MLA_WIDTH), full(w_a), full(w_b), full(g_ffn)],
        out_specs=out_specs,
        out_shape=out_shape,
        compiler_params=_cparams(("parallel",)),
        name="outproj",
    )(x2, o_da, o_ml, w_a, w_b, g_ffn)


def _ffn_kernel(x_ref, h_ref, wg_ref, wu_ref, wd_ref, o_ref):
    h = h_ref[...]
    g = jnp.dot(h, wg_ref[...], preferred_element_type=F32)
    u = jnp.dot(h, wu_ref[...], preferred_element_type=F32)
    a = (g * jax.nn.sigmoid(g) * u).astype(BF16)
    o_ref[...] = x_ref[...] + jnp.dot(a, wd_ref[...], preferred_element_type=F32)


def _dense_ffn(x2, h, wg, wu, wd):
    t = x2.shape[0]
    tm = min(TM_FFN, t)
    row = pl.BlockSpec((tm, D_MODEL), lambda i: (i, 0))
    res = lambda a: pl.BlockSpec(a.shape, lambda i: (0, 0), pipeline_mode=pl.Buffered(1))
    return pl.pallas_call(
        _ffn_kernel,
        grid=(t // tm,),
        in_specs=[row, row, res(wg), res(wu), res(wd)],
        out_specs=row,
        out_shape=jax.ShapeDtypeStruct((t, D_MODEL), F32),
        compiler_params=_cparams(("parallel",)),
        name="dense_ffn",
    )(x2, h, wg, wu, wd)


def _router_kernel(h_ref, wr_ref, gates_ref, rank_ref, cnt_ref):
    tm = h_ref.shape[0]

    @pl.when(pl.program_id(0) == 0)
    def _():
        cnt_ref[...] = jnp.zeros_like(cnt_ref)

    lane = lax.broadcasted_iota(jnp.int32, (tm, LANES), 1)
    h = h_ref[...]
    h_hi = h.astype(BF16)
    h_lo = (h - h_hi.astype(F32)).astype(BF16)
    hi = jnp.dot(h_hi, wr_ref[...], preferred_element_type=F32)
    lo = jnp.dot(h_lo, wr_ref[...], preferred_element_type=F32)
    logits = hi[:, :LANES] + hi[:, LANES:] + lo[:, :LANES]
    neg = jnp.float32(-jnp.inf)
    logits = jnp.where(lane < N_EXPERTS, logits, neg)
    v1 = jnp.max(logits, axis=1, keepdims=True)
    i1 = jnp.min(jnp.where(logits == v1, lane, LANES), axis=1, keepdims=True)
    sel1 = lane == i1
    rest = jnp.where(sel1, neg, logits)
    v2 = jnp.max(rest, axis=1, keepdims=True)
    i2 = jnp.min(jnp.where(rest == v2, lane, LANES), axis=1, keepdims=True)
    sel2 = lane == i2
    e = jnp.exp(v2 - v1)
    g1 = 1.0 / (1.0 + e)
    g2 = e / (1.0 + e)
    gates_ref[...] = jnp.where(sel1, g1, jnp.where(sel2, g2, 0.0))
    sel = jnp.where(sel1 | sel2, 1.0, 0.0)
    r_i = lax.broadcasted_iota(jnp.int32, (tm, tm), 0)
    c_i = lax.broadcasted_iota(jnp.int32, (tm, tm), 1)
    tri = jnp.where(c_i < r_i, 1.0, 0.0).astype(BF16)
    before = jnp.dot(tri, sel.astype(BF16), preferred_element_type=F32) + cnt_ref[0:1, :]
    rank_ref[...] = jnp.where(sel > 0, before, -1.0).astype(jnp.int32)
    cnt_ref[...] = cnt_ref[...] + jnp.sum(sel, axis=0, keepdims=True)


def _router(h, w_router_pad):
    t = h.shape[0]
    tm = min(TM_ROUTE, t)
    row = lambda width: pl.BlockSpec((tm, width), lambda i: (i, 0))
    return pl.pallas_call(
        _router_kernel,
        grid=(t // tm,),
        in_specs=[row(D_MODEL), pl.BlockSpec(w_router_pad.shape, lambda i: (0, 0))],
        out_specs=[row(LANES), row(LANES), pl.BlockSpec((8, LANES), lambda i: (0, 0))],
        out_shape=[jax.ShapeDtypeStruct((t, LANES), F32), jax.ShapeDtypeStruct((t, LANES), jnp.int32),
                   jax.ShapeDtypeStruct((8, LANES), F32)],
        compiler_params=_cparams(("arbitrary",)),
        name="router",
    )(h, w_router_pad)


def _route_rows_kernel(starts_ref, gates_ref, rank_ref, dest_ref, g12_ref):
    tm = rank_ref.shape[0]
    lane = lax.broadcasted_iota(jnp.int32, (tm, LANES), 1)
    rank = rank_ref[...]
    sel = rank >= 0
    dest = starts_ref[...] + rank.astype(F32)
    first = jnp.min(jnp.where(sel, lane, LANES), axis=1, keepdims=True)
    second = jnp.max(jnp.where(sel, lane, -1), axis=1, keepdims=True)
    pick = lambda a, e: jnp.sum(jnp.where(lane == e, a, 0.0), axis=1, keepdims=True)
    col = lax.broadcasted_iota(jnp.int32, (tm, TOP_K), 1)
    dest_ref[...] = jnp.where(col == 0, pick(dest, first), pick(dest, second)).astype(jnp.int32)
    g12_ref[...] = jnp.where(col == 0, pick(gates_ref[...], first), pick(gates_ref[...], second))


def _route_rows(starts_pad, gates, rank):
    t = gates.shape[0]
    tm = min(TM_ROUTE, t)
    row = lambda width: pl.BlockSpec((tm, width), lambda i: (i, 0))
    return pl.pallas_call(
        _route_rows_kernel,
        grid=(t // tm,),
        in_specs=[pl.BlockSpec((1, LANES), lambda i: (0, 0)), row(LANES), row(LANES)],
        out_specs=[row(TOP_K), row(TOP_K)],
        out_shape=[jax.ShapeDtypeStruct((t, TOP_K), jnp.int32), jax.ShapeDtypeStruct((t, TOP_K), F32)],
        compiler_params=_cparams(("parallel",)),
        name="route_rows",
    )(starts_pad, gates, rank)


def _scatter_kernel(ends_ref, dest_ref, h_ref, xs_ref, zero_ref, sem):
    tm = h_ref.shape[0] * SUBLANES
    tile_groups = TM_EXP // SUBLANES

    def row_copy(g, s, k):
        return pltpu.make_async_copy(h_ref.at[g, :, pl.ds(s, 1), :],
                                     _row_of(xs_ref, dest_ref[0, 2 * (SUBLANES * g + s) + k]), sem)

    @pl.when(pl.program_id(0) == 0)
    def _():
        zero_ref[...] = jnp.zeros_like(zero_ref)
        copies = []
        for e in range(N_EXPERTS):
            start = pl.multiple_of(jnp.maximum(ends_ref[e] - TM_EXP, 0) // SUBLANES, tile_groups)
            cp = pltpu.make_async_copy(zero_ref, xs_ref.at[pl.ds(start, tile_groups)], sem)
            cp.start()
            copies.append(cp)
        for cp in copies:
            cp.wait()

        def zero_tail(i, _):
            cp = pltpu.make_async_copy(
                zero_ref, xs_ref.at[pl.ds(pl.multiple_of(i * tile_groups, tile_groups), tile_groups)], sem)
            cp.start()
            cp.wait()
            return 0

        lax.fori_loop(ends_ref[N_EXPERTS - 1] // TM_EXP, xs_ref.shape[0] // tile_groups, zero_tail, 0)

    def issue(g, _):
        for s in range(SUBLANES):
            row_copy(g, s, 0).start(priority=0)
            row_copy(g, s, 1).start(priority=1)
        return 0

    lax.fori_loop(0, tm // SUBLANES, issue, 0)

    def drain(g, _):
        for s in range(SUBLANES):
            row_copy(g, s, 0).wait()
            row_copy(g, s, 1).wait()
        return 0

    lax.fori_loop(0, tm // SUBLANES, drain, 0)


def _scatter_rows(h, dest2, ends, n_rows):
    t = h.shape[0] * SUBLANES
    tm = min(TM_ROW, t)
    nt = t // tm
    dest3 = dest2.reshape(nt, 1, 2 * tm)
    grid_spec = pltpu.PrefetchScalarGridSpec(
        num_scalar_prefetch=1,
        grid=(nt,),
        in_specs=[pl.BlockSpec((None, 1, 2 * tm), lambda i, ends: (i, 0, 0), memory_space=pltpu.SMEM),
                  pl.BlockSpec((tm // SUBLANES,) + ROW_TILES, lambda i, ends: (i, 0, 0, 0))],
        out_specs=pl.BlockSpec(memory_space=pl.ANY),
        scratch_shapes=[pltpu.VMEM((TM_EXP // SUBLANES,) + ROW_TILES, F32), pltpu.SemaphoreType.DMA(())],
    )
    return pl.pallas_call(
        _scatter_kernel,
        grid_spec=grid_spec,
        out_shape=jax.ShapeDtypeStruct((n_rows // SUBLANES,) + ROW_TILES, F32),
        compiler_params=_cparams(("arbitrary",)),
        name="scatter_rows",
    )(ends, dest3, h)


def _expert_kernel(te_ref, na_ref, xs_ref, wg_ref, wu_ref, wd_ref, ys_ref, acc_ref):
    i = pl.program_id(0)
    j = pl.program_id(1)

    @pl.when((i == 0) & (j == 0))
    def _():
        acc_ref[...] = jnp.zeros_like(acc_ref)

    @pl.when(i < na_ref[0])
    def _():
        xb = _from_row_tiles(xs_ref).astype(BF16)
        g = jnp.dot(xb, wg_ref[...], preferred_element_type=F32)
        u = jnp.dot(xb, wu_ref[...], preferred_element_type=F32)
        a = (g * jax.nn.sigmoid(g) * u).astype(BF16)
        acc = jnp.where(j > 0, acc_ref[...], 0.0) + jnp.dot(a, wd_ref[...], preferred_element_type=F32)
        acc_ref[...] = acc
        _to_row_tiles(ys_ref, acc)

    @pl.when(i >= na_ref[0])
    def _():
        ys_ref[...] = jnp.zeros_like(ys_ref)


def _expert_ffn(xs, tile_expert, n_active, wg, wu, wd):
    n_rows = xs.shape[0] * SUBLANES
    ff = wg.shape[2]
    tf = ff // 2
    nj = ff // tf
    nt = n_rows // TM_EXP

    def wj(i, j, na):
        return jnp.where(i < na[0], j, nj - 1)

    row = pl.BlockSpec((TM_EXP // SUBLANES,) + ROW_TILES, lambda i, j, te, na: (i, 0, 0, 0))
    grid_spec = pltpu.PrefetchScalarGridSpec(
        num_scalar_prefetch=2,
        grid=(nt, nj),
        in_specs=[row,
                  pl.BlockSpec((None, D_MODEL, tf), lambda i, j, te, na: (te[i], 0, wj(i, j, na))),
                  pl.BlockSpec((None, D_MODEL, tf), lambda i, j, te, na: (te[i], 0, wj(i, j, na))),
                  pl.BlockSpec((None, tf, D_MODEL), lambda i, j, te, na: (te[i], wj(i, j, na), 0))],
        out_specs=row,
        scratch_shapes=[pltpu.VMEM((TM_EXP, D_MODEL), F32)],
    )
    return pl.pallas_call(
        _expert_kernel,
        grid_spec=grid_spec,
        out_shape=jax.ShapeDtypeStruct((n_rows // SUBLANES,) + ROW_TILES, F32),
        compiler_params=_cparams(("arbitrary", "arbitrary")),
        name="expert_ffn",
    )(tile_expert, n_active, xs, wg, wu, wd)


def _combine_kernel(dest_ref, x_ref, g12_ref, gfin_ref, ys_ref, o_ref, buf_ref, sem, *, final):
    tm = x_ref.shape[0]

    def row_copy(g, s, k):
        return pltpu.make_async_copy(_row_of(ys_ref, dest_ref[0, 2 * (SUBLANES * g + s) + k]),
                                     buf_ref.at[k, g, :, pl.ds(s, 1), :], sem)

    def issue(g, _):
        for s in range(SUBLANES):
            row_copy(g, s, 0).start(priority=0)
            row_copy(g, s, 1).start(priority=1)
        return 0

    lax.fori_loop(0, tm // SUBLANES, issue, 0)

    def drain(g, _):
        for s in range(SUBLANES):
            row_copy(g, s, 0).wait()
            row_copy(g, s, 1).wait()
        return 0

    lax.fori_loop(0, tm // SUBLANES, drain, 0)
    g12 = g12_ref[...]
    y = x_ref[...] + g12[:, 0:1] * _from_row_tiles(buf_ref.at[0]) + g12[:, 1:2] * _from_row_tiles(buf_ref.at[1])
    o_ref[...] = _rms(y, gfin_ref[...], NORM_EPS) if final else y


def _combine(x2, ys, dest2, g12, g_final, final):
    t = x2.shape[0]
    tm = min(TM_ROW, t)
    nt = t // tm
    dest3 = dest2.reshape(nt, 1, 2 * tm)
    row = pl.BlockSpec((tm, D_MODEL), lambda i: (i, 0))
    return pl.pallas_call(
        functools.partial(_combine_kernel, final=final),
        grid=(nt,),
        in_specs=[pl.BlockSpec((None, 1, 2 * tm), lambda i: (i, 0, 0), memory_space=pltpu.SMEM),
                  row, pl.BlockSpec((tm, 2), lambda i: (i, 0)),
                  pl.BlockSpec((1, D_MODEL), lambda i: (0, 0)),
                  pl.BlockSpec(memory_space=pl.ANY)],
        out_specs=row,
        out_shape=jax.ShapeDtypeStruct((t, D_MODEL), F32),
        scratch_shapes=[pltpu.VMEM((2, tm // SUBLANES) + ROW_TILES, F32), pltpu.SemaphoreType.DMA(())],
        compiler_params=_cparams(("arbitrary",)),
        name="combine",
    )(dest3, x2, g12, g_final, ys)


def _final_norm_kernel(x_ref, g_ref, o_ref):
    o_ref[...] = _rms(x_ref[...], g_ref[...], NORM_EPS)


def _final_norm(x2, g):
    t = x2.shape[0]
    tm = min(TM_PREP, t)
    row = pl.BlockSpec((tm, D_MODEL), lambda i: (i, 0))
    return pl.pallas_call(
        _final_norm_kernel, grid=(t // tm,),
        in_specs=[row, pl.BlockSpec((1, D_MODEL), lambda i: (0, 0))], out_specs=row,
        out_shape=jax.ShapeDtypeStruct((t, D_MODEL), F32),
        compiler_params=_cparams(("parallel",)), name="final_norm",
    )(x2, g)


def _moe_layer(x2, h, h_tiles, router, wg, wu, wd, g_final, final):
    t = x2.shape[0]
    w_router_pad = jnp.zeros((D_MODEL, LANES), F32).at[:, :N_EXPERTS].set(router.astype(F32))
    w_hi = w_router_pad.astype(BF16)
    w_lo = (w_router_pad - w_hi.astype(F32)).astype(BF16)
    gates, rank, cnt = _router(h, jnp.concatenate([w_hi, w_lo], axis=1))
    counts = cnt[0, :N_EXPERTS].astype(jnp.int32)
    padded = ((counts + TM_EXP - 1) // TM_EXP) * TM_EXP
    ends = jnp.cumsum(padded)
    starts = ends - padded
    n_rows = TOP_K * t + N_EXPERTS * TM_EXP
    nt = n_rows // TM_EXP
    n_active = (ends[-1] // TM_EXP).astype(jnp.int32)
    tile_start = jnp.minimum(jnp.arange(nt, dtype=jnp.int32), n_active - 1) * TM_EXP
    tile_expert = jnp.minimum(jnp.sum(ends[None, :] <= tile_start[:, None], axis=1), N_EXPERTS - 1).astype(jnp.int32)
    starts_pad = jnp.zeros((1, LANES), F32).at[0, :N_EXPERTS].set(starts.astype(F32))
    dest2, g12 = _route_rows(starts_pad, gates, rank)
    xs = _scatter_rows(h_tiles, dest2, ends.astype(jnp.int32), n_rows)
    ys = _expert_ffn(xs, tile_expert, n_active.reshape(1), wg, wu, wd)
    return _combine(x2, ys, dest2, g12, g_final, final)


def kernel(x, positions, w_in, w_out, norm_mix, norm_ffn, da_lambda_q1, da_lambda_k1, da_lambda_q2,
           da_lambda_k2, da_subln, mla_q_norm, mla_w_uq, mla_kv_norm, mla_w_ukv, ffn_w_gate, ffn_w_up,
           ffn_w_down, moe_router, moe_w_gate, moe_w_up, moe_w_down, norm_final):
    batch, seq, d = x.shape
    depth = w_in.shape[0]
    t = batch * seq
    x2 = x.reshape(t, d).astype(F32)
    tabs = _rope_tables(positions)

    n_qk = 2 * DA_HEADS * 2 * DA_HEAD_DIM
    n_v = DA_HEADS * DA_V_DIM
    c0 = n_qk + n_v
    kv_per_head = MLA_NOPE_DIM + MLA_V_DIM

    for l in range(depth):
        wl = w_in[l]
        w_kr = jnp.zeros((d, LANES), F32).at[:, MLA_NOPE_DIM:MLA_QK_DIM].set(
            wl[:, c0 + MLA_Q_RANK + MLA_KV_RANK:])
        wc = jnp.concatenate([wl[:, c0:c0 + MLA_Q_RANK + MLA_KV_RANK], w_kr], axis=1)
        wuq = jnp.pad(mla_w_uq[l].reshape(MLA_Q_RANK, MLA_HEADS, MLA_QK_DIM),
                      ((0, 0), (0, 0), (0, LANES - MLA_QK_DIM))).reshape(MLA_Q_RANK, MLA_HEADS * LANES)
        wukv = mla_w_ukv[l].reshape(MLA_KV_RANK, MLA_HEADS, kv_per_head)
        wuk = jnp.pad(wukv[:, :, :MLA_NOPE_DIM],
                      ((0, 0), (0, 0), (0, LANES - MLA_NOPE_DIM))).reshape(MLA_KV_RANK, MLA_HEADS * LANES)
        wuv = wukv[:, :, MLA_NOPE_DIM:].reshape(MLA_KV_RANK, MLA_HEADS * MLA_V_DIM)
        w = dict(wqk=wl[:, :n_qk].astype(BF16), wv=wl[:, n_qk:c0].T.astype(BF16), wc=wc.astype(BF16),
                 gq=mla_q_norm[l].reshape(1, -1).astype(F32), wuq=wuq.astype(BF16),
                 gkv=mla_kv_norm[l].reshape(1, -1).astype(F32), wuk=wuk.astype(BF16), wuv=wuv.T.astype(BF16))

        da_q, da_k, da_vt, ml_q, ml_k, ml_vt = _prep(
            x2, norm_mix[l].reshape(1, d).astype(F32), w, tabs, batch, seq)

        lambda_init = 0.8 - 0.6 * float(np.exp(-0.3 * l))
        lam = (jnp.exp(jnp.sum(da_lambda_q1[l].astype(F32) * da_lambda_k1[l].astype(F32)))
               - jnp.exp(jnp.sum(da_lambda_q2[l].astype(F32) * da_lambda_k2[l].astype(F32)))
               + lambda_init).reshape(1).astype(F32)
        o_da = _attention(da_q, da_k, da_vt, batch, seq, 2, lam=lam,
                          subln=da_subln[l].reshape(DA_V_DIM, 1).astype(F32),
                          one_minus_lambda_init=1.0 - lambda_init)
        o_ml = _attention(ml_q, ml_k, ml_vt, batch, seq, 1)

        is_moe = l % 2 == 1
        wo = w_out[l].astype(BF16)
        x2, h, *h_tiles = _outproj(x2, o_da, o_ml, wo[:DA_WIDTH], wo[DA_WIDTH:],
                                   norm_ffn[l].reshape(1, d).astype(F32), F32 if is_moe else BF16, is_moe)
        i = l // 2
        last = l == depth - 1
        if not is_moe:
            x2 = _dense_ffn(x2, h, ffn_w_gate[i].astype(BF16), ffn_w_up[i].astype(BF16),
                            ffn_w_down[i].astype(BF16))
            if last:
                x2 = _final_norm(x2, norm_final.reshape(1, d).astype(F32))
        else:
            x2 = _moe_layer(x2, h, h_tiles[0], moe_router[i], moe_w_gate[i].astype(BF16),
                            moe_w_up[i].astype(BF16), moe_w_down[i].astype(BF16),
                            norm_final.reshape(1, d).astype(F32), last)
    return x2.reshape(batch, seq, d).astype(x.dtype)
```
